```python
import math
import jax, jax.numpy as jnp
from jax import lax
import numpy as np

D_MODEL = 1024
BATCH = 2
SEQ = 8192
DEPTH = 1

GRID_W = 64
CTX_LEN = 256
RET_HEADS = 8
RET_QK_DIM = 64
RET_V_DIM = 128
RET_QK_WIDTH = RET_HEADS * RET_QK_DIM
RET_V_WIDTH = RET_HEADS * RET_V_DIM
CHUNK = 128
POOL_WINDOWS = (2, 4, 8, 16)
POOL_GROUPS = len(POOL_WINDOWS)
POOL_GROUP_DIM = 128
POOL_WIDTH = POOL_GROUPS * POOL_GROUP_DIM
D_FF = 2816
CONV_K = 3
N_MOD = 6
LN_EPS = 1e-6
DEEPNORM_ALPHA = (2.0 * DEPTH) ** 0.25
DEEPNORM_BETA = (8.0 * DEPTH) ** -0.25
IN_SIZES = (RET_QK_WIDTH, RET_V_WIDTH, RET_QK_WIDTH, RET_V_WIDTH, POOL_WIDTH, D_MODEL, D_MODEL)
IN_WIDTH = sum(IN_SIZES)
KV_COLS = RET_QK_WIDTH + RET_V_WIDTH

kernel_name = "hybrid_retention_pool_convffn_dit"


def _layer_norm(x):
    xf = x.astype(jnp.float32)
    mu = jnp.mean(xf, axis=-1, keepdims=True)
    var = jnp.mean(jnp.square(xf - mu), axis=-1, keepdims=True)
    return ((xf - mu) * lax.rsqrt(var + LN_EPS)).astype(x.dtype)


def _modulate(x, shift, scale):
    return _layer_norm(x) * (1.0 + scale) + shift


def _post_norm(z, g, b):
    return _layer_norm(z) * g + b


def _split_in(proj):
    offs = np.cumsum((0,) + IN_SIZES)
    return [proj[..., int(offs[i]):int(offs[i + 1])] for i in range(len(IN_SIZES))]


def _ret_kv(k, v):
    kh = k.reshape(k.shape[0], k.shape[1], RET_HEADS, RET_QK_DIM).astype(jnp.float32) * RET_QK_DIM ** -0.5
    vh = v.reshape(v.shape[0], v.shape[1], RET_HEADS, RET_V_DIM).astype(jnp.float32)
    return kh, vh


def _context_states(kh, vh, lg):
    Lc = kh.shape[1]
    pos = jnp.arange(Lc, dtype=jnp.float32)
    w_f = jnp.exp((Lc - 1 - pos)[:, None] * lg[0][None, :])
    w_b = jnp.exp(pos[:, None] * lg[1][None, :])
    s_f = jnp.einsum('bjhd,jh,bjhe->bhde', kh, w_f, vh)
    s_b = jnp.einsum('bjhd,jh,bjhe->bhde', kh, w_b, vh)
    return s_f, s_b


def _retention_scan(q, k, v, log_gamma, s0):
    B, L, H, dk = q.shape
    dv = v.shape[-1]
    n = L // CHUNK
    qc = q.reshape(B, n, CHUNK, H, dk)
    kc = k.reshape(B, n, CHUNK, H, dk)
    vc = v.reshape(B, n, CHUNK, H, dv)
    pos = jnp.arange(CHUNK, dtype=jnp.float32)
    diff = pos[:, None] - pos[None, :]
    decay_in = jnp.where(diff[None] >= 0,
                         jnp.exp(jnp.maximum(diff, 0.0)[None] * log_gamma[:, None, None]), 0.0)
    scores = jnp.einsum('bnihd,bnjhd->bnhij', qc, kc) * decay_in
    intra = jnp.einsum('bnhij,bnjhe->bnihe', scores, vc)
    q_dec = jnp.exp((pos + 1.0)[:, None] * log_gamma[None, :])
    k_dec = jnp.exp((CHUNK - 1.0 - pos)[:, None] * log_gamma[None, :])
    chunk_dec = jnp.exp(CHUNK * log_gamma)[None, :, None, None]
    kv = jnp.einsum('bnjhd,jh,bnjhe->nbhde', kc, k_dec, vc)

    def step(s, kv_n):
        return chunk_dec * s + kv_n, s

    _, s_prev = lax.scan(step, s0.astype(jnp.float32), kv)
    cross = jnp.einsum('bnihd,ih,nbhde->bnihe', qc, q_dec, s_prev)
    return (intra + cross).reshape(B, L, H, dv)


def _retention_branch(q, k, v, g, s_f, s_b, lg):
    B, L, _ = q.shape
    qh = q.reshape(B, L, RET_HEADS, RET_QK_DIM).astype(jnp.float32)
    kh, vh = _ret_kv(k, v)
    fwd = _retention_scan(qh, kh, vh, lg[0], s_f)
    bwd = jnp.flip(_retention_scan(jnp.flip(qh, 1), jnp.flip(kh, 1), jnp.flip(vh, 1), lg[1], s_b), 1)
    y = fwd + bwd
    mu = jnp.mean(y, axis=-1, keepdims=True)
    var = jnp.mean(jnp.square(y - mu), axis=-1, keepdims=True)
    y = ((y - mu) * lax.rsqrt(var + LN_EPS)).reshape(B, L, RET_V_WIDTH)
    return (y * jax.nn.silu(g.astype(jnp.float32))).astype(g.dtype)


def _pool_mixer(p, pool_w, pool_scale):
    B, L, _ = p.shape
    pf = p.astype(jnp.float32)
    csum = jnp.concatenate([jnp.zeros((B, 1, POOL_WIDTH), jnp.float32), jnp.cumsum(pf, axis=1)], axis=1)
    t = jnp.arange(L)
    outs = []
    for gi, w in enumerate(POOL_WINDOWS):
        lo = jnp.clip(t - w // 2, 0, L)
        hi = jnp.clip(t + w // 2, 0, L)
        cs = csum[:, :, gi * POOL_GROUP_DIM:(gi + 1) * POOL_GROUP_DIM]
        mean = (cs[:, hi] - cs[:, lo]) / (hi - lo).astype(jnp.float32)[None, :, None]
        diff = (mean - pf[:, :, gi * POOL_GROUP_DIM:(gi + 1) * POOL_GROUP_DIM]).astype(p.dtype)
        outs.append(diff @ pool_w[gi])
    return jnp.concatenate(outs, axis=-1) * pool_scale


def _token_mixer(proj, s_f, s_b, lg, pool_w, pool_scale, w_branch_ret, w_branch_pool, w_out):
    k, v, q, g, p_in, gate_a, gate_b = _split_in(proj)
    ret = _retention_branch(q, k, v, g, s_f, s_b, lg) @ w_branch_ret
    pool = _pool_mixer(p_in, pool_w, pool_scale) @ w_branch_pool
    merged = jax.nn.sigmoid(gate_a) * ret + jax.nn.sigmoid(gate_b) * pool
    return merged @ w_out


def _conv_ffn(u, rows, cols, w_up, conv_w, conv_b, w_down):
    B, L, _ = u.shape
    h = (u @ w_up).reshape(B, rows, cols, 2 * D_FF)
    h = lax.conv_general_dilated(h, conv_w[:, :, None, :], (1, 1), 'SAME',
                                 dimension_numbers=('NHWC', 'HWIO', 'NHWC'),
                                 feature_group_count=2 * D_FF) + conv_b
    a, b = jnp.split(h.reshape(B, L, 2 * D_FF), 2, axis=-1)
    return (jax.nn.gelu(a) * b) @ w_down


def _layer(x, ctx, c, c_ctx, w_ada, b_ada, w_in, ret_decay_logit, pool_w, pool_scale,
           w_branch_ret, w_branch_pool, w_out, ln1_g, ln1_b, w_up, conv_w, conv_b, w_down,
           ln2_g, ln2_b, update_ctx):
    B, L, _ = x.shape
    rows = L // GRID_W
    Lc = ctx.shape[1]
    mod = jax.nn.silu(c) @ w_ada + b_ada
    mod_c = jax.nn.silu(c_ctx) @ w_ada + b_ada
    sh1, sc1, g1, sh2, sc2, g2 = jnp.split(mod[:, None, :], N_MOD, axis=-1)
    sh1c, sc1c, g1c, sh2c, sc2c, g2c = jnp.split(mod_c, N_MOD, axis=-1)
    lg = jax.nn.log_sigmoid(ret_decay_logit.astype(jnp.float32))

    uc = _modulate(ctx, sh1c, sc1c)
    if update_ctx:
        projc = uc @ w_in
        kvc = projc[..., :KV_COLS]
    else:
        kvc = uc @ w_in[:, :KV_COLS]
    kc, vc = _ret_kv(kvc[..., :RET_QK_WIDTH], kvc[..., RET_QK_WIDTH:])
    s_f, s_b = _context_states(kc, vc, lg)

    u = _modulate(x, sh1, sc1)
    mix = _token_mixer(u @ w_in, s_f, s_b, lg, pool_w, pool_scale, w_branch_ret, w_branch_pool, w_out)
    x = _post_norm(DEEPNORM_ALPHA * x + g1 * mix, ln1_g, ln1_b)
    u2 = _modulate(x, sh2, sc2)
    x = _post_norm(DEEPNORM_ALPHA * x + g2 * _conv_ffn(u2, rows, GRID_W, w_up, conv_w, conv_b, w_down),
                   ln2_g, ln2_b)

    if update_ctx:
        zero_state = jnp.zeros((B, RET_HEADS, RET_QK_DIM, RET_V_DIM), jnp.float32)
        mix_c = _token_mixer(projc, zero_state, zero_state, lg, pool_w, pool_scale,
                             w_branch_ret, w_branch_pool, w_out)
        ctx = _post_norm(DEEPNORM_ALPHA * ctx + g1c * mix_c, ln1_g, ln1_b)
        u2c = _modulate(ctx, sh2c, sc2c)
        ctx = _post_norm(DEEPNORM_ALPHA * ctx + g2c * _conv_ffn(u2c, 1, Lc, w_up, conv_w, conv_b, w_down),
                         ln2_g, ln2_b)
    return x, ctx


def setup_inputs(seed: int = 0) -> dict:
    key = jax.random.key(seed)
    ks = jax.random.split(key, 24)
    D = D_MODEL
    nrm = lambda k, shape, s: jax.random.normal(k, shape, jnp.float32) * s
    base_logit = jnp.log(2.0 ** (5.0 + jnp.arange(RET_HEADS, dtype=jnp.float32)) - 1.0)
    return {
        "x": nrm(ks[0], (BATCH, SEQ, D), 1.0),
        "c": nrm(ks[1], (BATCH, D), 1.0),
        "ctx": nrm(ks[2], (BATCH, CTX_LEN, D), 1.0),
        "c_ctx": nrm(ks[3], (D,), 1.0),
        "w_ada": nrm(ks[4], (DEPTH, D, N_MOD * D), D ** -0.5),
        "b_ada": nrm(ks[5], (DEPTH, N_MOD * D), 0.02),
        "w_in": nrm(ks[6], (DEPTH, D, IN_WIDTH), D ** -0.5),
        "ret_decay_logit": base_logit[None, None, :] + nrm(ks[7], (DEPTH, 2, RET_HEADS), 0.05),
        "pool_w": nrm(ks[8], (DEPTH, POOL_GROUPS, POOL_GROUP_DIM, POOL_GROUP_DIM), POOL_GROUP_DIM ** -0.5),
        "pool_scale": 1.0 + nrm(ks[9], (DEPTH, POOL_WIDTH), 0.1),
        "w_branch_ret": nrm(ks[10], (DEPTH, RET_V_WIDTH, D), RET_V_WIDTH ** -0.5),
        "w_branch_pool": nrm(ks[11], (DEPTH, POOL_WIDTH, D), POOL_WIDTH ** -0.5),
        "w_out": nrm(ks[12], (DEPTH, D, D), D ** -0.5 * DEEPNORM_BETA),
        "ln1_g": 1.0 + nrm(ks[13], (DEPTH, D), 0.02),
        "ln1_b": nrm(ks[14], (DEPTH, D), 0.02),
        "w_up": nrm(ks[15], (DEPTH, D, 2 * D_FF), D ** -0.5),
        "conv_w": nrm(ks[16], (DEPTH, CONV_K, CONV_K, 2 * D_FF), 1.0 / CONV_K),
        "conv_b": nrm(ks[17], (DEPTH, 2 * D_FF), 0.02),
        "w_down": nrm(ks[18], (DEPTH, D_FF, D), D_FF ** -0.5 * DEEPNORM_BETA),
        "ln2_g": 1.0 + nrm(ks[19], (DEPTH, D), 0.02),
        "ln2_b": nrm(ks[20], (DEPTH, D), 0.02),
    }


def reference(x, c, ctx, c_ctx, w_ada, b_ada, w_in, ret_decay_logit, pool_w, pool_scale,
              w_branch_ret, w_branch_pool, w_out, ln1_g, ln1_b, w_up, conv_w, conv_b, w_down,
              ln2_g, ln2_b):
    for l in range(DEPTH):
        x, ctx = _layer(x, ctx, c, c_ctx, w_ada[l], b_ada[l], w_in[l], ret_decay_logit[l], pool_w[l],
                        pool_scale[l], w_branch_ret[l], w_branch_pool[l], w_out[l], ln1_g[l], ln1_b[l],
                        w_up[l], conv_w[l], conv_b[l], w_down[l], ln2_g[l], ln2_b[l],
                        update_ctx=(l < DEPTH - 1))
    return x
```

```python
import functools

import jax
import jax.numpy as jnp
import numpy as np
from jax import lax
from jax.experimental import pallas as pl
from jax.experimental.pallas import tpu as pltpu

F32 = jnp.float32
BF16 = jnp.bfloat16

D_MODEL = 1024
GRID_W = 64
HEADS = 8
DK = 64
DV = 128
QK_W = HEADS * DK
V_W = HEADS * DV
KV_W = QK_W + V_W
CHUNK = 128
PAIRS = HEADS // 2
POOL_WINDOWS = (2, 4, 8, 16)
POOL_GD = 128
POOL_W = 512
D_FF = 2816
FF_CW = 256
FF_NC = D_FF // FF_CW
N_MOD = 6
LN_EPS = 1e-6
ALPHA = 2.0 ** 0.25
POOL_HALO = 8
MOD_ROWS = 8

VMEM_LIMIT = 56 * 1024 * 1024

TB_KV = 512
TB_MIX = 512
TB_FFN = 512


def _dot(a, b):
    return jnp.dot(a, b, preferred_element_type=F32)


def _dot_nt(a, b):
    return lax.dot_general(a, b, (((1,), (1,)), ((), ())), preferred_element_type=F32)


def _dot_tn(a, b):
    return lax.dot_general(a, b, (((0,), (0,)), ((), ())), preferred_element_type=F32)


def _ln(x):
    mu = jnp.mean(x, axis=-1, keepdims=True)
    xc = x - mu
    var = jnp.mean(xc * xc, axis=-1, keepdims=True)
    return xc * lax.rsqrt(var + LN_EPS)


def _const_spec(shape):
    nd = len(shape)
    return pl.BlockSpec(shape, lambda *_: (0,) * nd, pipeline_mode=pl.Buffered(1))


def _pair_diag(r):
    row = lax.broadcasted_iota(jnp.int32, (CHUNK, DV), 0)
    return jnp.where(row < DK, r[:, :DV], r[:, DV:])


def _chunk_kv(kd, v):
    outs = []
    for p in range(PAIRS):
        r = _dot_tn(kd[:, p * 128:(p + 1) * 128], v[:, p * 256:(p + 1) * 256])
        outs.append(_pair_diag(r))
    return jnp.concatenate(outs, axis=0)


def _adaln_kernel(c_ref, w_ref, b_ref, o_ref):
    c = c_ref[...]
    s = c * jax.nn.sigmoid(c)
    o_ref[...] = _dot(s, w_ref[...]) + b_ref[...]


def _adaln(cc, w_ada, b_ada):
    n = w_ada.shape[1]
    bn = 1536
    return pl.pallas_call(
        _adaln_kernel,
        grid=(n // bn,),
        in_specs=[pl.BlockSpec((MOD_ROWS, D_MODEL), lambda i: (0, 0)),
                  pl.BlockSpec((D_MODEL, bn), lambda i: (0, i)),
                  pl.BlockSpec((1, bn), lambda i: (0, i))],
        out_specs=pl.BlockSpec((MOD_ROWS, bn), lambda i: (0, i)),
        out_shape=jax.ShapeDtypeStruct((MOD_ROWS, n), F32),
        compiler_params=pltpu.CompilerParams(vmem_limit_bytes=VMEM_LIMIT),
        name="adaln",
    )(cc, w_ada, b_ada)


def _ctx_kernel(ctx_ref, sh_ref, sc_ref, wkv_ref, kdf_ref, kdb_ref, cdf_ref, cdb_ref, sf_ref, sb_ref):
    x = ctx_ref[0]
    u = (_ln(x) * (1.0 + sc_ref[...]) + sh_ref[...]).astype(BF16)
    kv = _dot(u, wkv_ref[...])
    k = kv[:, :QK_W] * (DK ** -0.5)
    v = kv[:, QK_W:].astype(BF16)
    n = x.shape[0] // CHUNK
    sf = jnp.zeros((QK_W, DV), F32)
    for c in range(n):
        kc = k[c * CHUNK:(c + 1) * CHUNK]
        vc = v[c * CHUNK:(c + 1) * CHUNK]
        sf = cdf_ref[...] * sf + _chunk_kv((kc * kdf_ref[...]).astype(BF16), vc)
    sb = jnp.zeros((QK_W, DV), F32)
    for c in reversed(range(n)):
        kc = k[c * CHUNK:(c + 1) * CHUNK]
        vc = v[c * CHUNK:(c + 1) * CHUNK]
        sb = cdb_ref[...] * sb + _chunk_kv((kc * kdb_ref[...]).astype(BF16), vc)
    sf_ref[0] = sf
    sb_ref[0] = sb


def _ctx_states(ctx, sh, sc, w_kv, kdf, kdb, cdf, cdb):
    B, Lc, _ = ctx.shape
    st = jax.ShapeDtypeStruct((B, QK_W, DV), F32)
    return pl.pallas_call(
        _ctx_kernel,
        grid=(B,),
        in_specs=[pl.BlockSpec((1, Lc, D_MODEL), lambda b: (b, 0, 0)),
                  _const_spec((1, D_MODEL)), _const_spec((1, D_MODEL)),
                  _const_spec((D_MODEL, KV_W)),
                  _const_spec((CHUNK, QK_W)), _const_spec((CHUNK, QK_W)),
                  _const_spec((QK_W, DV)), _const_spec((QK_W, DV))],
        out_specs=[pl.BlockSpec((1, QK_W, DV), lambda b: (b, 0, 0)),
                   pl.BlockSpec((1, QK_W, DV), lambda b: (b, 0, 0))],
        out_shape=[st, st],
        compiler_params=pltpu.CompilerParams(vmem_limit_bytes=VMEM_LIMIT),
        name="ctx_states",
    )(ctx, sh, sc, w_kv, kdf, kdb, cdf, cdb)


def _kv_kernel(x_ref, sh_ref, sc_ref, wkv_ref, kdb_ref, cdb_ref, sb0_ref, kv_ref, sb_ref, s_ref):
    @pl.when(pl.program_id(1) == 0)
    def _():
        s_ref[...] = sb0_ref[0]

    x = x_ref[0]
    u = (_ln(x) * (1.0 + sc_ref[0]) + sh_ref[0]).astype(BF16)
    kv = _dot(u, wkv_ref[...])
    k = kv[:, :QK_W] * (DK ** -0.5)
    v = kv[:, QK_W:].astype(BF16)
    kv_ref[0, :, :QK_W] = k.astype(BF16)
    kv_ref[0, :, QK_W:] = v
    n = x.shape[0] // CHUNK
    for c in reversed(range(n)):
        s = s_ref[...]
        sb_ref[0, c] = s.astype(BF16)
        kc = k[c * CHUNK:(c + 1) * CHUNK]
        vc = v[c * CHUNK:(c + 1) * CHUNK]
        s_ref[...] = cdb_ref[...] * s + _chunk_kv((kc * kdb_ref[...]).astype(BF16), vc)


def _kv_states(x, sh, sc, w_kv, kdb, cdb, sb0):
    B, L, _ = x.shape
    tb = TB_KV
    nb = L // tb
    nch = tb // CHUNK
    return pl.pallas_call(
        _kv_kernel,
        grid=(B, nb),
        in_specs=[pl.BlockSpec((1, tb, D_MODEL), lambda b, j: (b, nb - 1 - j, 0)),
                  pl.BlockSpec((1, 1, D_MODEL), lambda b, j: (b, 0, 0)),
                  pl.BlockSpec((1, 1, D_MODEL), lambda b, j: (b, 0, 0)),
                  _const_spec((D_MODEL, KV_W)),
                  _const_spec((CHUNK, QK_W)),
                  _const_spec((QK_W, DV)),
                  pl.BlockSpec((1, QK_W, DV), lambda b, j: (b, 0, 0))],
        out_specs=[pl.BlockSpec((1, tb, KV_W), lambda b, j: (b, nb - 1 - j, 0)),
                   pl.BlockSpec((1, nch, QK_W, DV), lambda b, j: (b, nb - 1 - j, 0, 0))],
        out_shape=[jax.ShapeDtypeStruct((B, L, KV_W), BF16),
                   jax.ShapeDtypeStruct((B, L // CHUNK, QK_W, DV), BF16)],
        scratch_shapes=[pltpu.VMEM((QK_W, DV), F32)],
        compiler_params=pltpu.CompilerParams(
            dimension_semantics=("arbitrary", "arbitrary"), vmem_limit_bytes=VMEM_LIMIT),
        name="kv_states",
    )(x, sh, sc, w_kv, kdb, cdb, sb0)


def _pool_features(pe, j, nb, tb, seq_len, poolw_ref, pscale_ref):
    n = pe.shape[0]
    t = j * tb + lax.broadcasted_iota(jnp.int32, (tb, POOL_GD), 0)
    outs = []
    for gi, w in enumerate(POOL_WINDOWS):
        half = w // 2
        a = pe[:, gi * POOL_GD:(gi + 1) * POOL_GD]
        centre = a[POOL_HALO:POOL_HALO + tb]
        s = a
        step = 1
        while step < w:
            s = s + pltpu.roll(s, n - step, axis=0)
            step *= 2
        s = pltpu.roll(s, half, axis=0)[POOL_HALO:POOL_HALO + tb]
        cnt = (jnp.minimum(t + half, seq_len) - jnp.maximum(t - half, 0)).astype(F32)
        diff = (s / cnt - centre).astype(BF16)
        outs.append(_dot(diff, poolw_ref[gi]))
    return jnp.concatenate(outs, axis=-1) * pscale_ref[...]


def _mixer_kernel(x_ref, xp_ref, xn_ref, kv_ref, sb_ref, sf0_ref, sh_ref, sc_ref, g1_ref,
                  wr_ref, dm_ref, qdf_ref, qdb_ref, kdf_ref, cdf_ref,
                  poolw_ref, pscale_ref, wbr_ref, wbp_ref, wout_ref, lng_ref, lnb_ref,
                  o_ref, s_ref, r_ref, *, nb, seq_len):
    j = pl.program_id(1)
    tb = x_ref.shape[1]

    @pl.when(j == 0)
    def _():
        s_ref[...] = sf0_ref[0]

    scale = 1.0 + sc_ref[0]
    shift = sh_ref[0]
    x = x_ref[0]
    u = (_ln(x) * scale + shift).astype(BF16)

    q = _dot(u, wr_ref[:, 0:QK_W])
    g = _dot(u, wr_ref[:, QK_W:QK_W + V_W])

    lane = lax.broadcasted_iota(jnp.int32, (CHUNK, 128), 1)
    zv = jnp.zeros((CHUNK, DV), BF16)
    zs = jnp.zeros((DK, DV), BF16)

    def block_diag_state(s):
        left = jnp.concatenate([s[:DK], zs], axis=0)
        right = jnp.concatenate([zs, s[DK:]], axis=0)
        return jnp.concatenate([left, right], axis=1)

    for c in range(tb // CHUNK):
        rows = slice(c * CHUNK, (c + 1) * CHUNK)
        qc = q[rows]
        qb16 = qc.astype(BF16)
        qf = (qc * qdf_ref[...]).astype(BF16)
        qb = (qc * qdb_ref[...]).astype(BF16)
        kc = kv_ref[0, rows, 0:QK_W].astype(F32)
        vc = kv_ref[0, rows, QK_W:KV_W]
        kd = (kc * kdf_ref[...]).astype(BF16)
        ys = []
        for p in range(PAIRS):
            ql = slice(p * 128, (p + 1) * 128)
            kp = kc[:, ql]
            k_lo = jnp.where(lane < DK, kp, 0.0).astype(BF16)
            k_hi = jnp.where(lane >= DK, kp, 0.0).astype(BF16)
            krhs = jnp.concatenate([k_lo, k_hi], axis=0)
            sc = (_dot_nt(qb16[:, ql], krhs) * dm_ref[p]).astype(BF16)
            vp = vc[:, p * 256:(p + 1) * 256]
            vrhs = jnp.concatenate([jnp.concatenate([vp[:, :DV], zv], axis=1),
                                    jnp.concatenate([zv, vp[:, DV:]], axis=1)], axis=0)
            y = _dot(sc, vrhs)
            s_f = s_ref[ql, :]
            srhs = jnp.concatenate([block_diag_state(s_f.astype(BF16)),
                                    block_diag_state(sb_ref[0, c, ql, :])], axis=0)
            qlhs = jnp.concatenate([qf[:, ql], qb[:, ql]], axis=1)
            y = y + _dot(qlhs, srhs)
            ys.append(y)
            s_ref[ql, :] = cdf_ref[ql, :] * s_f + _pair_diag(_dot_tn(kd[:, ql], vp))
        gc = g[rows]
        sg = gc * jax.nn.sigmoid(gc)
        for p in range(PAIRS):
            for hh in range(2):
                h = 2 * p + hh
                yh = ys[p][:, hh * DV:(hh + 1) * DV]
                mu = jnp.mean(yh, axis=-1, keepdims=True)
                yc = yh - mu
                var = jnp.mean(yc * yc, axis=-1, keepdims=True)
                yn = yc * lax.rsqrt(var + LN_EPS)
                r_ref[rows, h * DV:(h + 1) * DV] = (yn * sg[:, h * DV:(h + 1) * DV]).astype(BF16)

    ret = _dot(r_ref[...], wbr_ref[...])

    uh = (_ln(jnp.concatenate([xp_ref[0], xn_ref[0]], axis=0)) * scale + shift).astype(BF16)
    ph = _dot(uh, wr_ref[:, QK_W + V_W:QK_W + V_W + POOL_W])
    p_prev = ph[:POOL_HALO] * (j > 0).astype(F32)
    p_next = ph[POOL_HALO:] * (j < nb - 1).astype(F32)
    pm = _dot(u, wr_ref[:, QK_W + V_W:QK_W + V_W + POOL_W])
    pe = jnp.concatenate([p_prev, pm, p_next], axis=0)
    feat = _pool_features(pe, j, nb, tb, seq_len, poolw_ref, pscale_ref).astype(BF16)
    pool = _dot(feat, wbp_ref[...])

    ga = _dot(u, wr_ref[:, 2048:3072])
    merged = jax.nn.sigmoid(ga) * ret
    gb = _dot(u, wr_ref[:, 3072:4096])
    merged = (merged + jax.nn.sigmoid(gb) * pool).astype(BF16)
    mix = _dot(merged, wout_ref[...])
    z = ALPHA * x + g1_ref[0] * mix
    o_ref[0] = _ln(z) * lng_ref[...] + lnb_ref[...]


def _mixer(x, kv, sb, sf0, sh, sc, g1, w_rest, dm, qdf, qdb, kdf, cdf,
           pool_w, pool_scale, w_br, w_bp, w_out, ln_g, ln_b):
    B, L, _ = x.shape
    tb = TB_MIX
    nb = L // tb
    nch = tb // CHUNK
    hb = tb // POOL_HALO
    nh = L // POOL_HALO
    mod_spec = pl.BlockSpec((1, 1, D_MODEL), lambda b, j: (b, 0, 0))
    return pl.pallas_call(
        functools.partial(_mixer_kernel, nb=nb, seq_len=L),
        grid=(B, nb),
        in_specs=[pl.BlockSpec((1, tb, D_MODEL), lambda b, j: (b, j, 0)),
                  pl.BlockSpec((1, POOL_HALO, D_MODEL), lambda b, j: (b, jnp.maximum(j * hb - 1, 0), 0)),
                  pl.BlockSpec((1, POOL_HALO, D_MODEL),
                               lambda b, j: (b, jnp.minimum((j + 1) * hb, nh - 1), 0)),
                  pl.BlockSpec((1, tb, KV_W), lambda b, j: (b, j, 0)),
                  pl.BlockSpec((1, nch, QK_W, DV), lambda b, j: (b, j, 0, 0)),
                  pl.BlockSpec((1, QK_W, DV), lambda b, j: (b, 0, 0)),
                  mod_spec, mod_spec, mod_spec,
                  _const_spec((D_MODEL, 4096)),
                  _const_spec((PAIRS, CHUNK, 256)),
                  _const_spec((CHUNK, QK_W)), _const_spec((CHUNK, QK_W)), _const_spec((CHUNK, QK_W)),
                  _const_spec((QK_W, DV)),
                  _const_spec((len(POOL_WINDOWS), POOL_GD, POOL_GD)),
                  _const_spec((1, POOL_W)),
                  _const_spec((V_W, D_MODEL)),
                  _const_spec((POOL_W, D_MODEL)),
                  _const_spec((D_MODEL, D_MODEL)),
                  _const_spec((1, D_MODEL)), _const_spec((1, D_MODEL))],
        out_specs=pl.BlockSpec((1, tb, D_MODEL), lambda b, j: (b, j, 0)),
        out_shape=jax.ShapeDtypeStruct((B, L, D_MODEL), F32),
        scratch_shapes=[pltpu.VMEM((QK_W, DV), F32), pltpu.VMEM((tb, V_W), BF16)],
        compiler_params=pltpu.CompilerParams(
            dimension_semantics=("arbitrary", "arbitrary"), vmem_limit_bytes=VMEM_LIMIT),
        name="mixer",
    )(x, x, x, kv, sb, sf0, sh, sc, g1, w_rest, dm, qdf, qdb, kdf, cdf,
      pool_w, pool_scale, w_br, w_bp, w_out, ln_g, ln_b)


def _ffn_kernel(x_ref, xp_ref, xn_ref, sh_ref, sc_ref, g2_ref, wup_ref, cw_ref, wdn_ref,
                lng_ref, lnb_ref, o_ref, u_ref, acc_ref, *, nb):
    j = pl.program_id(1)
    tb = x_ref.shape[1]
    n = tb + 2 * GRID_W
    scale = 1.0 + sc_ref[0]
    shift = sh_ref[0]
    x = x_ref[0]

    def mod(v):
        return _ln(v) * scale + shift

    u_ref[0:GRID_W] = (mod(xp_ref[0]) * (j > 0).astype(F32)).astype(BF16)
    u_ref[GRID_W:GRID_W + tb] = mod(x).astype(BF16)
    u_ref[GRID_W + tb:n] = (mod(xn_ref[0]) * (j < nb - 1).astype(F32)).astype(BF16)

    col = lax.broadcasted_iota(jnp.int32, (n, FF_CW), 0) % GRID_W
    has_left = col > 0
    has_right = col < GRID_W - 1

    def conv(h, cw):
        hl = jnp.where(has_left, pltpu.roll(h, 1, axis=0), 0.0)
        hr = jnp.where(has_right, pltpu.roll(h, n - 1, axis=0), 0.0)
        out = None
        for dr in range(3):
            sl = slice(dr * GRID_W, dr * GRID_W + tb)
            term = (cw[3 * dr:3 * dr + 1] * hl[sl] + cw[3 * dr + 1:3 * dr + 2] * h[sl]
                    + cw[3 * dr + 2:3 * dr + 3] * hr[sl])
            out = term if out is None else out + term
        return out + cw[9:10]

    acc_ref[...] = jnp.zeros_like(acc_ref)

    def body(c, carry):
        u = u_ref[...]
        a = conv(_dot(u, wup_ref[c]), cw_ref[c])
        b = conv(_dot(u, wup_ref[FF_NC + c]), cw_ref[FF_NC + c])
        gated = (jax.nn.gelu(a) * b).astype(BF16)
        acc_ref[...] += _dot(gated, wdn_ref[c])
        return carry

    lax.fori_loop(0, FF_NC, body, 0)
    z = ALPHA * x + g2_ref[0] * acc_ref[...]
    o_ref[0] = _ln(z) * lng_ref[...] + lnb_ref[...]


def _ffn(x, sh, sc, g2, w_up, conv_wb, w_down, ln_g, ln_b):
    B, L, _ = x.shape
    tb = TB_FFN
    nb = L // tb
    hb = tb // GRID_W
    nh = L // GRID_W
    mod_spec = pl.BlockSpec((1, 1, D_MODEL), lambda b, j: (b, 0, 0))
    return pl.pallas_call(
        functools.partial(_ffn_kernel, nb=nb),
        grid=(B, nb),
        in_specs=[pl.BlockSpec((1, tb, D_MODEL), lambda b, j: (b, j, 0)),
                  pl.BlockSpec((1, GRID_W, D_MODEL), lambda b, j: (b, jnp.maximum(j * hb - 1, 0), 0)),
                  pl.BlockSpec((1, GRID_W, D_MODEL),
                               lambda b, j: (b, jnp.minimum((j + 1) * hb, nh - 1), 0)),
                  mod_spec, mod_spec, mod_spec,
                  _const_spec((2 * FF_NC, D_MODEL, FF_CW)),
                  _const_spec((2 * FF_NC, 16, FF_CW)),
                  _const_spec((FF_NC, FF_CW, D_MODEL)),
                  _const_spec((1, D_MODEL)), _const_spec((1, D_MODEL))],
        out_specs=pl.BlockSpec((1, tb, D_MODEL), lambda b, j: (b, j, 0)),
        out_shape=jax.ShapeDtypeStruct((B, L, D_MODEL), F32),
        scratch_shapes=[pltpu.VMEM((tb + 2 * GRID_W, D_MODEL), BF16),
                        pltpu.VMEM((tb, D_MODEL), F32)],
        compiler_params=pltpu.CompilerParams(
            dimension_semantics=("arbitrary", "arbitrary"), vmem_limit_bytes=VMEM_LIMIT),
        name="conv_ffn",
    )(x, x, x, sh, sc, g2, w_up, conv_wb, w_down, ln_g, ln_b)


def _decay_tables(ret_decay_logit):
    lg = jax.nn.log_sigmoid(ret_decay_logit.astype(F32))
    pos = jnp.arange(CHUNK, dtype=F32)
    diff = pos[:, None] - pos[None, :]
    d_f = jnp.where(diff[None] >= 0, jnp.exp(jnp.maximum(diff, 0.0)[None] * lg[0][:, None, None]), 0.0)
    d_b = jnp.where(diff[None] <= 0, jnp.exp(jnp.maximum(-diff, 0.0)[None] * lg[1][:, None, None]), 0.0)
    dm = (d_f + d_b).reshape(PAIRS, 2, CHUNK, CHUNK).transpose(0, 2, 1, 3).reshape(PAIRS, CHUNK, 2 * CHUNK)

    def lanes(t):
        return jnp.repeat(t, DK, axis=1)

    qdf = lanes(jnp.exp((pos + 1.0)[:, None] * lg[0][None, :]))
    qdb = lanes(jnp.exp((CHUNK - pos)[:, None] * lg[1][None, :]))
    kdf = lanes(jnp.exp((CHUNK - 1.0 - pos)[:, None] * lg[0][None, :]))
    kdb = lanes(jnp.exp(pos[:, None] * lg[1][None, :]))

    def rows(t):
        return jnp.broadcast_to(jnp.repeat(t, DK)[:, None], (QK_W, DV))

    cdf = rows(jnp.exp(CHUNK * lg[0]))
    cdb = rows(jnp.exp(CHUNK * lg[1]))
    return dm, qdf, qdb, kdf, kdb, cdf, cdb


def kernel(x, c, ctx, c_ctx, w_ada, b_ada, w_in, ret_decay_logit, pool_w, pool_scale, w_branch_ret,
           w_branch_pool, w_out, ln1_g, ln1_b, w_up, conv_w, conv_b, w_down, ln2_g, ln2_b):
    B = x.shape[0]
    D = D_MODEL
    assert w_ada.shape[0] == 1, "single-layer stack"

    cc = jnp.zeros((MOD_ROWS, D), F32).at[:B].set(c).at[B].set(c_ctx)
    mod = _adaln(cc, w_ada[0], b_ada[0][None, :])
    lat = mod[:B].reshape(B, N_MOD, 1, D)
    sh1, sc1, g1, sh2, sc2, g2 = (lat[:, i] for i in range(N_MOD))
    sh1c = mod[B:B + 1, 0:D]
    sc1c = mod[B:B + 1, D:2 * D]

    dm, qdf, qdb, kdf, kdb, cdf, cdb = _decay_tables(ret_decay_logit[0])

    w_in_b = w_in[0].astype(BF16)
    w_kv = w_in_b[:, :KV_W]
    w_rest = w_in_b[:, KV_W:]

    s_f, s_b = _ctx_states(ctx, sh1c, sc1c, w_kv, kdf, kdb, cdf, cdb)
    kv, sb = _kv_states(x, sh1, sc1, w_kv, kdb, cdb, s_b)
    x1 = _mixer(x, kv, sb, s_f, sh1, sc1, g1, w_rest, dm, qdf, qdb, kdf, cdf,
                pool_w[0].astype(BF16), pool_scale[0][None, :],
                w_branch_ret[0].astype(BF16), w_branch_pool[0].astype(BF16), w_out[0].astype(BF16),
                ln1_g[0][None, :], ln1_b[0][None, :])

    w_up_c = w_up[0].astype(BF16).reshape(D, 2 * FF_NC, FF_CW).transpose(1, 0, 2)
    conv_wb = jnp.concatenate([conv_w[0].reshape(9, 2 * D_FF), conv_b[0][None, :],
                               jnp.zeros((6, 2 * D_FF), F32)], axis=0)
    conv_wb = conv_wb.reshape(16, 2 * FF_NC, FF_CW).transpose(1, 0, 2)
    w_down_c = w_down[0].astype(BF16).reshape(FF_NC, FF_CW, D)
    return _ffn(x1, sh2, sc2, g2, w_up_c, conv_wb, w_down_c, ln2_g[0][None, :], ln2_b[0][None, :])
```

```python
import functools

import jax
import jax.numpy as jnp
import numpy as np
from jax import lax
from jax.experimental import pallas as pl
from jax.experimental.pallas import tpu as pltpu

F32 = jnp.float32
BF16 = jnp.bfloat16

D_MODEL = 1024
GRID_W = 64
HEADS = 8
DK = 64
DV = 128
QK_W = HEADS * DK
V_W = HEADS * DV
KV_W = QK_W + V_W
CHUNK = 128
PAIRS = HEADS // 2
POOL_WINDOWS = (2, 4, 8, 16)
POOL_GD = 128
POOL_W = 512
D_FF = 2816
FF_CW = 256
FF_NC = D_FF // FF_CW
FF_PAD = 8
N_MOD = 6
LN_EPS = 1e-6
ALPHA = 2.0 ** 0.25
POOL_HALO = 8
MOD_ROWS = 8

VMEM_LIMIT = 56 * 1024 * 1024

TB_KV = 512
TB_MIX = 512
TB_FFN = 512


def _dot(a, b):
    return jnp.dot(a, b, preferred_element_type=F32)


def _dot_nt(a, b):
    return lax.dot_general(a, b, (((1,), (1,)), ((), ())), preferred_element_type=F32)


def _dot_tn(a, b):
    return lax.dot_general(a, b, (((0,), (0,)), ((), ())), preferred_element_type=F32)


def _ln(x):
    mu = jnp.mean(x, axis=-1, keepdims=True)
    xc = x - mu
    var = jnp.mean(xc * xc, axis=-1, keepdims=True)
    return xc * lax.rsqrt(var + LN_EPS)


def _const_spec(shape):
    nd = len(shape)
    return pl.BlockSpec(shape, lambda *_: (0,) * nd, pipeline_mode=pl.Buffered(1))


def _pair_diag(r):
    row = lax.broadcasted_iota(jnp.int32, (CHUNK, DV), 0)
    return jnp.where(row < DK, r[:, :DV], r[:, DV:])


def _chunk_kv(kd, v):
    outs = []
    for p in range(PAIRS):
        r = _dot_tn(kd[:, p * 128:(p + 1) * 128], v[:, p * 256:(p + 1) * 256])
        outs.append(_pair_diag(r))
    return jnp.concatenate(outs, axis=0)


def _adaln_kernel(c_ref, w_ref, b_ref, o_ref):
    c = c_ref[...]
    s = c * jax.nn.sigmoid(c)
    o_ref[...] = _dot(s, w_ref[...]) + b_ref[...]


def _adaln(cc, w_ada, b_ada):
    n = w_ada.shape[1]
    bn = 1536
    return pl.pallas_call(
        _adaln_kernel,
        grid=(n // bn,),
        in_specs=[pl.BlockSpec((MOD_ROWS, D_MODEL), lambda i: (0, 0)),
                  pl.BlockSpec((D_MODEL, bn), lambda i: (0, i)),
                  pl.BlockSpec((1, bn), lambda i: (0, i))],
        out_specs=pl.BlockSpec((MOD_ROWS, bn), lambda i: (0, i)),
        out_shape=jax.ShapeDtypeStruct((MOD_ROWS, n), F32),
        compiler_params=pltpu.CompilerParams(vmem_limit_bytes=VMEM_LIMIT),
        name="adaln",
    )(cc, w_ada, b_ada)


def _ctx_kernel(ctx_ref, sh_ref, sc_ref, wkv_ref, kdf_ref, kdb_ref, cdf_ref, cdb_ref, sf_ref, sb_ref):
    x = ctx_ref[0]
    u = (_ln(x) * (1.0 + sc_ref[...]) + sh_ref[...]).astype(BF16)
    kv = _dot(u, wkv_ref[...])
    k = kv[:, :QK_W] * (DK ** -0.5)
    v = kv[:, QK_W:].astype(BF16)
    n = x.shape[0] // CHUNK
    sf = jnp.zeros((QK_W, DV), F32)
    for c in range(n):
        kc = k[c * CHUNK:(c + 1) * CHUNK]
        vc = v[c * CHUNK:(c + 1) * CHUNK]
        sf = cdf_ref[...] * sf + _chunk_kv((kc * kdf_ref[...]).astype(BF16), vc)
    sb = jnp.zeros((QK_W, DV), F32)
    for c in reversed(range(n)):
        kc = k[c * CHUNK:(c + 1) * CHUNK]
        vc = v[c * CHUNK:(c + 1) * CHUNK]
        sb = cdb_ref[...] * sb + _chunk_kv((kc * kdb_ref[...]).astype(BF16), vc)
    sf_ref[0] = sf
    sb_ref[0] = sb


def _ctx_states(ctx, sh, sc, w_kv, kdf, kdb, cdf, cdb):
    B, Lc, _ = ctx.shape
    st = jax.ShapeDtypeStruct((B, QK_W, DV), F32)
    return pl.pallas_call(
        _ctx_kernel,
        grid=(B,),
        in_specs=[pl.BlockSpec((1, Lc, D_MODEL), lambda b: (b, 0, 0)),
                  _const_spec((1, D_MODEL)), _const_spec((1, D_MODEL)),
                  _const_spec((D_MODEL, KV_W)),
                  _const_spec((CHUNK, QK_W)), _const_spec((CHUNK, QK_W)),
                  _const_spec((QK_W, DV)), _const_spec((QK_W, DV))],
        out_specs=[pl.BlockSpec((1, QK_W, DV), lambda b: (b, 0, 0)),
                   pl.BlockSpec((1, QK_W, DV), lambda b: (b, 0, 0))],
        out_shape=[st, st],
        compiler_params=pltpu.CompilerParams(vmem_limit_bytes=VMEM_LIMIT),
        name="ctx_states",
    )(ctx, sh, sc, w_kv, kdf, kdb, cdf, cdb)


def _kv_kernel(x_ref, sh_ref, sc_ref, wkv_ref, kdb_ref, cdb_ref, sb0_ref, kv_ref, sb_ref, s_ref):
    @pl.when(pl.program_id(1) == 0)
    def _():
        s_ref[...] = sb0_ref[0]

    x = x_ref[0]
    u = (_ln(x) * (1.0 + sc_ref[0]) + sh_ref[0]).astype(BF16)
    kv = _dot(u, wkv_ref[...])
    k = kv[:, :QK_W] * (DK ** -0.5)
    v = kv[:, QK_W:].astype(BF16)
    kv_ref[0, :, :QK_W] = k.astype(BF16)
    kv_ref[0, :, QK_W:] = v
    n = x.shape[0] // CHUNK
    for c in reversed(range(n)):
        s = s_ref[...]
        sb_ref[0, c] = s.astype(BF16)
        kc = k[c * CHUNK:(c + 1) * CHUNK]
        vc = v[c * CHUNK:(c + 1) * CHUNK]
        s_ref[...] = cdb_ref[...] * s + _chunk_kv((kc * kdb_ref[...]).astype(BF16), vc)


def _kv_states(x, sh, sc, w_kv, kdb, cdb, sb0):
    B, L, _ = x.shape
    tb = TB_KV
    nb = L // tb
    nch = tb // CHUNK
    return pl.pallas_call(
        _kv_kernel,
        grid=(B, nb),
        in_specs=[pl.BlockSpec((1, tb, D_MODEL), lambda b, j: (b, nb - 1 - j, 0)),
                  pl.BlockSpec((1, 1, D_MODEL), lambda b, j: (b, 0, 0)),
                  pl.BlockSpec((1, 1, D_MODEL), lambda b, j: (b, 0, 0)),
                  _const_spec((D_MODEL, KV_W)),
                  _const_spec((CHUNK, QK_W)),
                  _const_spec((QK_W, DV)),
                  pl.BlockSpec((1, QK_W, DV), lambda b, j: (b, 0, 0))],
        out_specs=[pl.BlockSpec((1, tb, KV_W), lambda b, j: (b, nb - 1 - j, 0)),
                   pl.BlockSpec((1, nch, QK_W, DV), lambda b, j: (b, nb - 1 - j, 0, 0))],
        out_shape=[jax.ShapeDtypeStruct((B, L, KV_W), BF16),
                   jax.ShapeDtypeStruct((B, L // CHUNK, QK_W, DV), BF16)],
        scratch_shapes=[pltpu.VMEM((QK_W, DV), F32)],
        compiler_params=pltpu.CompilerParams(
            dimension_semantics=("arbitrary", "arbitrary"), vmem_limit_bytes=VMEM_LIMIT),
        name="kv_states",
    )(x, sh, sc, w_kv, kdb, cdb, sb0)


def _pool_features(pe, j, nb, tb, seq_len, poolw_ref, pscale_ref):
    n = pe.shape[0]
    t = j * tb + lax.broadcasted_iota(jnp.int32, (tb, POOL_GD), 0)
    outs = []
    for gi, w in enumerate(POOL_WINDOWS):
        half = w // 2
        a = pe[:, gi * POOL_GD:(gi + 1) * POOL_GD]
        centre = a[POOL_HALO:POOL_HALO + tb]
        s = a
        step = 1
        while step < w:
            s = s + pltpu.roll(s, n - step, axis=0)
            step *= 2
        s = pltpu.roll(s, half, axis=0)[POOL_HALO:POOL_HALO + tb]
        cnt = (jnp.minimum(t + half, seq_len) - jnp.maximum(t - half, 0)).astype(F32)
        diff = (s / cnt - centre).astype(BF16)
        outs.append(_dot(diff, poolw_ref[gi]))
    return jnp.concatenate(outs, axis=-1) * pscale_ref[...]


def _mixer_kernel(x_ref, xp_ref, xn_ref, kv_ref, sb_ref, sf0_ref, sh_ref, sc_ref, g1_ref,
                  wr_ref, dm_ref, qdf_ref, qdb_ref, kdf_ref, cdf_ref,
                  poolw_ref, pscale_ref, wbr_ref, wbp_ref, wout_ref, lng_ref, lnb_ref,
                  o_ref, s_ref, r_ref, *, nb, seq_len):
    j = pl.program_id(1)
    tb = x_ref.shape[1]

    @pl.when(j == 0)
    def _():
        s_ref[...] = sf0_ref[0]

    scale = 1.0 + sc_ref[0]
    shift = sh_ref[0]
    x = x_ref[0]
    u = (_ln(x) * scale + shift).astype(BF16)

    q = _dot(u, wr_ref[:, 0:QK_W])
    g = _dot(u, wr_ref[:, QK_W:QK_W + V_W])

    lane = lax.broadcasted_iota(jnp.int32, (CHUNK, 128), 1)
    zv = jnp.zeros((CHUNK, DV), BF16)
    zs = jnp.zeros((DK, DV), BF16)

    def block_diag_state(s):
        left = jnp.concatenate([s[:DK], zs], axis=0)
        right = jnp.concatenate([zs, s[DK:]], axis=0)
        return jnp.concatenate([left, right], axis=1)

    for c in range(tb // CHUNK):
        rows = slice(c * CHUNK, (c + 1) * CHUNK)
        qc = q[rows]
        qb16 = qc.astype(BF16)
        qf = (qc * qdf_ref[...]).astype(BF16)
        qb = (qc * qdb_ref[...]).astype(BF16)
        kc = kv_ref[0, rows, 0:QK_W].astype(F32)
        vc = kv_ref[0, rows, QK_W:KV_W]
        kd = (kc * kdf_ref[...]).astype(BF16)
        ys = []
        for p in range(PAIRS):
            ql = slice(p * 128, (p + 1) * 128)
            kp = kc[:, ql]
            k_lo = jnp.where(lane < DK, kp, 0.0).astype(BF16)
            k_hi = jnp.where(lane >= DK, kp, 0.0).astype(BF16)
            krhs = jnp.concatenate([k_lo, k_hi], axis=0)
            sc = (_dot_nt(qb16[:, ql], krhs) * dm_ref[p]).astype(BF16)
            vp = vc[:, p * 256:(p + 1) * 256]
            vrhs = jnp.concatenate([jnp.concatenate([vp[:, :DV], zv], axis=1),
                                    jnp.concatenate([zv, vp[:, DV:]], axis=1)], axis=0)
            y = _dot(sc, vrhs)
            s_f = s_ref[ql, :]
            srhs = jnp.concatenate([block_diag_state(s_f.astype(BF16)),
                                    block_diag_state(sb_ref[0, c, ql, :])], axis=0)
            qlhs = jnp.concatenate([qf[:, ql], qb[:, ql]], axis=1)
            y = y + _dot(qlhs, srhs)
            ys.append(y)
            s_ref[ql, :] = cdf_ref[ql, :] * s_f + _pair_diag(_dot_tn(kd[:, ql], vp))
        gc = g[rows]
        sg = gc * jax.nn.sigmoid(gc)
        for p in range(PAIRS):
            for hh in range(2):
                h = 2 * p + hh
                yh = ys[p][:, hh * DV:(hh + 1) * DV]
                mu = jnp.mean(yh, axis=-1, keepdims=True)
                yc = yh - mu
                var = jnp.mean(yc * yc, axis=-1, keepdims=True)
                yn = yc * lax.rsqrt(var + LN_EPS)
                r_ref[rows, h * DV:(h + 1) * DV] = (yn * sg[:, h * DV:(h + 1) * DV]).astype(BF16)

    ret = _dot(r_ref[...], wbr_ref[...])

    uh = (_ln(jnp.concatenate([xp_ref[0], xn_ref[0]], axis=0)) * scale + shift).astype(BF16)
    ph = _dot(uh, wr_ref[:, QK_W + V_W:QK_W + V_W + POOL_W])
    p_prev = ph[:POOL_HALO] * (j > 0).astype(F32)
    p_next = ph[POOL_HALO:] * (j < nb - 1).astype(F32)
    pm = _dot(u, wr_ref[:, QK_W + V_W:QK_W + V_W + POOL_W])
    pe = jnp.concatenate([p_prev, pm, p_next], axis=0)
    feat = _pool_features(pe, j, nb, tb, seq_len, poolw_ref, pscale_ref).astype(BF16)
    pool = _dot(feat, wbp_ref[...])

    ga = _dot(u, wr_ref[:, 2048:3072])
    merged = jax.nn.sigmoid(ga) * ret
    gb = _dot(u, wr_ref[:, 3072:4096])
    merged = (merged + jax.nn.sigmoid(gb) * pool).astype(BF16)
    mix = _dot(merged, wout_ref[...])
    z = ALPHA * x + g1_ref[0] * mix
    o_ref[0] = _ln(z) * lng_ref[...] + lnb_ref[...]


def _mixer(x, kv, sb, sf0, sh, sc, g1, w_rest, dm, qdf, qdb, kdf, cdf,
           pool_w, pool_scale, w_br, w_bp, w_out, ln_g, ln_b):
    B, L, _ = x.shape
    tb = TB_MIX
    nb = L // tb
    nch = tb // CHUNK
    hb = tb // POOL_HALO
    nh = L // POOL_HALO
    mod_spec = pl.BlockSpec((1, 1, D_MODEL), lambda b, j: (b, 0, 0))
    return pl.pallas_call(
        functools.partial(_mixer_kernel, nb=nb, seq_len=L),
        grid=(B, nb),
        in_specs=[pl.BlockSpec((1, tb, D_MODEL), lambda b, j: (b, j, 0)),
                  pl.BlockSpec((1, POOL_HALO, D_MODEL), lambda b, j: (b, jnp.maximum(j * hb - 1, 0), 0)),
                  pl.BlockSpec((1, POOL_HALO, D_MODEL),
                               lambda b, j: (b, jnp.minimum((j + 1) * hb, nh - 1), 0)),
                  pl.BlockSpec((1, tb, KV_W), lambda b, j: (b, j, 0)),
                  pl.BlockSpec((1, nch, QK_W, DV), lambda b, j: (b, j, 0, 0)),
                  pl.BlockSpec((1, QK_W, DV), lambda b, j: (b, 0, 0)),
                  mod_spec, mod_spec, mod_spec,
                  _const_spec((D_MODEL, 4096)),
                  _const_spec((PAIRS, CHUNK, 256)),
                  _const_spec((CHUNK, QK_W)), _const_spec((CHUNK, QK_W)), _const_spec((CHUNK, QK_W)),
                  _const_spec((QK_W, DV)),
                  _const_spec((len(POOL_WINDOWS), POOL_GD, POOL_GD)),
                  _const_spec((1, POOL_W)),
                  _const_spec((V_W, D_MODEL)),
                  _const_spec((POOL_W, D_MODEL)),
                  _const_spec((D_MODEL, D_MODEL)),
                  _const_spec((1, D_MODEL)), _const_spec((1, D_MODEL))],
        out_specs=pl.BlockSpec((1, tb, D_MODEL), lambda b, j: (b, j, 0)),
        out_shape=jax.ShapeDtypeStruct((B, L, D_MODEL), F32),
        scratch_shapes=[pltpu.VMEM((QK_W, DV), F32), pltpu.VMEM((tb, V_W), BF16)],
        compiler_params=pltpu.CompilerParams(
            dimension_semantics=("arbitrary", "arbitrary"), vmem_limit_bytes=VMEM_LIMIT),
        name="mixer",
    )(x, x, x, kv, sb, sf0, sh, sc, g1, w_rest, dm, qdf, qdb, kdf, cdf,
      pool_w, pool_scale, w_br, w_bp, w_out, ln_g, ln_b)


def _ffn_kernel(x_ref, xp_ref, xn_ref, sh_ref, sc_ref, g2_ref, wup_ref, cw_ref, wdn_ref,
                lng_ref, lnb_ref, o_ref, u_ref, acc_ref, ha_ref, hb_ref, *, nb):
    j = pl.program_id(1)
    tb = x_ref.shape[1]
    n = tb + 2 * GRID_W
    scale = 1.0 + sc_ref[0]
    shift = sh_ref[0]
    x = x_ref[0]

    def mod(v):
        return _ln(v) * scale + shift

    u_ref[0:GRID_W] = (mod(xp_ref[0]) * (j > 0).astype(F32)).astype(BF16)
    u_ref[GRID_W:GRID_W + tb] = mod(x).astype(BF16)
    u_ref[GRID_W + tb:n] = (mod(xn_ref[0]) * (j < nb - 1).astype(F32)).astype(BF16)

    col = lax.broadcasted_iota(jnp.int32, (tb, FF_CW), 0) & (GRID_W - 1)
    has_left = col > 0
    has_right = col < GRID_W - 1

    pad = jnp.zeros((FF_PAD, FF_CW), F32)
    for h_ref in (ha_ref, hb_ref):
        for ab in range(2):
            h_ref[ab, 0:FF_PAD] = pad
            h_ref[ab, FF_PAD + n:FF_PAD + n + FF_PAD] = pad

    def up(c, h_ref):
        u = u_ref[...]
        h_ref[0, FF_PAD:FF_PAD + n] = _dot(u, wup_ref[c])
        h_ref[1, FF_PAD:FF_PAD + n] = _dot(u, wup_ref[FF_NC + c])

    def conv(h_ref, ab, cw):
        cols = []
        for dc in range(3):
            g = None
            for dr in range(3):
                start = FF_PAD + dr * GRID_W + dc - 1
                term = cw[3 * dr + dc:3 * dr + dc + 1] * h_ref[ab, start:start + tb]
                g = term if g is None else g + term
            cols.append(g)
        return (cols[1] + cw[9:10]) + (jnp.where(has_left, cols[0], 0.0) + jnp.where(has_right, cols[2], 0.0))

    def down(c, h_ref):
        a = conv(h_ref, 0, cw_ref[c])
        b = conv(h_ref, 1, cw_ref[FF_NC + c])
        gated = (jax.nn.gelu(a) * b).astype(BF16)
        acc_ref[...] += _dot(gated, wdn_ref[c])

    acc_ref[...] = jnp.zeros_like(acc_ref)
    up(0, ha_ref)

    def body(i, carry):
        c = 2 * i
        up(c + 1, hb_ref)
        down(c, ha_ref)
        up(c + 2, ha_ref)
        down(c + 1, hb_ref)
        return carry

    lax.fori_loop(0, FF_NC // 2, body, 0)
    down(FF_NC - 1, ha_ref)
    z = ALPHA * x + g2_ref[0] * acc_ref[...]
    o_ref[0] = _ln(z) * lng_ref[...] + lnb_ref[...]


def _ffn(x, sh, sc, g2, w_up, conv_wb, w_down, ln_g, ln_b):
    B, L, _ = x.shape
    tb = TB_FFN
    nb = L // tb
    hb = tb // GRID_W
    nh = L // GRID_W
    mod_spec = pl.BlockSpec((1, 1, D_MODEL), lambda b, j: (b, 0, 0))
    return pl.pallas_call(
        functools.partial(_ffn_kernel, nb=nb),
        grid=(B, nb),
        in_specs=[pl.BlockSpec((1, tb, D_MODEL), lambda b, j: (b, j, 0)),
                  pl.BlockSpec((1, GRID_W, D_MODEL), lambda b, j: (b, jnp.maximum(j * hb - 1, 0), 0)),
                  pl.BlockSpec((1, GRID_W, D_MODEL),
                               lambda b, j: (b, jnp.minimum((j + 1) * hb, nh - 1), 0)),
                  mod_spec, mod_spec, mod_spec,
                  _const_spec((2 * FF_NC, D_MODEL, FF_CW)),
                  _const_spec((2 * FF_NC, 16, FF_CW)),
                  _const_spec((FF_NC, FF_CW, D_MODEL)),
                  _const_spec((1, D_MODEL)), _const_spec((1, D_MODEL))],
        out_specs=pl.BlockSpec((1, tb, D_MODEL), lambda b, j: (b, j, 0)),
        out_shape=jax.ShapeDtypeStruct((B, L, D_MODEL), F32),
        scratch_shapes=[pltpu.VMEM((tb + 2 * GRID_W, D_MODEL), BF16),
                        pltpu.VMEM((tb, D_MODEL), F32),
                        pltpu.VMEM((2, tb + 2 * GRID_W + 2 * FF_PAD, FF_CW), F32),
                        pltpu.VMEM((2, tb + 2 * GRID_W + 2 * FF_PAD, FF_CW), F32)],
        compiler_params=pltpu.CompilerParams(
            dimension_semantics=("arbitrary", "arbitrary"), vmem_limit_bytes=VMEM_LIMIT),
        name="conv_ffn",
    )(x, x, x, sh, sc, g2, w_up, conv_wb, w_down, ln_g, ln_b)


def _decay_tables(ret_decay_logit):
    lg = jax.nn.log_sigmoid(ret_decay_logit.astype(F32))
    pos = jnp.arange(CHUNK, dtype=F32)
    diff = pos[:, None] - pos[None, :]
    d_f = jnp.where(diff[None] >= 0, jnp.exp(jnp.maximum(diff, 0.0)[None] * lg[0][:, None, None]), 0.0)
    d_b = jnp.where(diff[None] <= 0, jnp.exp(jnp.maximum(-diff, 0.0)[None] * lg[1][:, None, None]), 0.0)
    dm = (d_f + d_b).reshape(PAIRS, 2, CHUNK, CHUNK).transpose(0, 2, 1, 3).reshape(PAIRS, CHUNK, 2 * CHUNK)

    def lanes(t):
        return jnp.repeat(t, DK, axis=1)

    qdf = lanes(jnp.exp((pos + 1.0)[:, None] * lg[0][None, :]))
    qdb = lanes(jnp.exp((CHUNK - pos)[:, None] * lg[1][None, :]))
    kdf = lanes(jnp.exp((CHUNK - 1.0 - pos)[:, None] * lg[0][None, :]))
    kdb = lanes(jnp.exp(pos[:, None] * lg[1][None, :]))

    def rows(t):
        return jnp.broadcast_to(jnp.repeat(t, DK)[:, None], (QK_W, DV))

    cdf = rows(jnp.exp(CHUNK * lg[0]))
    cdb = rows(jnp.exp(CHUNK * lg[1]))
    return dm, qdf, qdb, kdf, kdb, cdf, cdb


def kernel(x, c, ctx, c_ctx, w_ada, b_ada, w_in, ret_decay_logit, pool_w, pool_scale, w_branch_ret,
           w_branch_pool, w_out, ln1_g, ln1_b, w_up, conv_w, conv_b, w_down, ln2_g, ln2_b):
    B = x.shape[0]
    D = D_MODEL
    assert w_ada.shape[0] == 1, "single-layer stack"

    cc = jnp.zeros((MOD_ROWS, D), F32).at[:B].set(c).at[B].set(c_ctx)
    mod = _adaln(cc, w_ada[0], b_ada[0][None, :])
    lat = mod[:B].reshape(B, N_MOD, 1, D)
    sh1, sc1, g1, sh2, sc2, g2 = (lat[:, i] for i in range(N_MOD))
    sh1c = mod[B:B + 1, 0:D]
    sc1c = mod[B:B + 1, D:2 * D]

    dm, qdf, qdb, kdf, kdb, cdf, cdb = _decay_tables(ret_decay_logit[0])

    w_in_b = w_in[0].astype(BF16)
    w_kv = w_in_b[:, :KV_W]
    w_rest = w_in_b[:, KV_W:]

    s_f, s_b = _ctx_states(ctx, sh1c, sc1c, w_kv, kdf, kdb, cdf, cdb)
    kv, sb = _kv_states(x, sh1, sc1, w_kv, kdb, cdb, s_b)
    x1 = _mixer(x, kv, sb, s_f, sh1, sc1, g1, w_rest, dm, qdf, qdb, kdf, cdf,
                pool_w[0].astype(BF16), pool_scale[0][None, :],
                w_branch_ret[0].astype(BF16), w_branch_pool[0].astype(BF16), w_out[0].astype(BF16),
                ln1_g[0][None, :], ln1_b[0][None, :])

    w_up_c = w_up[0].astype(BF16).reshape(D, 2 * FF_NC, FF_CW).transpose(1, 0, 2)
    conv_wb = jnp.concatenate([conv_w[0].reshape(9, 2 * D_FF), conv_b[0][None, :],
                               jnp.zeros((6, 2 * D_FF), F32)], axis=0)
    conv_wb = conv_wb.reshape(16, 2 * FF_NC, FF_CW).transpose(1, 0, 2)
    w_down_c = w_down[0].astype(BF16).reshape(FF_NC, FF_CW, D)
    return _ffn(x1, sh2, sc2, g2, w_up_c, conv_wb, w_down_c, ln2_g[0][None, :], ln2_b[0][None, :])
```

```python
import functools

import jax
import jax.numpy as jnp
import numpy as np
from jax import lax
from jax.experimental import pallas as pl
from jax.experimental.pallas import tpu as pltpu

F32 = jnp.float32
BF16 = jnp.bfloat16

D_MODEL = 1024
GRID_W = 64
HEADS = 8
DK = 64
DV = 128
QK_W = HEADS * DK
V_W = HEADS * DV
KV_W = QK_W + V_W
CHUNK = 128
PAIRS = HEADS // 2
POOL_WINDOWS = (2, 4, 8, 16)
POOL_GD = 128
POOL_W = 512
COL_Q = KV_W
COL_G = COL_Q + QK_W
COL_P = COL_G + V_W
COL_GA = COL_P + POOL_W
COL_GB = COL_GA + D_MODEL
IN_W = COL_GB + D_MODEL
D_FF = 2816
FF_CW = 256
FF_NC = D_FF // FF_CW
FF_PAD = 8
N_MOD = 6
LN_EPS = 1e-6
ALPHA = 2.0 ** 0.25
POOL_HALO = 8
MOD_ROWS = 8

VMEM_LIMIT = 56 * 1024 * 1024

TB_KV = 512
TB_MIX = 512
TB_FFN = 512


def _dot(a, b):
    return jnp.dot(a, b, preferred_element_type=F32)


def _dot_nt(a, b):
    return lax.dot_general(a, b, (((1,), (1,)), ((), ())), preferred_element_type=F32)


def _dot_tn(a, b):
    return lax.dot_general(a, b, (((0,), (0,)), ((), ())), preferred_element_type=F32)


def _ln(x):
    mu = jnp.mean(x, axis=-1, keepdims=True)
    xc = x - mu
    var = jnp.mean(xc * xc, axis=-1, keepdims=True)
    return xc * lax.rsqrt(var + LN_EPS)


def _const_spec(shape):
    nd = len(shape)
    return pl.BlockSpec(shape, lambda *_: (0,) * nd, pipeline_mode=pl.Buffered(1))


def _pair_diag(r):
    row = lax.broadcasted_iota(jnp.int32, (CHUNK, DV), 0)
    return jnp.where(row < DK, r[:, :DV], r[:, DV:])


def _chunk_kv(kd, v):
    outs = []
    for p in range(PAIRS):
        r = _dot_tn(kd[:, p * 128:(p + 1) * 128], v[:, p * 256:(p + 1) * 256])
        outs.append(_pair_diag(r))
    return jnp.concatenate(outs, axis=0)


def _adaln_kernel(c_ref, w_ref, b_ref, o_ref):
    c = c_ref[...]
    s = c * jax.nn.sigmoid(c)
    o_ref[...] = _dot(s, w_ref[...]) + b_ref[...]


def _adaln(cc, w_ada, b_ada):
    n = w_ada.shape[1]
    bn = 1536
    return pl.pallas_call(
        _adaln_kernel,
        grid=(n // bn,),
        in_specs=[pl.BlockSpec((MOD_ROWS, D_MODEL), lambda i: (0, 0)),
                  pl.BlockSpec((D_MODEL, bn), lambda i: (0, i)),
                  pl.BlockSpec((1, bn), lambda i: (0, i))],
        out_specs=pl.BlockSpec((MOD_ROWS, bn), lambda i: (0, i)),
        out_shape=jax.ShapeDtypeStruct((MOD_ROWS, n), F32),
        compiler_params=pltpu.CompilerParams(vmem_limit_bytes=VMEM_LIMIT),
        name="adaln",
    )(cc, w_ada, b_ada)


def _ctx_kernel(ctx_ref, sh_ref, sc_ref, wkv_ref, kdf_ref, kdb_ref, cdf_ref, cdb_ref, sf_ref, sb_ref):
    x = ctx_ref[0]
    u = (_ln(x) * (1.0 + sc_ref[...]) + sh_ref[...]).astype(BF16)
    kv = _dot(u, wkv_ref[...])
    k = kv[:, :QK_W] * (DK ** -0.5)
    v = kv[:, QK_W:].astype(BF16)
    n = x.shape[0] // CHUNK
    sf = jnp.zeros((QK_W, DV), F32)
    for c in range(n):
        kc = k[c * CHUNK:(c + 1) * CHUNK]
        vc = v[c * CHUNK:(c + 1) * CHUNK]
        sf = cdf_ref[...] * sf + _chunk_kv((kc * kdf_ref[...]).astype(BF16), vc)
    sb = jnp.zeros((QK_W, DV), F32)
    for c in reversed(range(n)):
        kc = k[c * CHUNK:(c + 1) * CHUNK]
        vc = v[c * CHUNK:(c + 1) * CHUNK]
        sb = cdb_ref[...] * sb + _chunk_kv((kc * kdb_ref[...]).astype(BF16), vc)
    sf_ref[0] = sf
    sb_ref[0] = sb


def _ctx_states(ctx, sh, sc, w_kv, kdf, kdb, cdf, cdb):
    B, Lc, _ = ctx.shape
    st = jax.ShapeDtypeStruct((B, QK_W, DV), F32)
    return pl.pallas_call(
        _ctx_kernel,
        grid=(B,),
        in_specs=[pl.BlockSpec((1, Lc, D_MODEL), lambda b: (b, 0, 0)),
                  _const_spec((1, D_MODEL)), _const_spec((1, D_MODEL)),
                  _const_spec((D_MODEL, KV_W)),
                  _const_spec((CHUNK, QK_W)), _const_spec((CHUNK, QK_W)),
                  _const_spec((QK_W, DV)), _const_spec((QK_W, DV))],
        out_specs=[pl.BlockSpec((1, QK_W, DV), lambda b: (b, 0, 0)),
                   pl.BlockSpec((1, QK_W, DV), lambda b: (b, 0, 0))],
        out_shape=[st, st],
        compiler_params=pltpu.CompilerParams(vmem_limit_bytes=VMEM_LIMIT),
        name="ctx_states",
    )(ctx, sh, sc, w_kv, kdf, kdb, cdf, cdb)


def _kv_kernel(x_ref, sh_ref, sc_ref, wkv_ref, kdb_ref, cdb_ref, sb0_ref, kv_ref, sb_ref, s_ref):
    @pl.when(pl.program_id(1) == 0)
    def _():
        s_ref[...] = sb0_ref[0]

    x = x_ref[0]
    u = (_ln(x) * (1.0 + sc_ref[0]) + sh_ref[0]).astype(BF16)
    kv = _dot(u, wkv_ref[...])
    k = kv[:, :QK_W] * (DK ** -0.5)
    v = kv[:, QK_W:].astype(BF16)
    kv_ref[0, :, :QK_W] = k.astype(BF16)
    kv_ref[0, :, QK_W:] = v
    n = x.shape[0] // CHUNK
    for c in reversed(range(n)):
        s = s_ref[...]
        sb_ref[0, c] = s.astype(BF16)
        kc = k[c * CHUNK:(c + 1) * CHUNK]
        vc = v[c * CHUNK:(c + 1) * CHUNK]
        s_ref[...] = cdb_ref[...] * s + _chunk_kv((kc * kdb_ref[...]).astype(BF16), vc)


def _kv_states(x, sh, sc, w_kv, kdb, cdb, sb0):
    B, L, _ = x.shape
    tb = TB_KV
    nb = L // tb
    nch = tb // CHUNK
    return pl.pallas_call(
        _kv_kernel,
        grid=(B, nb),
        in_specs=[pl.BlockSpec((1, tb, D_MODEL), lambda b, j: (b, nb - 1 - j, 0)),
                  pl.BlockSpec((1, 1, D_MODEL), lambda b, j: (b, 0, 0)),
                  pl.BlockSpec((1, 1, D_MODEL), lambda b, j: (b, 0, 0)),
                  _const_spec((D_MODEL, KV_W)),
                  _const_spec((CHUNK, QK_W)),
                  _const_spec((QK_W, DV)),
                  pl.BlockSpec((1, QK_W, DV), lambda b, j: (b, 0, 0))],
        out_specs=[pl.BlockSpec((1, tb, KV_W), lambda b, j: (b, nb - 1 - j, 0)),
                   pl.BlockSpec((1, nch, QK_W, DV), lambda b, j: (b, nb - 1 - j, 0, 0))],
        out_shape=[jax.ShapeDtypeStruct((B, L, KV_W), BF16),
                   jax.ShapeDtypeStruct((B, L // CHUNK, QK_W, DV), BF16)],
        scratch_shapes=[pltpu.VMEM((QK_W, DV), F32)],
        compiler_params=pltpu.CompilerParams(
            dimension_semantics=("arbitrary", "arbitrary"), vmem_limit_bytes=VMEM_LIMIT),
        name="kv_states",
    )(x, sh, sc, w_kv, kdb, cdb, sb0)


def _pool_features(pe, j, nb, tb, seq_len, poolw_ref, pscale_ref):
    n = pe.shape[0]
    t = j * tb + lax.broadcasted_iota(jnp.int32, (tb, POOL_GD), 0)
    outs = []
    for gi, w in enumerate(POOL_WINDOWS):
        half = w // 2
        a = pe[:, gi * POOL_GD:(gi + 1) * POOL_GD]
        centre = a[POOL_HALO:POOL_HALO + tb]
        s = a
        step = 1
        while step < w:
            s = s + pltpu.roll(s, n - step, axis=0)
            step *= 2
        s = pltpu.roll(s, half, axis=0)[POOL_HALO:POOL_HALO + tb]
        cnt = (jnp.minimum(t + half, seq_len) - jnp.maximum(t - half, 0)).astype(F32)
        diff = (s / cnt - centre).astype(BF16)
        outs.append(_dot(diff, poolw_ref[gi]))
    return jnp.concatenate(outs, axis=-1) * pscale_ref[...]


def _mixer_kernel(x_ref, xp_ref, xn_ref, kv_ref, sb_ref, sf0_ref, sh_ref, sc_ref, g1_ref,
                  wr_ref, dm_ref, qdf_ref, qdb_ref, kdf_ref, cdf_ref,
                  poolw_ref, pscale_ref, wbr_ref, wbp_ref, wout_ref, lng_ref, lnb_ref,
                  o_ref, s_ref, r_ref, *, nb, seq_len):
    j = pl.program_id(1)
    tb = x_ref.shape[1]

    @pl.when(j == 0)
    def _():
        s_ref[...] = sf0_ref[0]

    scale = 1.0 + sc_ref[0]
    shift = sh_ref[0]
    x = x_ref[0]
    u = (_ln(x) * scale + shift).astype(BF16)

    q = _dot(u, wr_ref[:, COL_Q:COL_G])
    g = _dot(u, wr_ref[:, COL_G:COL_P])

    lane = lax.broadcasted_iota(jnp.int32, (CHUNK, 128), 1)
    zv = jnp.zeros((CHUNK, DV), BF16)
    zs = jnp.zeros((DK, DV), BF16)

    def block_diag_state(s):
        left = jnp.concatenate([s[:DK], zs], axis=0)
        right = jnp.concatenate([zs, s[DK:]], axis=0)
        return jnp.concatenate([left, right], axis=1)

    for c in range(tb // CHUNK):
        rows = slice(c * CHUNK, (c + 1) * CHUNK)
        qc = q[rows]
        qb16 = qc.astype(BF16)
        qf = (qc * qdf_ref[...]).astype(BF16)
        qb = (qc * qdb_ref[...]).astype(BF16)
        kc = kv_ref[0, rows, 0:QK_W].astype(F32)
        vc = kv_ref[0, rows, QK_W:KV_W]
        kd = (kc * kdf_ref[...]).astype(BF16)
        ys = []
        for p in range(PAIRS):
            ql = slice(p * 128, (p + 1) * 128)
            kp = kc[:, ql]
            k_lo = jnp.where(lane < DK, kp, 0.0).astype(BF16)
            k_hi = jnp.where(lane >= DK, kp, 0.0).astype(BF16)
            krhs = jnp.concatenate([k_lo, k_hi], axis=0)
            sc = (_dot_nt(qb16[:, ql], krhs) * dm_ref[p]).astype(BF16)
            vp = vc[:, p * 256:(p + 1) * 256]
            vrhs = jnp.concatenate([jnp.concatenate([vp[:, :DV], zv], axis=1),
                                    jnp.concatenate([zv, vp[:, DV:]], axis=1)], axis=0)
            y = _dot(sc, vrhs)
            s_f = s_ref[ql, :]
            srhs = jnp.concatenate([block_diag_state(s_f.astype(BF16)),
                                    block_diag_state(sb_ref[0, c, ql, :])], axis=0)
            qlhs = jnp.concatenate([qf[:, ql], qb[:, ql]], axis=1)
            y = y + _dot(qlhs, srhs)
            ys.append(y)
            s_ref[ql, :] = cdf_ref[ql, :] * s_f + _pair_diag(_dot_tn(kd[:, ql], vp))
        gc = g[rows]
        sg = gc * jax.nn.sigmoid(gc)
        for p in range(PAIRS):
            for hh in range(2):
                h = 2 * p + hh
                yh = ys[p][:, hh * DV:(hh + 1) * DV]
                mu = jnp.mean(yh, axis=-1, keepdims=True)
                yc = yh - mu
                var = jnp.mean(yc * yc, axis=-1, keepdims=True)
                yn = yc * lax.rsqrt(var + LN_EPS)
                r_ref[rows, h * DV:(h + 1) * DV] = (yn * sg[:, h * DV:(h + 1) * DV]).astype(BF16)

    ret = _dot(r_ref[...], wbr_ref[...])

    uh = (_ln(jnp.concatenate([xp_ref[0], xn_ref[0]], axis=0)) * scale + shift).astype(BF16)
    ph = _dot(uh, wr_ref[:, COL_P:COL_GA])
    p_prev = ph[:POOL_HALO] * (j > 0).astype(F32)
    p_next = ph[POOL_HALO:] * (j < nb - 1).astype(F32)
    pm = _dot(u, wr_ref[:, COL_P:COL_GA])
    pe = jnp.concatenate([p_prev, pm, p_next], axis=0)
    feat = _pool_features(pe, j, nb, tb, seq_len, poolw_ref, pscale_ref).astype(BF16)
    pool = _dot(feat, wbp_ref[...])

    ga = _dot(u, wr_ref[:, COL_GA:COL_GB])
    merged = jax.nn.sigmoid(ga) * ret
    gb = _dot(u, wr_ref[:, COL_GB:IN_W])
    merged = (merged + jax.nn.sigmoid(gb) * pool).astype(BF16)
    mix = _dot(merged, wout_ref[...])
    z = ALPHA * x + g1_ref[0] * mix
    o_ref[0] = _ln(z) * lng_ref[...] + lnb_ref[...]


def _mixer(x, kv, sb, sf0, sh, sc, g1, w_rest, dm, qdf, qdb, kdf, cdf,
           pool_w, pool_scale, w_br, w_bp, w_out, ln_g, ln_b):
    B, L, _ = x.shape
    tb = TB_MIX
    nb = L // tb
    nch = tb // CHUNK
    hb = tb // POOL_HALO
    nh = L // POOL_HALO
    mod_spec = pl.BlockSpec((1, 1, D_MODEL), lambda b, j: (b, 0, 0))
    return pl.pallas_call(
        functools.partial(_mixer_kernel, nb=nb, seq_len=L),
        grid=(B, nb),
        in_specs=[pl.BlockSpec((1, tb, D_MODEL), lambda b, j: (b, j, 0)),
                  pl.BlockSpec((1, POOL_HALO, D_MODEL), lambda b, j: (b, jnp.maximum(j * hb - 1, 0), 0)),
                  pl.BlockSpec((1, POOL_HALO, D_MODEL),
                               lambda b, j: (b, jnp.minimum((j + 1) * hb, nh - 1), 0)),
                  pl.BlockSpec((1, tb, KV_W), lambda b, j: (b, j, 0)),
                  pl.BlockSpec((1, nch, QK_W, DV), lambda b, j: (b, j, 0, 0)),
                  pl.BlockSpec((1, QK_W, DV), lambda b, j: (b, 0, 0)),
                  mod_spec, mod_spec, mod_spec,
                  _const_spec((D_MODEL, IN_W)),
                  _const_spec((PAIRS, CHUNK, 256)),
                  _const_spec((CHUNK, QK_W)), _const_spec((CHUNK, QK_W)), _const_spec((CHUNK, QK_W)),
                  _const_spec((QK_W, DV)),
                  _const_spec((len(POOL_WINDOWS), POOL_GD, POOL_GD)),
                  _const_spec((1, POOL_W)),
                  _const_spec((V_W, D_MODEL)),
                  _const_spec((POOL_W, D_MODEL)),
                  _const_spec((D_MODEL, D_MODEL)),
                  _const_spec((1, D_MODEL)), _const_spec((1, D_MODEL))],
        out_specs=pl.BlockSpec((1, tb, D_MODEL), lambda b, j: (b, j, 0)),
        out_shape=jax.ShapeDtypeStruct((B, L, D_MODEL), F32),
        scratch_shapes=[pltpu.VMEM((QK_W, DV), F32), pltpu.VMEM((tb, V_W), BF16)],
        compiler_params=pltpu.CompilerParams(
            dimension_semantics=("arbitrary", "arbitrary"), vmem_limit_bytes=VMEM_LIMIT),
        name="mixer",
    )(x, x, x, kv, sb, sf0, sh, sc, g1, w_rest, dm, qdf, qdb, kdf, cdf,
      pool_w, pool_scale, w_br, w_bp, w_out, ln_g, ln_b)


def _ffn_kernel(x_ref, xp_ref, xn_ref, sh_ref, sc_ref, g2_ref, wup_ref, cw_ref, wdn_ref,
                lng_ref, lnb_ref, o_ref, u_ref, acc_ref, ha_ref, hb_ref, *, nb):
    j = pl.program_id(1)
    tb = x_ref.shape[1]
    n = tb + 2 * GRID_W
    scale = 1.0 + sc_ref[0]
    shift = sh_ref[0]
    x = x_ref[0]

    def mod(v):
        return _ln(v) * scale + shift

    u_ref[0:GRID_W] = (mod(xp_ref[0]) * (j > 0).astype(F32)).astype(BF16)
    u_ref[GRID_W:GRID_W + tb] = mod(x).astype(BF16)
    u_ref[GRID_W + tb:n] = (mod(xn_ref[0]) * (j < nb - 1).astype(F32)).astype(BF16)

    col = lax.broadcasted_iota(jnp.int32, (tb, FF_CW), 0) & (GRID_W - 1)
    has_left = col > 0
    has_right = col < GRID_W - 1

    pad = jnp.zeros((FF_PAD, FF_CW), F32)
    for h_ref in (ha_ref, hb_ref):
        for ab in range(2):
            h_ref[ab, 0:FF_PAD] = pad
            h_ref[ab, FF_PAD + n:FF_PAD + n + FF_PAD] = pad

    def lanes(c, half):
        return pl.ds(pl.multiple_of(half * D_FF + c * FF_CW, FF_CW), FF_CW)

    def up(c, h_ref):
        u = u_ref[...]
        h_ref[0, FF_PAD:FF_PAD + n] = _dot(u, wup_ref[:, lanes(c, 0)])
        h_ref[1, FF_PAD:FF_PAD + n] = _dot(u, wup_ref[:, lanes(c, 1)])

    def conv(h_ref, ab, cw):
        cols = []
        for dc in range(3):
            g = None
            for dr in range(3):
                start = FF_PAD + dr * GRID_W + dc - 1
                term = cw[3 * dr + dc:3 * dr + dc + 1] * h_ref[ab, start:start + tb]
                g = term if g is None else g + term
            cols.append(g)
        return (cols[1] + cw[9:10]) + (jnp.where(has_left, cols[0], 0.0) + jnp.where(has_right, cols[2], 0.0))

    def down(c, h_ref):
        a = conv(h_ref, 0, cw_ref[:, lanes(c, 0)])
        b = conv(h_ref, 1, cw_ref[:, lanes(c, 1)])
        gated = (jax.nn.gelu(a) * b).astype(BF16)
        acc_ref[...] += _dot(gated, wdn_ref[c])

    acc_ref[...] = jnp.zeros_like(acc_ref)
    up(0, ha_ref)

    def body(i, carry):
        c = 2 * i
        up(c + 1, hb_ref)
        down(c, ha_ref)
        up(c + 2, ha_ref)
        down(c + 1, hb_ref)
        return carry

    lax.fori_loop(0, FF_NC // 2, body, 0)
    down(FF_NC - 1, ha_ref)
    z = ALPHA * x + g2_ref[0] * acc_ref[...]
    o_ref[0] = _ln(z) * lng_ref[...] + lnb_ref[...]


def _ffn(x, sh, sc, g2, w_up, conv_wb, w_down, ln_g, ln_b):
    B, L, _ = x.shape
    tb = TB_FFN
    nb = L // tb
    hb = tb // GRID_W
    nh = L // GRID_W
    mod_spec = pl.BlockSpec((1, 1, D_MODEL), lambda b, j: (b, 0, 0))
    return pl.pallas_call(
        functools.partial(_ffn_kernel, nb=nb),
        grid=(B, nb),
        in_specs=[pl.BlockSpec((1, tb, D_MODEL), lambda b, j: (b, j, 0)),
                  pl.BlockSpec((1, GRID_W, D_MODEL), lambda b, j: (b, jnp.maximum(j * hb - 1, 0), 0)),
                  pl.BlockSpec((1, GRID_W, D_MODEL),
                               lambda b, j: (b, jnp.minimum((j + 1) * hb, nh - 1), 0)),
                  mod_spec, mod_spec, mod_spec,
                  _const_spec((D_MODEL, 2 * D_FF)),
                  _const_spec((16, 2 * D_FF)),
                  _const_spec((FF_NC, FF_CW, D_MODEL)),
                  _const_spec((1, D_MODEL)), _const_spec((1, D_MODEL))],
        out_specs=pl.BlockSpec((1, tb, D_MODEL), lambda b, j: (b, j, 0)),
        out_shape=jax.ShapeDtypeStruct((B, L, D_MODEL), F32),
        scratch_shapes=[pltpu.VMEM((tb + 2 * GRID_W, D_MODEL), BF16),
                        pltpu.VMEM((tb, D_MODEL), F32),
                        pltpu.VMEM((2, tb + 2 * GRID_W + 2 * FF_PAD, FF_CW), F32),
                        pltpu.VMEM((2, tb + 2 * GRID_W + 2 * FF_PAD, FF_CW), F32)],
        compiler_params=pltpu.CompilerParams(
            dimension_semantics=("arbitrary", "arbitrary"), vmem_limit_bytes=VMEM_LIMIT),
        name="conv_ffn",
    )(x, x, x, sh, sc, g2, w_up, conv_wb, w_down, ln_g, ln_b)


def _decay_tables(ret_decay_logit):
    lg = jax.nn.log_sigmoid(ret_decay_logit.astype(F32))
    pos = jnp.arange(CHUNK, dtype=F32)
    diff = pos[:, None] - pos[None, :]
    d_f = jnp.where(diff[None] >= 0, jnp.exp(jnp.maximum(diff, 0.0)[None] * lg[0][:, None, None]), 0.0)
    d_b = jnp.where(diff[None] <= 0, jnp.exp(jnp.maximum(-diff, 0.0)[None] * lg[1][:, None, None]), 0.0)
    dm = (d_f + d_b).reshape(PAIRS, 2, CHUNK, CHUNK).transpose(0, 2, 1, 3).reshape(PAIRS, CHUNK, 2 * CHUNK)

    def lanes(t):
        return jnp.repeat(t, DK, axis=1)

    qdf = lanes(jnp.exp((pos + 1.0)[:, None] * lg[0][None, :]))
    qdb = lanes(jnp.exp((CHUNK - pos)[:, None] * lg[1][None, :]))
    kdf = lanes(jnp.exp((CHUNK - 1.0 - pos)[:, None] * lg[0][None, :]))
    kdb = lanes(jnp.exp(pos[:, None] * lg[1][None, :]))

    def rows(t):
        return jnp.broadcast_to(jnp.repeat(t, DK)[:, None], (QK_W, DV))

    cdf = rows(jnp.exp(CHUNK * lg[0]))
    cdb = rows(jnp.exp(CHUNK * lg[1]))
    return dm, qdf, qdb, kdf, kdb, cdf, cdb


def kernel(x, c, ctx, c_ctx, w_ada, b_ada, w_in, ret_decay_logit, pool_w, pool_scale, w_branch_ret,
           w_branch_pool, w_out, ln1_g, ln1_b, w_up, conv_w, conv_b, w_down, ln2_g, ln2_b):
    B = x.shape[0]
    D = D_MODEL
    assert w_ada.shape[0] == 1, "single-layer stack"

    cc = jnp.zeros((MOD_ROWS, D), F32).at[:B].set(c).at[B].set(c_ctx)
    mod = _adaln(cc, w_ada[0], b_ada[0][None, :])
    lat = mod[:B].reshape(B, N_MOD, 1, D)
    sh1, sc1, g1, sh2, sc2, g2 = (lat[:, i] for i in range(N_MOD))
    sh1c = mod[B:B + 1, 0:D]
    sc1c = mod[B:B + 1, D:2 * D]

    dm, qdf, qdb, kdf, kdb, cdf, cdb = _decay_tables(ret_decay_logit[0])

    w_in_b = w_in[0].astype(BF16)

    s_f, s_b = _ctx_states(ctx, sh1c, sc1c, w_in_b, kdf, kdb, cdf, cdb)
    kv, sb = _kv_states(x, sh1, sc1, w_in_b, kdb, cdb, s_b)
    x1 = _mixer(x, kv, sb, s_f, sh1, sc1, g1, w_in_b, dm, qdf, qdb, kdf, cdf,
                pool_w[0].astype(BF16), pool_scale[0][None, :],
                w_branch_ret[0].astype(BF16), w_branch_pool[0].astype(BF16), w_out[0].astype(BF16),
                ln1_g[0][None, :], ln1_b[0][None, :])

    conv_wb = jnp.concatenate([conv_w[0].reshape(9, 2 * D_FF), conv_b[0][None, :],
                               jnp.zeros((6, 2 * D_FF), F32)], axis=0)
    w_down_c = w_down[0].astype(BF16).reshape(FF_NC, FF_CW, D)
    return _ffn(x1, sh2, sc2, g2, w_up[0].astype(BF16), conv_wb, w_down_c,
                ln2_g[0][None, :], ln2_b[0][None, :])
```

```python
import functools

import jax
import jax.numpy as jnp
import numpy as np
from jax import lax
from jax.experimental import pallas as pl
from jax.experimental.pallas import tpu as pltpu

F32 = jnp.float32
BF16 = jnp.bfloat16

D_MODEL = 1024
GRID_W = 64
HEADS = 8
DK = 64
DV = 128
QK_W = HEADS * DK
V_W = HEADS * DV
KV_W = QK_W + V_W
CHUNK = 128
PAIRS = HEADS // 2
POOL_WINDOWS = (2, 4, 8, 16)
POOL_GD = 128
POOL_W = 512
COL_Q = KV_W
COL_G = COL_Q + QK_W
COL_P = COL_G + V_W
COL_GA = COL_P + POOL_W
COL_GB = COL_GA + D_MODEL
IN_W = COL_GB + D_MODEL
D_FF = 2816
FF_CW = 256
FF_NC = D_FF // FF_CW
FF_PAD = 8
N_MOD = 6
LN_EPS = 1e-6
ALPHA = 2.0 ** 0.25
POOL_HALO = 8
U_HALO = 16
MOD_ROWS = 8

VMEM_LIMIT = 56 * 1024 * 1024

TB_KV = 512
TB_MIX = 512
TB_FFN = 512


def _dot(a, b):
    return jnp.dot(a, b, preferred_element_type=F32)


def _dot_nt(a, b):
    return lax.dot_general(a, b, (((1,), (1,)), ((), ())), preferred_element_type=F32)


def _dot_tn(a, b):
    return lax.dot_general(a, b, (((0,), (0,)), ((), ())), preferred_element_type=F32)


def _ln(x):
    mu = jnp.mean(x, axis=-1, keepdims=True)
    xc = x - mu
    var = jnp.mean(xc * xc, axis=-1, keepdims=True)
    return xc * lax.rsqrt(var + LN_EPS)


def _const_spec(shape):
    nd = len(shape)
    return pl.BlockSpec(shape, lambda *_: (0,) * nd, pipeline_mode=pl.Buffered(1))


def _pair_diag(r):
    row = lax.broadcasted_iota(jnp.int32, (CHUNK, DV), 0)
    return jnp.where(row < DK, r[:, :DV], r[:, DV:])


def _chunk_kv(kd, v):
    outs = []
    for p in range(PAIRS):
        r = _dot_tn(kd[:, p * 128:(p + 1) * 128], v[:, p * 256:(p + 1) * 256])
        outs.append(_pair_diag(r))
    return jnp.concatenate(outs, axis=0)


def _adaln_kernel(c_ref, w_ref, b_ref, o_ref):
    c = c_ref[...]
    s = c * jax.nn.sigmoid(c)
    o_ref[...] = _dot(s, w_ref[...]) + b_ref[...]


def _adaln(cc, w_ada, b_ada):
    n = w_ada.shape[1]
    bn = 1536
    return pl.pallas_call(
        _adaln_kernel,
        grid=(n // bn,),
        in_specs=[pl.BlockSpec((MOD_ROWS, D_MODEL), lambda i: (0, 0)),
                  pl.BlockSpec((D_MODEL, bn), lambda i: (0, i)),
                  pl.BlockSpec((1, bn), lambda i: (0, i))],
        out_specs=pl.BlockSpec((MOD_ROWS, bn), lambda i: (0, i)),
        out_shape=jax.ShapeDtypeStruct((MOD_ROWS, n), F32),
        compiler_params=pltpu.CompilerParams(vmem_limit_bytes=VMEM_LIMIT),
        name="adaln",
    )(cc, w_ada, b_ada)


def _ctx_kernel(ctx_ref, sh_ref, sc_ref, wkv_ref, kdf_ref, kdb_ref, cdf_ref, cdb_ref, sf_ref, sb_ref):
    x = ctx_ref[0]
    u = (_ln(x) * (1.0 + sc_ref[...]) + sh_ref[...]).astype(BF16)
    kv = _dot(u, wkv_ref[...])
    k = kv[:, :QK_W] * (DK ** -0.5)
    v = kv[:, QK_W:].astype(BF16)
    n = x.shape[0] // CHUNK
    sf = jnp.zeros((QK_W, DV), F32)
    for c in range(n):
        kc = k[c * CHUNK:(c + 1) * CHUNK]
        vc = v[c * CHUNK:(c + 1) * CHUNK]
        sf = cdf_ref[...] * sf + _chunk_kv((kc * kdf_ref[...]).astype(BF16), vc)
    sb = jnp.zeros((QK_W, DV), F32)
    for c in reversed(range(n)):
        kc = k[c * CHUNK:(c + 1) * CHUNK]
        vc = v[c * CHUNK:(c + 1) * CHUNK]
        sb = cdb_ref[...] * sb + _chunk_kv((kc * kdb_ref[...]).astype(BF16), vc)
    sf_ref[0] = sf
    sb_ref[0] = sb


def _ctx_states(ctx, sh, sc, w_kv, kdf, kdb, cdf, cdb):
    B, Lc, _ = ctx.shape
    st = jax.ShapeDtypeStruct((B, QK_W, DV), F32)
    return pl.pallas_call(
        _ctx_kernel,
        grid=(B,),
        in_specs=[pl.BlockSpec((1, Lc, D_MODEL), lambda b: (b, 0, 0)),
                  _const_spec((1, D_MODEL)), _const_spec((1, D_MODEL)),
                  _const_spec((D_MODEL, KV_W)),
                  _const_spec((CHUNK, QK_W)), _const_spec((CHUNK, QK_W)),
                  _const_spec((QK_W, DV)), _const_spec((QK_W, DV))],
        out_specs=[pl.BlockSpec((1, QK_W, DV), lambda b: (b, 0, 0)),
                   pl.BlockSpec((1, QK_W, DV), lambda b: (b, 0, 0))],
        out_shape=[st, st],
        compiler_params=pltpu.CompilerParams(vmem_limit_bytes=VMEM_LIMIT),
        name="ctx_states",
    )(ctx, sh, sc, w_kv, kdf, kdb, cdf, cdb)


def _kv_kernel(x_ref, sh_ref, sc_ref, wkv_ref, kdb_ref, cdb_ref, sb0_ref, u_ref, kv_ref, sb_ref, s_ref):
    @pl.when(pl.program_id(1) == 0)
    def _():
        s_ref[...] = sb0_ref[0]

    x = x_ref[0]
    u = (_ln(x) * (1.0 + sc_ref[0]) + sh_ref[0]).astype(BF16)
    u_ref[0] = u
    kv = _dot(u, wkv_ref[...])
    k = kv[:, :QK_W] * (DK ** -0.5)
    v = kv[:, QK_W:].astype(BF16)
    kv_ref[0, :, :QK_W] = k.astype(BF16)
    kv_ref[0, :, QK_W:] = v
    n = x.shape[0] // CHUNK
    for c in reversed(range(n)):
        s = s_ref[...]
        sb_ref[0, c] = s.astype(BF16)
        kc = k[c * CHUNK:(c + 1) * CHUNK]
        vc = v[c * CHUNK:(c + 1) * CHUNK]
        s_ref[...] = cdb_ref[...] * s + _chunk_kv((kc * kdb_ref[...]).astype(BF16), vc)


def _kv_states(x, sh, sc, w_kv, kdb, cdb, sb0):
    B, L, _ = x.shape
    tb = TB_KV
    nb = L // tb
    nch = tb // CHUNK
    return pl.pallas_call(
        _kv_kernel,
        grid=(B, nb),
        in_specs=[pl.BlockSpec((1, tb, D_MODEL), lambda b, j: (b, nb - 1 - j, 0)),
                  pl.BlockSpec((1, 1, D_MODEL), lambda b, j: (b, 0, 0)),
                  pl.BlockSpec((1, 1, D_MODEL), lambda b, j: (b, 0, 0)),
                  _const_spec((D_MODEL, KV_W)),
                  _const_spec((CHUNK, QK_W)),
                  _const_spec((QK_W, DV)),
                  pl.BlockSpec((1, QK_W, DV), lambda b, j: (b, 0, 0))],
        out_specs=[pl.BlockSpec((1, tb, D_MODEL), lambda b, j: (b, nb - 1 - j, 0)),
                   pl.BlockSpec((1, tb, KV_W), lambda b, j: (b, nb - 1 - j, 0)),
                   pl.BlockSpec((1, nch, QK_W, DV), lambda b, j: (b, nb - 1 - j, 0, 0))],
        out_shape=[jax.ShapeDtypeStruct((B, L, D_MODEL), BF16),
                   jax.ShapeDtypeStruct((B, L, KV_W), BF16),
                   jax.ShapeDtypeStruct((B, L // CHUNK, QK_W, DV), BF16)],
        scratch_shapes=[pltpu.VMEM((QK_W, DV), F32)],
        compiler_params=pltpu.CompilerParams(
            dimension_semantics=("arbitrary", "arbitrary"), vmem_limit_bytes=VMEM_LIMIT),
        name="kv_states",
    )(x, sh, sc, w_kv, kdb, cdb, sb0)


def _pool_features(pe, j, nb, tb, seq_len, poolw_ref, pscale_ref):
    n = pe.shape[0]
    t = j * tb + lax.broadcasted_iota(jnp.int32, (tb, POOL_GD), 0)
    outs = []
    for gi, w in enumerate(POOL_WINDOWS):
        half = w // 2
        a = pe[:, gi * POOL_GD:(gi + 1) * POOL_GD]
        centre = a[POOL_HALO:POOL_HALO + tb]
        s = a
        step = 1
        while step < w:
            s = s + pltpu.roll(s, n - step, axis=0)
            step *= 2
        s = pltpu.roll(s, half, axis=0)[POOL_HALO:POOL_HALO + tb]
        cnt = (jnp.minimum(t + half, seq_len) - jnp.maximum(t - half, 0)).astype(F32)
        diff = (s / cnt - centre).astype(BF16)
        outs.append(_dot(diff, poolw_ref[gi]))
    return jnp.concatenate(outs, axis=-1) * pscale_ref[...]


def _mixer_kernel(x_ref, u_ref, up_ref, un_ref, kv_ref, sb_ref, sf0_ref, g1_ref,
                  wr_ref, dm_ref, qdf_ref, qdb_ref, kdf_ref, cdf_ref,
                  poolw_ref, pscale_ref, wbr_ref, wbp_ref, wout_ref, lng_ref, lnb_ref,
                  o_ref, s_ref, r_ref, *, nb, seq_len):
    j = pl.program_id(1)
    tb = x_ref.shape[1]

    @pl.when(j == 0)
    def _():
        s_ref[...] = sf0_ref[0]

    x = x_ref[0]
    u = u_ref[0]

    q = _dot(u, wr_ref[:, COL_Q:COL_G])
    g = _dot(u, wr_ref[:, COL_G:COL_P])

    lane = lax.broadcasted_iota(jnp.int32, (CHUNK, 128), 1)
    zv = jnp.zeros((CHUNK, DV), BF16)
    zs = jnp.zeros((DK, DV), BF16)

    def block_diag_state(s):
        left = jnp.concatenate([s[:DK], zs], axis=0)
        right = jnp.concatenate([zs, s[DK:]], axis=0)
        return jnp.concatenate([left, right], axis=1)

    for c in range(tb // CHUNK):
        rows = slice(c * CHUNK, (c + 1) * CHUNK)
        qc = q[rows]
        qb16 = qc.astype(BF16)
        qf = (qc * qdf_ref[...]).astype(BF16)
        qb = (qc * qdb_ref[...]).astype(BF16)
        kc = kv_ref[0, rows, 0:QK_W].astype(F32)
        vc = kv_ref[0, rows, QK_W:KV_W]
        kd = (kc * kdf_ref[...]).astype(BF16)
        ys = []
        for p in range(PAIRS):
            ql = slice(p * 128, (p + 1) * 128)
            kp = kc[:, ql]
            k_lo = jnp.where(lane < DK, kp, 0.0).astype(BF16)
            k_hi = jnp.where(lane >= DK, kp, 0.0).astype(BF16)
            krhs = jnp.concatenate([k_lo, k_hi], axis=0)
            sc = (_dot_nt(qb16[:, ql], krhs) * dm_ref[p]).astype(BF16)
            vp = vc[:, p * 256:(p + 1) * 256]
            vrhs = jnp.concatenate([jnp.concatenate([vp[:, :DV], zv], axis=1),
                                    jnp.concatenate([zv, vp[:, DV:]], axis=1)], axis=0)
            y = _dot(sc, vrhs)
            s_f = s_ref[ql, :]
            srhs = jnp.concatenate([block_diag_state(s_f.astype(BF16)),
                                    block_diag_state(sb_ref[0, c, ql, :])], axis=0)
            qlhs = jnp.concatenate([qf[:, ql], qb[:, ql]], axis=1)
            y = y + _dot(qlhs, srhs)
            ys.append(y)
            s_ref[ql, :] = cdf_ref[ql, :] * s_f + _pair_diag(_dot_tn(kd[:, ql], vp))
        gc = g[rows]
        sg = gc * jax.nn.sigmoid(gc)
        for p in range(PAIRS):
            for hh in range(2):
                h = 2 * p + hh
                yh = ys[p][:, hh * DV:(hh + 1) * DV]
                mu = jnp.mean(yh, axis=-1, keepdims=True)
                yc = yh - mu
                var = jnp.mean(yc * yc, axis=-1, keepdims=True)
                yn = yc * lax.rsqrt(var + LN_EPS)
                r_ref[rows, h * DV:(h + 1) * DV] = (yn * sg[:, h * DV:(h + 1) * DV]).astype(BF16)

    ret = _dot(r_ref[...], wbr_ref[...])

    ph = _dot(jnp.concatenate([up_ref[0], un_ref[0]], axis=0), wr_ref[:, COL_P:COL_GA])
    p_prev = ph[U_HALO - POOL_HALO:U_HALO] * (j > 0).astype(F32)
    p_next = ph[U_HALO:U_HALO + POOL_HALO] * (j < nb - 1).astype(F32)
    pm = _dot(u, wr_ref[:, COL_P:COL_GA])
    pe = jnp.concatenate([p_prev, pm, p_next], axis=0)
    feat = _pool_features(pe, j, nb, tb, seq_len, poolw_ref, pscale_ref).astype(BF16)
    pool = _dot(feat, wbp_ref[...])

    ga = _dot(u, wr_ref[:, COL_GA:COL_GB])
    merged = jax.nn.sigmoid(ga) * ret
    gb = _dot(u, wr_ref[:, COL_GB:IN_W])
    merged = (merged + jax.nn.sigmoid(gb) * pool).astype(BF16)
    mix = _dot(merged, wout_ref[...])
    z = ALPHA * x + g1_ref[0] * mix
    o_ref[0] = _ln(z) * lng_ref[...] + lnb_ref[...]


def _mixer(x, u, kv, sb, sf0, g1, w_rest, dm, qdf, qdb, kdf, cdf,
           pool_w, pool_scale, w_br, w_bp, w_out, ln_g, ln_b):
    B, L, _ = x.shape
    tb = TB_MIX
    nb = L // tb
    nch = tb // CHUNK
    hb = tb // U_HALO
    nh = L // U_HALO
    mod_spec = pl.BlockSpec((1, 1, D_MODEL), lambda b, j: (b, 0, 0))
    return pl.pallas_call(
        functools.partial(_mixer_kernel, nb=nb, seq_len=L),
        grid=(B, nb),
        in_specs=[pl.BlockSpec((1, tb, D_MODEL), lambda b, j: (b, j, 0)),
                  pl.BlockSpec((1, tb, D_MODEL), lambda b, j: (b, j, 0)),
                  pl.BlockSpec((1, U_HALO, D_MODEL), lambda b, j: (b, jnp.maximum(j * hb - 1, 0), 0)),
                  pl.BlockSpec((1, U_HALO, D_MODEL),
                               lambda b, j: (b, jnp.minimum((j + 1) * hb, nh - 1), 0)),
                  pl.BlockSpec((1, tb, KV_W), lambda b, j: (b, j, 0)),
                  pl.BlockSpec((1, nch, QK_W, DV), lambda b, j: (b, j, 0, 0)),
                  pl.BlockSpec((1, QK_W, DV), lambda b, j: (b, 0, 0)),
                  mod_spec,
                  _const_spec((D_MODEL, IN_W)),
                  _const_spec((PAIRS, CHUNK, 256)),
                  _const_spec((CHUNK, QK_W)), _const_spec((CHUNK, QK_W)), _const_spec((CHUNK, QK_W)),
                  _const_spec((QK_W, DV)),
                  _const_spec((len(POOL_WINDOWS), POOL_GD, POOL_GD)),
                  _const_spec((1, POOL_W)),
                  _const_spec((V_W, D_MODEL)),
                  _const_spec((POOL_W, D_MODEL)),
                  _const_spec((D_MODEL, D_MODEL)),
                  _const_spec((1, D_MODEL)), _const_spec((1, D_MODEL))],
        out_specs=pl.BlockSpec((1, tb, D_MODEL), lambda b, j: (b, j, 0)),
        out_shape=jax.ShapeDtypeStruct((B, L, D_MODEL), F32),
        scratch_shapes=[pltpu.VMEM((QK_W, DV), F32), pltpu.VMEM((tb, V_W), BF16)],
        compiler_params=pltpu.CompilerParams(
            dimension_semantics=("arbitrary", "arbitrary"), vmem_limit_bytes=VMEM_LIMIT),
        name="mixer",
    )(x, u, u, u, kv, sb, sf0, g1, w_rest, dm, qdf, qdb, kdf, cdf,
      pool_w, pool_scale, w_br, w_bp, w_out, ln_g, ln_b)


def _ffn_kernel(x_ref, xp_ref, xn_ref, sh_ref, sc_ref, g2_ref, wup_ref, cw_ref, wdn_ref,
                lng_ref, lnb_ref, o_ref, u_ref, acc_ref, ha_ref, hb_ref, *, nb):
    j = pl.program_id(1)
    tb = x_ref.shape[1]
    n = tb + 2 * GRID_W
    scale = 1.0 + sc_ref[0]
    shift = sh_ref[0]
    x = x_ref[0]

    def mod(v):
        return _ln(v) * scale + shift

    u_ref[0:GRID_W] = (mod(xp_ref[0]) * (j > 0).astype(F32)).astype(BF16)
    u_ref[GRID_W:GRID_W + tb] = mod(x).astype(BF16)
    u_ref[GRID_W + tb:n] = (mod(xn_ref[0]) * (j < nb - 1).astype(F32)).astype(BF16)

    col = lax.broadcasted_iota(jnp.int32, (tb, FF_CW), 0) & (GRID_W - 1)
    has_left = col > 0
    has_right = col < GRID_W - 1

    pad = jnp.zeros((FF_PAD, FF_CW), F32)
    for h_ref in (ha_ref, hb_ref):
        for ab in range(2):
            h_ref[ab, 0:FF_PAD] = pad
            h_ref[ab, FF_PAD + n:FF_PAD + n + FF_PAD] = pad

    def lanes(c, half):
        return pl.ds(pl.multiple_of(half * D_FF + c * FF_CW, FF_CW), FF_CW)

    def up(c, h_ref):
        u = u_ref[...]
        h_ref[0, FF_PAD:FF_PAD + n] = _dot(u, wup_ref[:, lanes(c, 0)])
        h_ref[1, FF_PAD:FF_PAD + n] = _dot(u, wup_ref[:, lanes(c, 1)])

    def conv(h_ref, ab, cw):
        cols = []
        for dc in range(3):
            g = None
            for dr in range(3):
                start = FF_PAD + dr * GRID_W + dc - 1
                term = cw[3 * dr + dc:3 * dr + dc + 1] * h_ref[ab, start:start + tb]
                g = term if g is None else g + term
            cols.append(g)
        return (cols[1] + cw[9:10]) + (jnp.where(has_left, cols[0], 0.0) + jnp.where(has_right, cols[2], 0.0))

    def down(c, h_ref):
        a = conv(h_ref, 0, cw_ref[:, lanes(c, 0)])
        b = conv(h_ref, 1, cw_ref[:, lanes(c, 1)])
        gated = (jax.nn.gelu(a) * b).astype(BF16)
        acc_ref[...] += _dot(gated, wdn_ref[c])

    acc_ref[...] = jnp.zeros_like(acc_ref)
    up(0, ha_ref)

    def body(i, carry):
        c = 2 * i
        up(c + 1, hb_ref)
        down(c, ha_ref)
        up(c + 2, ha_ref)
        down(c + 1, hb_ref)
        return carry

    lax.fori_loop(0, FF_NC // 2, body, 0)
    down(FF_NC - 1, ha_ref)
    z = ALPHA * x + g2_ref[0] * acc_ref[...]
    o_ref[0] = _ln(z) * lng_ref[...] + lnb_ref[...]


def _ffn(x, sh, sc, g2, w_up, conv_wb, w_down, ln_g, ln_b):
    B, L, _ = x.shape
    tb = TB_FFN
    nb = L // tb
    hb = tb // GRID_W
    nh = L // GRID_W
    mod_spec = pl.BlockSpec((1, 1, D_MODEL), lambda b, j: (b, 0, 0))
    return pl.pallas_call(
        functools.partial(_ffn_kernel, nb=nb),
        grid=(B, nb),
        in_specs=[pl.BlockSpec((1, tb, D_MODEL), lambda b, j: (b, j, 0)),
                  pl.BlockSpec((1, GRID_W, D_MODEL), lambda b, j: (b, jnp.maximum(j * hb - 1, 0), 0)),
                  pl.BlockSpec((1, GRID_W, D_MODEL),
                               lambda b, j: (b, jnp.minimum((j + 1) * hb, nh - 1), 0)),
                  mod_spec, mod_spec, mod_spec,
                  _const_spec((D_MODEL, 2 * D_FF)),
                  _const_spec((16, 2 * D_FF)),
                  _const_spec((FF_NC, FF_CW, D_MODEL)),
                  _const_spec((1, D_MODEL)), _const_spec((1, D_MODEL))],
        out_specs=pl.BlockSpec((1, tb, D_MODEL), lambda b, j: (b, j, 0)),
        out_shape=jax.ShapeDtypeStruct((B, L, D_MODEL), F32),
        scratch_shapes=[pltpu.VMEM((tb + 2 * GRID_W, D_MODEL), BF16),
                        pltpu.VMEM((tb, D_MODEL), F32),
                        pltpu.VMEM((2, tb + 2 * GRID_W + 2 * FF_PAD, FF_CW), F32),
                        pltpu.VMEM((2, tb + 2 * GRID_W + 2 * FF_PAD, FF_CW), F32)],
        compiler_params=pltpu.CompilerParams(
            dimension_semantics=("arbitrary", "arbitrary"), vmem_limit_bytes=VMEM_LIMIT),
        name="conv_ffn",
    )(x, x, x, sh, sc, g2, w_up, conv_wb, w_down, ln_g, ln_b)


def _decay_tables(ret_decay_logit):
    lg = jax.nn.log_sigmoid(ret_decay_logit.astype(F32))
    pos = jnp.arange(CHUNK, dtype=F32)
    diff = pos[:, None] - pos[None, :]
    d_f = jnp.where(diff[None] >= 0, jnp.exp(jnp.maximum(diff, 0.0)[None] * lg[0][:, None, None]), 0.0)
    d_b = jnp.where(diff[None] <= 0, jnp.exp(jnp.maximum(-diff, 0.0)[None] * lg[1][:, None, None]), 0.0)
    dm = (d_f + d_b).reshape(PAIRS, 2, CHUNK, CHUNK).transpose(0, 2, 1, 3).reshape(PAIRS, CHUNK, 2 * CHUNK)

    def lanes(t):
        return jnp.repeat(t, DK, axis=1)

    qdf = lanes(jnp.exp((pos + 1.0)[:, None] * lg[0][None, :]))
    qdb = lanes(jnp.exp((CHUNK - pos)[:, None] * lg[1][None, :]))
    kdf = lanes(jnp.exp((CHUNK - 1.0 - pos)[:, None] * lg[0][None, :]))
    kdb = lanes(jnp.exp(pos[:, None] * lg[1][None, :]))

    def rows(t):
        return jnp.broadcast_to(jnp.repeat(t, DK)[:, None], (QK_W, DV))

    cdf = rows(jnp.exp(CHUNK * lg[0]))
    cdb = rows(jnp.exp(CHUNK * lg[1]))
    return dm, qdf, qdb, kdf, kdb, cdf, cdb


def kernel(x, c, ctx, c_ctx, w_ada, b_ada, w_in, ret_decay_logit, pool_w, pool_scale, w_branch_ret,
           w_branch_pool, w_out, ln1_g, ln1_b, w_up, conv_w, conv_b, w_down, ln2_g, ln2_b):
    B = x.shape[0]
    D = D_MODEL
    assert w_ada.shape[0] == 1, "single-layer stack"

    cc = jnp.zeros((MOD_ROWS, D), F32).at[:B].set(c).at[B].set(c_ctx)
    mod = _adaln(cc, w_ada[0], b_ada[0][None, :])
    lat = mod[:B].reshape(B, N_MOD, 1, D)
    sh1, sc1, g1, sh2, sc2, g2 = (lat[:, i] for i in range(N_MOD))
    sh1c = mod[B:B + 1, 0:D]
    sc1c = mod[B:B + 1, D:2 * D]

    dm, qdf, qdb, kdf, kdb, cdf, cdb = _decay_tables(ret_decay_logit[0])

    w_in_b = w_in[0].astype(BF16)

    s_f, s_b = _ctx_states(ctx, sh1c, sc1c, w_in_b, kdf, kdb, cdf, cdb)
    u, kv, sb = _kv_states(x, sh1, sc1, w_in_b, kdb, cdb, s_b)
    x1 = _mixer(x, u, kv, sb, s_f, g1, w_in_b, dm, qdf, qdb, kdf, cdf,
                pool_w[0].astype(BF16), pool_scale[0][None, :],
                w_branch_ret[0].astype(BF16), w_branch_pool[0].astype(BF16), w_out[0].astype(BF16),
                ln1_g[0][None, :], ln1_b[0][None, :])

    conv_wb = jnp.concatenate([conv_w[0].reshape(9, 2 * D_FF), conv_b[0][None, :],
                               jnp.zeros((6, 2 * D_FF), F32)], axis=0)
    w_down_c = w_down[0].astype(BF16).reshape(FF_NC, FF_CW, D)
    return _ffn(x1, sh2, sc2, g2, w_up[0].astype(BF16), conv_wb, w_down_c,
                ln2_g[0][None, :], ln2_b[0][None, :])
```

```python
import functools

import jax
import jax.numpy as jnp
import numpy as np
from jax import lax
from jax.experimental import pallas as pl
from jax.experimental.pallas import tpu as pltpu

F32 = jnp.float32
BF16 = jnp.bfloat16

D_MODEL = 1024
GRID_W = 64
HEADS = 8
DK = 64
DV = 128
QK_W = HEADS * DK
V_W = HEADS * DV
KV_W = QK_W + V_W
CHUNK = 128
PAIRS = HEADS // 2
POOL_WINDOWS = (2, 4, 8, 16)
POOL_GD = 128
POOL_W = 512
COL_Q = KV_W
COL_G = COL_Q + QK_W
COL_P = COL_G + V_W
COL_GA = COL_P + POOL_W
COL_GB = COL_GA + D_MODEL
IN_W = COL_GB + D_MODEL
D_FF = 2816
FF_CW = 256
FF_NC = D_FF // FF_CW
FF_PAD = 8
N_MOD = 6
LN_EPS = 1e-6
ALPHA = 2.0 ** 0.25
POOL_HALO = 8
U_HALO = 16
MOD_ROWS = 8

VMEM_LIMIT = 56 * 1024 * 1024

TB_KV = 512
TB_MIX = 512
TB_FFN = 512


def _dot(a, b):
    return jnp.dot(a, b, preferred_element_type=F32)


def _dot_nt(a, b):
    return lax.dot_general(a, b, (((1,), (1,)), ((), ())), preferred_element_type=F32)


def _dot_tn(a, b):
    return lax.dot_general(a, b, (((0,), (0,)), ((), ())), preferred_element_type=F32)


def _ln(x):
    mu = jnp.mean(x, axis=-1, keepdims=True)
    xc = x - mu
    var = jnp.mean(xc * xc, axis=-1, keepdims=True)
    return xc * lax.rsqrt(var + LN_EPS)


def _gelu_tanh(x):
    c = float(np.sqrt(2.0 / np.pi))
    half = 0.5 * x
    return half + half * jnp.tanh(x * (c + (c * 0.044715) * (x * x)))


def _const_spec(shape):
    nd = len(shape)
    return pl.BlockSpec(shape, lambda *_: (0,) * nd, pipeline_mode=pl.Buffered(1))


def _pair_diag(r):
    row = lax.broadcasted_iota(jnp.int32, (CHUNK, DV), 0)
    return jnp.where(row < DK, r[:, :DV], r[:, DV:])


def _chunk_kv(kd, v):
    outs = []
    for p in range(PAIRS):
        r = _dot_tn(kd[:, p * 128:(p + 1) * 128], v[:, p * 256:(p + 1) * 256])
        outs.append(_pair_diag(r))
    return jnp.concatenate(outs, axis=0)


def _adaln_kernel(c_ref, w_ref, b_ref, o_ref):
    c = c_ref[...]
    s = c * jax.nn.sigmoid(c)
    o_ref[...] = _dot(s, w_ref[...]) + b_ref[...]


def _adaln(cc, w_ada, b_ada):
    n = w_ada.shape[1]
    bn = 1536
    return pl.pallas_call(
        _adaln_kernel,
        grid=(n // bn,),
        in_specs=[pl.BlockSpec((MOD_ROWS, D_MODEL), lambda i: (0, 0)),
                  pl.BlockSpec((D_MODEL, bn), lambda i: (0, i)),
                  pl.BlockSpec((1, bn), lambda i: (0, i))],
        out_specs=pl.BlockSpec((MOD_ROWS, bn), lambda i: (0, i)),
        out_shape=jax.ShapeDtypeStruct((MOD_ROWS, n), F32),
        compiler_params=pltpu.CompilerParams(vmem_limit_bytes=VMEM_LIMIT),
        name="adaln",
    )(cc, w_ada, b_ada)


def _ctx_kernel(ctx_ref, sh_ref, sc_ref, wkv_ref, kdf_ref, kdb_ref, cdf_ref, cdb_ref, sf_ref, sb_ref):
    x = ctx_ref[0]
    u = (_ln(x) * (1.0 + sc_ref[...]) + sh_ref[...]).astype(BF16)
    kv = _dot(u, wkv_ref[...])
    k = kv[:, :QK_W] * (DK ** -0.5)
    v = kv[:, QK_W:].astype(BF16)
    n = x.shape[0] // CHUNK
    sf = jnp.zeros((QK_W, DV), F32)
    for c in range(n):
        kc = k[c * CHUNK:(c + 1) * CHUNK]
        vc = v[c * CHUNK:(c + 1) * CHUNK]
        sf = cdf_ref[...] * sf + _chunk_kv((kc * kdf_ref[...]).astype(BF16), vc)
    sb = jnp.zeros((QK_W, DV), F32)
    for c in reversed(range(n)):
        kc = k[c * CHUNK:(c + 1) * CHUNK]
        vc = v[c * CHUNK:(c + 1) * CHUNK]
        sb = cdb_ref[...] * sb + _chunk_kv((kc * kdb_ref[...]).astype(BF16), vc)
    sf_ref[0] = sf
    sb_ref[0] = sb


def _ctx_states(ctx, sh, sc, w_kv, kdf, kdb, cdf, cdb):
    B, Lc, _ = ctx.shape
    st = jax.ShapeDtypeStruct((B, QK_W, DV), F32)
    return pl.pallas_call(
        _ctx_kernel,
        grid=(B,),
        in_specs=[pl.BlockSpec((1, Lc, D_MODEL), lambda b: (b, 0, 0)),
                  _const_spec((1, D_MODEL)), _const_spec((1, D_MODEL)),
                  _const_spec((D_MODEL, KV_W)),
                  _const_spec((CHUNK, QK_W)), _const_spec((CHUNK, QK_W)),
                  _const_spec((QK_W, DV)), _const_spec((QK_W, DV))],
        out_specs=[pl.BlockSpec((1, QK_W, DV), lambda b: (b, 0, 0)),
                   pl.BlockSpec((1, QK_W, DV), lambda b: (b, 0, 0))],
        out_shape=[st, st],
        compiler_params=pltpu.CompilerParams(vmem_limit_bytes=VMEM_LIMIT),
        name="ctx_states",
    )(ctx, sh, sc, w_kv, kdf, kdb, cdf, cdb)


def _kv_kernel(x_ref, sh_ref, sc_ref, wkv_ref, kdb_ref, cdb_ref, sb0_ref, u_ref, kv_ref, sb_ref, s_ref):
    @pl.when(pl.program_id(1) == 0)
    def _():
        s_ref[...] = sb0_ref[0]

    x = x_ref[0]
    u = (_ln(x) * (1.0 + sc_ref[0]) + sh_ref[0]).astype(BF16)
    u_ref[0] = u
    kv = _dot(u, wkv_ref[...])
    k = kv[:, :QK_W] * (DK ** -0.5)
    v = kv[:, QK_W:].astype(BF16)
    kv_ref[0, :, :QK_W] = k.astype(BF16)
    kv_ref[0, :, QK_W:] = v
    n = x.shape[0] // CHUNK
    for c in reversed(range(n)):
        s = s_ref[...]
        sb_ref[0, c] = s.astype(BF16)
        kc = k[c * CHUNK:(c + 1) * CHUNK]
        vc = v[c * CHUNK:(c + 1) * CHUNK]
        s_ref[...] = cdb_ref[...] * s + _chunk_kv((kc * kdb_ref[...]).astype(BF16), vc)


def _kv_states(x, sh, sc, w_kv, kdb, cdb, sb0):
    B, L, _ = x.shape
    tb = TB_KV
    nb = L // tb
    nch = tb // CHUNK
    return pl.pallas_call(
        _kv_kernel,
        grid=(B, nb),
        in_specs=[pl.BlockSpec((1, tb, D_MODEL), lambda b, j: (b, nb - 1 - j, 0)),
                  pl.BlockSpec((1, 1, D_MODEL), lambda b, j: (b, 0, 0)),
                  pl.BlockSpec((1, 1, D_MODEL), lambda b, j: (b, 0, 0)),
                  _const_spec((D_MODEL, KV_W)),
                  _const_spec((CHUNK, QK_W)),
                  _const_spec((QK_W, DV)),
                  pl.BlockSpec((1, QK_W, DV), lambda b, j: (b, 0, 0))],
        out_specs=[pl.BlockSpec((1, tb, D_MODEL), lambda b, j: (b, nb - 1 - j, 0)),
                   pl.BlockSpec((1, tb, KV_W), lambda b, j: (b, nb - 1 - j, 0)),
                   pl.BlockSpec((1, nch, QK_W, DV), lambda b, j: (b, nb - 1 - j, 0, 0))],
        out_shape=[jax.ShapeDtypeStruct((B, L, D_MODEL), BF16),
                   jax.ShapeDtypeStruct((B, L, KV_W), BF16),
                   jax.ShapeDtypeStruct((B, L // CHUNK, QK_W, DV), BF16)],
        scratch_shapes=[pltpu.VMEM((QK_W, DV), F32)],
        compiler_params=pltpu.CompilerParams(
            dimension_semantics=("arbitrary", "arbitrary"), vmem_limit_bytes=VMEM_LIMIT),
        name="kv_states",
    )(x, sh, sc, w_kv, kdb, cdb, sb0)


def _pool_features(pe, j, nb, tb, seq_len, poolw_ref, pscale_ref):
    n = pe.shape[0]
    t = j * tb + lax.broadcasted_iota(jnp.int32, (tb, POOL_GD), 0)
    outs = []
    for gi, w in enumerate(POOL_WINDOWS):
        half = w // 2
        a = pe[:, gi * POOL_GD:(gi + 1) * POOL_GD]
        centre = a[POOL_HALO:POOL_HALO + tb]
        s = a
        step = 1
        while step < w:
            s = s + pltpu.roll(s, n - step, axis=0)
            step *= 2
        s = pltpu.roll(s, half, axis=0)[POOL_HALO:POOL_HALO + tb]
        cnt = (jnp.minimum(t + half, seq_len) - jnp.maximum(t - half, 0)).astype(F32)
        diff = (s / cnt - centre).astype(BF16)
        outs.append(_dot(diff, poolw_ref[gi]))
    return jnp.concatenate(outs, axis=-1) * pscale_ref[...]


def _mixer_kernel(x_ref, u_ref, up_ref, un_ref, kv_ref, sb_ref, sf0_ref, g1_ref,
                  wr_ref, dm_ref, qdf_ref, qdb_ref, kdf_ref, cdf_ref,
                  poolw_ref, pscale_ref, wbr_ref, wbp_ref, wout_ref, lng_ref, lnb_ref,
                  o_ref, s_ref, r_ref, *, nb, seq_len):
    j = pl.program_id(1)
    tb = x_ref.shape[1]

    @pl.when(j == 0)
    def _():
        s_ref[...] = sf0_ref[0]

    x = x_ref[0]
    u = u_ref[0]

    q = _dot(u, wr_ref[:, COL_Q:COL_G])
    g = _dot(u, wr_ref[:, COL_G:COL_P])

    lane = lax.broadcasted_iota(jnp.int32, (CHUNK, 128), 1)
    zv = jnp.zeros((CHUNK, DV), BF16)
    zs = jnp.zeros((DK, DV), BF16)

    def block_diag_state(s):
        left = jnp.concatenate([s[:DK], zs], axis=0)
        right = jnp.concatenate([zs, s[DK:]], axis=0)
        return jnp.concatenate([left, right], axis=1)

    for c in range(tb // CHUNK):
        rows = slice(c * CHUNK, (c + 1) * CHUNK)
        qc = q[rows]
        qb16 = qc.astype(BF16)
        qf = (qc * qdf_ref[...]).astype(BF16)
        qb = (qc * qdb_ref[...]).astype(BF16)
        kc = kv_ref[0, rows, 0:QK_W].astype(F32)
        vc = kv_ref[0, rows, QK_W:KV_W]
        kd = (kc * kdf_ref[...]).astype(BF16)
        ys = []
        for p in range(PAIRS):
            ql = slice(p * 128, (p + 1) * 128)
            kp = kc[:, ql]
            k_lo = jnp.where(lane < DK, kp, 0.0).astype(BF16)
            k_hi = jnp.where(lane >= DK, kp, 0.0).astype(BF16)
            krhs = jnp.concatenate([k_lo, k_hi], axis=0)
            sc = (_dot_nt(qb16[:, ql], krhs) * dm_ref[p]).astype(BF16)
            vp = vc[:, p * 256:(p + 1) * 256]
            vrhs = jnp.concatenate([jnp.concatenate([vp[:, :DV], zv], axis=1),
                                    jnp.concatenate([zv, vp[:, DV:]], axis=1)], axis=0)
            y = _dot(sc, vrhs)
            s_f = s_ref[ql, :]
            srhs = jnp.concatenate([block_diag_state(s_f.astype(BF16)),
                                    block_diag_state(sb_ref[0, c, ql, :])], axis=0)
            qlhs = jnp.concatenate([qf[:, ql], qb[:, ql]], axis=1)
            y = y + _dot(qlhs, srhs)
            ys.append(y)
            s_ref[ql, :] = cdf_ref[ql, :] * s_f + _pair_diag(_dot_tn(kd[:, ql], vp))
        gc = g[rows]
        sg = gc * jax.nn.sigmoid(gc)
        for p in range(PAIRS):
            for hh in range(2):
                h = 2 * p + hh
                yh = ys[p][:, hh * DV:(hh + 1) * DV]
                mu = jnp.mean(yh, axis=-1, keepdims=True)
                yc = yh - mu
                var = jnp.mean(yc * yc, axis=-1, keepdims=True)
                yn = yc * lax.rsqrt(var + LN_EPS)
                r_ref[rows, h * DV:(h + 1) * DV] = (yn * sg[:, h * DV:(h + 1) * DV]).astype(BF16)

    ret = _dot(r_ref[...], wbr_ref[...])

    ph = _dot(jnp.concatenate([up_ref[0], un_ref[0]], axis=0), wr_ref[:, COL_P:COL_GA])
    p_prev = ph[U_HALO - POOL_HALO:U_HALO] * (j > 0).astype(F32)
    p_next = ph[U_HALO:U_HALO + POOL_HALO] * (j < nb - 1).astype(F32)
    pm = _dot(u, wr_ref[:, COL_P:COL_GA])
    pe = jnp.concatenate([p_prev, pm, p_next], axis=0)
    feat = _pool_features(pe, j, nb, tb, seq_len, poolw_ref, pscale_ref).astype(BF16)
    pool = _dot(feat, wbp_ref[...])

    ga = _dot(u, wr_ref[:, COL_GA:COL_GB])
    merged = jax.nn.sigmoid(ga) * ret
    gb = _dot(u, wr_ref[:, COL_GB:IN_W])
    merged = (merged + jax.nn.sigmoid(gb) * pool).astype(BF16)
    mix = _dot(merged, wout_ref[...])
    z = ALPHA * x + g1_ref[0] * mix
    o_ref[0] = _ln(z) * lng_ref[...] + lnb_ref[...]


def _mixer(x, u, kv, sb, sf0, g1, w_rest, dm, qdf, qdb, kdf, cdf,
           pool_w, pool_scale, w_br, w_bp, w_out, ln_g, ln_b):
    B, L, _ = x.shape
    tb = TB_MIX
    nb = L // tb
    nch = tb // CHUNK
    hb = tb // U_HALO
    nh = L // U_HALO
    mod_spec = pl.BlockSpec((1, 1, D_MODEL), lambda b, j: (b, 0, 0))
    return pl.pallas_call(
        functools.partial(_mixer_kernel, nb=nb, seq_len=L),
        grid=(B, nb),
        in_specs=[pl.BlockSpec((1, tb, D_MODEL), lambda b, j: (b, j, 0)),
                  pl.BlockSpec((1, tb, D_MODEL), lambda b, j: (b, j, 0)),
                  pl.BlockSpec((1, U_HALO, D_MODEL), lambda b, j: (b, jnp.maximum(j * hb - 1, 0), 0)),
                  pl.BlockSpec((1, U_HALO, D_MODEL),
                               lambda b, j: (b, jnp.minimum((j + 1) * hb, nh - 1), 0)),
                  pl.BlockSpec((1, tb, KV_W), lambda b, j: (b, j, 0)),
                  pl.BlockSpec((1, nch, QK_W, DV), lambda b, j: (b, j, 0, 0)),
                  pl.BlockSpec((1, QK_W, DV), lambda b, j: (b, 0, 0)),
                  mod_spec,
                  _const_spec((D_MODEL, IN_W)),
                  _const_spec((PAIRS, CHUNK, 256)),
                  _const_spec((CHUNK, QK_W)), _const_spec((CHUNK, QK_W)), _const_spec((CHUNK, QK_W)),
                  _const_spec((QK_W, DV)),
                  _const_spec((len(POOL_WINDOWS), POOL_GD, POOL_GD)),
                  _const_spec((1, POOL_W)),
                  _const_spec((V_W, D_MODEL)),
                  _const_spec((POOL_W, D_MODEL)),
                  _const_spec((D_MODEL, D_MODEL)),
                  _const_spec((1, D_MODEL)), _const_spec((1, D_MODEL))],
        out_specs=pl.BlockSpec((1, tb, D_MODEL), lambda b, j: (b, j, 0)),
        out_shape=jax.ShapeDtypeStruct((B, L, D_MODEL), F32),
        scratch_shapes=[pltpu.VMEM((QK_W, DV), F32), pltpu.VMEM((tb, V_W), BF16)],
        compiler_params=pltpu.CompilerParams(
            dimension_semantics=("arbitrary", "arbitrary"), vmem_limit_bytes=VMEM_LIMIT),
        name="mixer",
    )(x, u, u, u, kv, sb, sf0, g1, w_rest, dm, qdf, qdb, kdf, cdf,
      pool_w, pool_scale, w_br, w_bp, w_out, ln_g, ln_b)


def _ffn_kernel(x_ref, xp_ref, xn_ref, sh_ref, sc_ref, g2_ref, wup_ref, cw_ref, wdn_ref,
                lng_ref, lnb_ref, o_ref, u_ref, acc_ref, ha_ref, hb_ref, *, nb):
    j = pl.program_id(1)
    tb = x_ref.shape[1]
    n = tb + 2 * GRID_W
    scale = 1.0 + sc_ref[0]
    shift = sh_ref[0]
    x = x_ref[0]

    def mod(v):
        return _ln(v) * scale + shift

    u_ref[0:GRID_W] = (mod(xp_ref[0]) * (j > 0).astype(F32)).astype(BF16)
    u_ref[GRID_W:GRID_W + tb] = mod(x).astype(BF16)
    u_ref[GRID_W + tb:n] = (mod(xn_ref[0]) * (j < nb - 1).astype(F32)).astype(BF16)

    col = lax.broadcasted_iota(jnp.int32, (tb, FF_CW), 0) & (GRID_W - 1)
    has_left = col > 0
    has_right = col < GRID_W - 1

    pad = jnp.zeros((FF_PAD, FF_CW), F32)
    for h_ref in (ha_ref, hb_ref):
        for ab in range(2):
            h_ref[ab, 0:FF_PAD] = pad
            h_ref[ab, FF_PAD + n:FF_PAD + n + FF_PAD] = pad

    def lanes(c, half):
        return pl.ds(pl.multiple_of(half * D_FF + c * FF_CW, FF_CW), FF_CW)

    def up(c, h_ref):
        u = u_ref[...]
        h_ref[0, FF_PAD:FF_PAD + n] = _dot(u, wup_ref[:, lanes(c, 0)])
        h_ref[1, FF_PAD:FF_PAD + n] = _dot(u, wup_ref[:, lanes(c, 1)])

    def conv(h_ref, ab, cw):
        cols = []
        for dc in range(3):
            hs = h_ref[ab, FF_PAD + dc - 1:FF_PAD + dc - 1 + n].astype(BF16)
            g = None
            for dr in range(3):
                term = cw[3 * dr + dc:3 * dr + dc + 1] * hs[dr * GRID_W:dr * GRID_W + tb]
                g = term if g is None else g + term
            cols.append(g)
        zero = jnp.zeros_like(cols[1])
        return (cols[1] + cw[9:10]) + (jnp.where(has_left, cols[0], zero) + jnp.where(has_right, cols[2], zero))

    def down(c, h_ref):
        a = conv(h_ref, 0, cw_ref[:, lanes(c, 0)])
        b = conv(h_ref, 1, cw_ref[:, lanes(c, 1)])
        acc_ref[...] += _dot(_gelu_tanh(a.astype(F32)).astype(BF16) * b, wdn_ref[c])

    acc_ref[...] = jnp.zeros_like(acc_ref)
    up(0, ha_ref)

    def body(i, carry):
        c = 2 * i
        up(c + 1, hb_ref)
        down(c, ha_ref)
        up(c + 2, ha_ref)
        down(c + 1, hb_ref)
        return carry

    lax.fori_loop(0, FF_NC // 2, body, 0)
    down(FF_NC - 1, ha_ref)
    z = ALPHA * x + g2_ref[0] * acc_ref[...]
    o_ref[0] = _ln(z) * lng_ref[...] + lnb_ref[...]


def _ffn(x, sh, sc, g2, w_up, conv_wb, w_down, ln_g, ln_b):
    B, L, _ = x.shape
    tb = TB_FFN
    nb = L // tb
    hb = tb // GRID_W
    nh = L // GRID_W
    mod_spec = pl.BlockSpec((1, 1, D_MODEL), lambda b, j: (b, 0, 0))
    return pl.pallas_call(
        functools.partial(_ffn_kernel, nb=nb),
        grid=(B, nb),
        in_specs=[pl.BlockSpec((1, tb, D_MODEL), lambda b, j: (b, j, 0)),
                  pl.BlockSpec((1, GRID_W, D_MODEL), lambda b, j: (b, jnp.maximum(j * hb - 1, 0), 0)),
                  pl.BlockSpec((1, GRID_W, D_MODEL),
                               lambda b, j: (b, jnp.minimum((j + 1) * hb, nh - 1), 0)),
                  mod_spec, mod_spec, mod_spec,
                  _const_spec((D_MODEL, 2 * D_FF)),
                  _const_spec((16, 2 * D_FF)),
                  _const_spec((FF_NC, FF_CW, D_MODEL)),
                  _const_spec((1, D_MODEL)), _const_spec((1, D_MODEL))],
        out_specs=pl.BlockSpec((1, tb, D_MODEL), lambda b, j: (b, j, 0)),
        out_shape=jax.ShapeDtypeStruct((B, L, D_MODEL), F32),
        scratch_shapes=[pltpu.VMEM((tb + 2 * GRID_W, D_MODEL), BF16),
                        pltpu.VMEM((tb, D_MODEL), F32),
                        pltpu.VMEM((2, tb + 2 * GRID_W + 2 * FF_PAD, FF_CW), F32),
                        pltpu.VMEM((2, tb + 2 * GRID_W + 2 * FF_PAD, FF_CW), F32)],
        compiler_params=pltpu.CompilerParams(
            dimension_semantics=("arbitrary", "arbitrary"), vmem_limit_bytes=VMEM_LIMIT),
        name="conv_ffn",
    )(x, x, x, sh, sc, g2, w_up, conv_wb, w_down, ln_g, ln_b)


def _decay_tables(ret_decay_logit):
    lg = jax.nn.log_sigmoid(ret_decay_logit.astype(F32))
    pos = jnp.arange(CHUNK, dtype=F32)
    diff = pos[:, None] - pos[None, :]
    d_f = jnp.where(diff[None] >= 0, jnp.exp(jnp.maximum(diff, 0.0)[None] * lg[0][:, None, None]), 0.0)
    d_b = jnp.where(diff[None] <= 0, jnp.exp(jnp.maximum(-diff, 0.0)[None] * lg[1][:, None, None]), 0.0)
    dm = (d_f + d_b).reshape(PAIRS, 2, CHUNK, CHUNK).transpose(0, 2, 1, 3).reshape(PAIRS, CHUNK, 2 * CHUNK)

    def lanes(t):
        return jnp.repeat(t, DK, axis=1)

    qdf = lanes(jnp.exp((pos + 1.0)[:, None] * lg[0][None, :]))
    qdb = lanes(jnp.exp((CHUNK - pos)[:, None] * lg[1][None, :]))
    kdf = lanes(jnp.exp((CHUNK - 1.0 - pos)[:, None] * lg[0][None, :]))
    kdb = lanes(jnp.exp(pos[:, None] * lg[1][None, :]))

    def rows(t):
        return jnp.broadcast_to(jnp.repeat(t, DK)[:, None], (QK_W, DV))

    cdf = rows(jnp.exp(CHUNK * lg[0]))
    cdb = rows(jnp.exp(CHUNK * lg[1]))
    return dm, qdf, qdb, kdf, kdb, cdf, cdb


def kernel(x, c, ctx, c_ctx, w_ada, b_ada, w_in, ret_decay_logit, pool_w, pool_scale, w_branch_ret,
           w_branch_pool, w_out, ln1_g, ln1_b, w_up, conv_w, conv_b, w_down, ln2_g, ln2_b):
    B = x.shape[0]
    D = D_MODEL
    assert w_ada.shape[0] == 1, "single-layer stack"

    cc = jnp.zeros((MOD_ROWS, D), F32).at[:B].set(c).at[B].set(c_ctx)
    mod = _adaln(cc, w_ada[0], b_ada[0][None, :])
    lat = mod[:B].reshape(B, N_MOD, 1, D)
    sh1, sc1, g1, sh2, sc2, g2 = (lat[:, i] for i in range(N_MOD))
    sh1c = mod[B:B + 1, 0:D]
    sc1c = mod[B:B + 1, D:2 * D]

    dm, qdf, qdb, kdf, kdb, cdf, cdb = _decay_tables(ret_decay_logit[0])

    w_in_b = w_in[0].astype(BF16)

    s_f, s_b = _ctx_states(ctx, sh1c, sc1c, w_in_b, kdf, kdb, cdf, cdb)
    u, kv, sb = _kv_states(x, sh1, sc1, w_in_b, kdb, cdb, s_b)
    x1 = _mixer(x, u, kv, sb, s_f, g1, w_in_b, dm, qdf, qdb, kdf, cdf,
                pool_w[0].astype(BF16), pool_scale[0][None, :],
                w_branch_ret[0].astype(BF16), w_branch_pool[0].astype(BF16), w_out[0].astype(BF16),
                ln1_g[0][None, :], ln1_b[0][None, :])

    conv_wb = jnp.concatenate([conv_w[0].reshape(9, 2 * D_FF), conv_b[0][None, :],
                               jnp.zeros((6, 2 * D_FF), F32)], axis=0).astype(BF16)
    w_down_c = w_down[0].astype(BF16).reshape(FF_NC, FF_CW, D)
    return _ffn(x1, sh2, sc2, g2, w_up[0].astype(BF16), conv_wb, w_down_c,
                ln2_g[0][None, :], ln2_b[0][None, :])
```

```python
import functools

import jax
import jax.numpy as jnp
import numpy as np
from jax import lax
from jax.experimental import pallas as pl
from jax.experimental.pallas import tpu as pltpu

F32 = jnp.float32
BF16 = jnp.bfloat16

D_MODEL = 1024
GRID_W = 64
HEADS = 8
DK = 64
DV = 128
QK_W = HEADS * DK
V_W = HEADS * DV
KV_W = QK_W + V_W
CHUNK = 128
PAIRS = HEADS // 2
POOL_WINDOWS = (2, 4, 8, 16)
POOL_GD = 128
POOL_W = 512
COL_Q = KV_W
COL_G = COL_Q + QK_W
COL_P = COL_G + V_W
COL_GA = COL_P + POOL_W
COL_GB = COL_GA + D_MODEL
IN_W = COL_GB + D_MODEL
D_FF = 2816
FF_CW = 256
FF_NC = D_FF // FF_CW
FF_PAD = 8
N_MOD = 6
LN_EPS = 1e-6
ALPHA = 2.0 ** 0.25
POOL_HALO = 8
U_HALO = 16
MOD_ROWS = 8

VMEM_LIMIT = 56 * 1024 * 1024

TB_KV = 512
TB_MIX = 512
TB_FFN = 512


def _dot(a, b):
    return jnp.dot(a, b, preferred_element_type=F32)


def _dot_nt(a, b):
    return lax.dot_general(a, b, (((1,), (1,)), ((), ())), preferred_element_type=F32)


def _dot_tn(a, b):
    return lax.dot_general(a, b, (((0,), (0,)), ((), ())), preferred_element_type=F32)


def _ln(x):
    mu = jnp.mean(x, axis=-1, keepdims=True)
    xc = x - mu
    var = jnp.mean(xc * xc, axis=-1, keepdims=True)
    return xc * lax.rsqrt(var + LN_EPS)


def _gelu_tanh(x):
    c = float(np.sqrt(2.0 / np.pi))
    half = 0.5 * x
    return half + half * jnp.tanh(x * (c + (c * 0.044715) * (x * x)))


def _const_spec(shape):
    nd = len(shape)
    return pl.BlockSpec(shape, lambda *_: (0,) * nd, pipeline_mode=pl.Buffered(1))


def _pair_diag(r):
    row = lax.broadcasted_iota(jnp.int32, (CHUNK, DV), 0)
    return jnp.where(row < DK, r[:, :DV], r[:, DV:])


def _chunk_kv(kd, v):
    outs = []
    for p in range(PAIRS):
        r = _dot_tn(kd[:, p * 128:(p + 1) * 128], v[:, p * 256:(p + 1) * 256])
        outs.append(_pair_diag(r))
    return jnp.concatenate(outs, axis=0)


def _adaln_kernel(c_ref, w_ref, b_ref, o_ref):
    c = c_ref[...]
    s = c * jax.nn.sigmoid(c)
    o_ref[...] = _dot(s, w_ref[...]) + b_ref[...]


def _adaln(cc, w_ada, b_ada):
    n = w_ada.shape[1]
    bn = 1536
    return pl.pallas_call(
        _adaln_kernel,
        grid=(n // bn,),
        in_specs=[pl.BlockSpec((MOD_ROWS, D_MODEL), lambda i: (0, 0)),
                  pl.BlockSpec((D_MODEL, bn), lambda i: (0, i)),
                  pl.BlockSpec((1, bn), lambda i: (0, i))],
        out_specs=pl.BlockSpec((MOD_ROWS, bn), lambda i: (0, i)),
        out_shape=jax.ShapeDtypeStruct((MOD_ROWS, n), F32),
        compiler_params=pltpu.CompilerParams(vmem_limit_bytes=VMEM_LIMIT),
        name="adaln",
    )(cc, w_ada, b_ada)


def _ctx_kernel(ctx_ref, sh_ref, sc_ref, wkv_ref, kdf_ref, kdb_ref, cdf_ref, cdb_ref, sf_ref, sb_ref):
    x = ctx_ref[0]
    u = (_ln(x) * (1.0 + sc_ref[...]) + sh_ref[...]).astype(BF16)
    kv = _dot(u, wkv_ref[...])
    k = kv[:, :QK_W] * (DK ** -0.5)
    v = kv[:, QK_W:].astype(BF16)
    n = x.shape[0] // CHUNK
    sf = jnp.zeros((QK_W, DV), F32)
    for c in range(n):
        kc = k[c * CHUNK:(c + 1) * CHUNK]
        vc = v[c * CHUNK:(c + 1) * CHUNK]
        sf = cdf_ref[...] * sf + _chunk_kv((kc * kdf_ref[...]).astype(BF16), vc)
    sb = jnp.zeros((QK_W, DV), F32)
    for c in reversed(range(n)):
        kc = k[c * CHUNK:(c + 1) * CHUNK]
        vc = v[c * CHUNK:(c + 1) * CHUNK]
        sb = cdb_ref[...] * sb + _chunk_kv((kc * kdb_ref[...]).astype(BF16), vc)
    sf_ref[0] = sf
    sb_ref[0] = sb


def _ctx_states(ctx, sh, sc, w_kv, kdf, kdb, cdf, cdb):
    B, Lc, _ = ctx.shape
    st = jax.ShapeDtypeStruct((B, QK_W, DV), F32)
    return pl.pallas_call(
        _ctx_kernel,
        grid=(B,),
        in_specs=[pl.BlockSpec((1, Lc, D_MODEL), lambda b: (b, 0, 0)),
                  _const_spec((1, D_MODEL)), _const_spec((1, D_MODEL)),
                  _const_spec((D_MODEL, KV_W)),
                  _const_spec((CHUNK, QK_W)), _const_spec((CHUNK, QK_W)),
                  _const_spec((QK_W, DV)), _const_spec((QK_W, DV))],
        out_specs=[pl.BlockSpec((1, QK_W, DV), lambda b: (b, 0, 0)),
                   pl.BlockSpec((1, QK_W, DV), lambda b: (b, 0, 0))],
        out_shape=[st, st],
        compiler_params=pltpu.CompilerParams(vmem_limit_bytes=VMEM_LIMIT),
        name="ctx_states",
    )(ctx, sh, sc, w_kv, kdf, kdb, cdf, cdb)


def _kv_kernel(x_ref, sh_ref, sc_ref, wkv_ref, kdb_ref, cdb_ref, sb0_ref, u_ref, kv_ref, sb_ref, s_ref):
    @pl.when(pl.program_id(1) == 0)
    def _():
        s_ref[...] = sb0_ref[0]

    x = x_ref[0]
    u = (_ln(x) * (1.0 + sc_ref[0]) + sh_ref[0]).astype(BF16)
    u_ref[0] = u
    kv = _dot(u, wkv_ref[...])
    k = kv[:, :QK_W] * (DK ** -0.5)
    v = kv[:, QK_W:].astype(BF16)
    kv_ref[0, :, :QK_W] = k.astype(BF16)
    kv_ref[0, :, QK_W:] = v
    n = x.shape[0] // CHUNK
    for c in reversed(range(n)):
        s = s_ref[...]
        sb_ref[0, c] = s.astype(BF16)
        kc = k[c * CHUNK:(c + 1) * CHUNK]
        vc = v[c * CHUNK:(c + 1) * CHUNK]
        s_ref[...] = cdb_ref[...] * s + _chunk_kv((kc * kdb_ref[...]).astype(BF16), vc)


def _kv_states(x, sh, sc, w_kv, kdb, cdb, sb0):
    B, L, _ = x.shape
    tb = TB_KV
    nb = L // tb
    nch = tb // CHUNK
    return pl.pallas_call(
        _kv_kernel,
        grid=(B, nb),
        in_specs=[pl.BlockSpec((1, tb, D_MODEL), lambda b, j: (b, nb - 1 - j, 0)),
                  pl.BlockSpec((1, 1, D_MODEL), lambda b, j: (b, 0, 0)),
                  pl.BlockSpec((1, 1, D_MODEL), lambda b, j: (b, 0, 0)),
                  _const_spec((D_MODEL, KV_W)),
                  _const_spec((CHUNK, QK_W)),
                  _const_spec((QK_W, DV)),
                  pl.BlockSpec((1, QK_W, DV), lambda b, j: (b, 0, 0))],
        out_specs=[pl.BlockSpec((1, tb, D_MODEL), lambda b, j: (b, nb - 1 - j, 0)),
                   pl.BlockSpec((1, tb, KV_W), lambda b, j: (b, nb - 1 - j, 0)),
                   pl.BlockSpec((1, nch, QK_W, DV), lambda b, j: (b, nb - 1 - j, 0, 0))],
        out_shape=[jax.ShapeDtypeStruct((B, L, D_MODEL), BF16),
                   jax.ShapeDtypeStruct((B, L, KV_W), BF16),
                   jax.ShapeDtypeStruct((B, L // CHUNK, QK_W, DV), BF16)],
        scratch_shapes=[pltpu.VMEM((QK_W, DV), F32)],
        compiler_params=pltpu.CompilerParams(
            dimension_semantics=("arbitrary", "arbitrary"), vmem_limit_bytes=VMEM_LIMIT),
        name="kv_states",
    )(x, sh, sc, w_kv, kdb, cdb, sb0)


def _pool_features(pe, j, nb, tb, seq_len, poolw_ref, pscale_ref):
    n = pe.shape[0]
    t = j * tb + lax.broadcasted_iota(jnp.int32, (tb, POOL_GD), 0)
    outs = []
    for gi, w in enumerate(POOL_WINDOWS):
        half = w // 2
        a = pe[:, gi * POOL_GD:(gi + 1) * POOL_GD]
        centre = a[POOL_HALO:POOL_HALO + tb]
        s = a
        step = 1
        while step < w:
            s = s + pltpu.roll(s, n - step, axis=0)
            step *= 2
        s = pltpu.roll(s, half, axis=0)[POOL_HALO:POOL_HALO + tb]
        cnt = (jnp.minimum(t + half, seq_len) - jnp.maximum(t - half, 0)).astype(F32)
        diff = (s / cnt - centre).astype(BF16)
        outs.append(_dot(diff, poolw_ref[gi]))
    return jnp.concatenate(outs, axis=-1) * pscale_ref[...]


def _mixer_kernel(x_ref, u_ref, up_ref, un_ref, kv_ref, sb_ref, sf0_ref, g1_ref,
                  wr_ref, dm_ref, qdf_ref, qdb_ref, kdf_ref, cdf_ref,
                  poolw_ref, pscale_ref, wbr_ref, wbp_ref, wout_ref, lng_ref, lnb_ref,
                  o_ref, s_ref, r_ref, *, nb, seq_len):
    j = pl.program_id(1)
    tb = x_ref.shape[1]

    @pl.when(j == 0)
    def _():
        s_ref[...] = sf0_ref[0]

    x = x_ref[0]
    u = u_ref[0]

    q = _dot(u, wr_ref[:, COL_Q:COL_G])
    g = _dot(u, wr_ref[:, COL_G:COL_P])

    lane = lax.broadcasted_iota(jnp.int32, (CHUNK, 128), 1)
    zv = jnp.zeros((CHUNK, DV), BF16)
    zs = jnp.zeros((DK, DV), BF16)

    def block_diag_state(s):
        left = jnp.concatenate([s[:DK], zs], axis=0)
        right = jnp.concatenate([zs, s[DK:]], axis=0)
        return jnp.concatenate([left, right], axis=1)

    for c in range(tb // CHUNK):
        rows = slice(c * CHUNK, (c + 1) * CHUNK)
        qc = q[rows]
        qb16 = qc.astype(BF16)
        qf = (qc * qdf_ref[...]).astype(BF16)
        qb = (qc * qdb_ref[...]).astype(BF16)
        kc = kv_ref[0, rows, 0:QK_W].astype(F32)
        vc = kv_ref[0, rows, QK_W:KV_W]
        kd = (kc * kdf_ref[...]).astype(BF16)
        ys = []
        for p in range(PAIRS):
            ql = slice(p * 128, (p + 1) * 128)
            kp = kc[:, ql]
            k_lo = jnp.where(lane < DK, kp, 0.0).astype(BF16)
            k_hi = jnp.where(lane >= DK, kp, 0.0).astype(BF16)
            krhs = jnp.concatenate([k_lo, k_hi], axis=0)
            sc = (_dot_nt(qb16[:, ql], krhs) * dm_ref[p]).astype(BF16)
            vp = vc[:, p * 256:(p + 1) * 256]
            vrhs = jnp.concatenate([jnp.concatenate([vp[:, :DV], zv], axis=1),
                                    jnp.concatenate([zv, vp[:, DV:]], axis=1)], axis=0)
            y = _dot(sc, vrhs)
            s_f = s_ref[ql, :]
            srhs = jnp.concatenate([block_diag_state(s_f.astype(BF16)),
                                    block_diag_state(sb_ref[0, c, ql, :])], axis=0)
            qlhs = jnp.concatenate([qf[:, ql], qb[:, ql]], axis=1)
            y = y + _dot(qlhs, srhs)
            ys.append(y)
            s_ref[ql, :] = cdf_ref[ql, :] * s_f + _pair_diag(_dot_tn(kd[:, ql], vp))
        gc = g[rows]
        sg = gc * jax.nn.sigmoid(gc)
        for p in range(PAIRS):
            for hh in range(2):
                h = 2 * p + hh
                yh = ys[p][:, hh * DV:(hh + 1) * DV]
                mu = jnp.mean(yh, axis=-1, keepdims=True)
                yc = yh - mu
                var = jnp.mean(yc * yc, axis=-1, keepdims=True)
                yn = yc * lax.rsqrt(var + LN_EPS)
                r_ref[rows, h * DV:(h + 1) * DV] = (yn * sg[:, h * DV:(h + 1) * DV]).astype(BF16)

    ret = _dot(r_ref[...], wbr_ref[...])

    ph = _dot(jnp.concatenate([up_ref[0], un_ref[0]], axis=0), wr_ref[:, COL_P:COL_GA])
    p_prev = ph[U_HALO - POOL_HALO:U_HALO] * (j > 0).astype(F32)
    p_next = ph[U_HALO:U_HALO + POOL_HALO] * (j < nb - 1).astype(F32)
    pm = _dot(u, wr_ref[:, COL_P:COL_GA])
    pe = jnp.concatenate([p_prev, pm, p_next], axis=0)
    feat = _pool_features(pe, j, nb, tb, seq_len, poolw_ref, pscale_ref).astype(BF16)
    pool = _dot(feat, wbp_ref[...])

    ga = _dot(u, wr_ref[:, COL_GA:COL_GB])
    merged = jax.nn.sigmoid(ga) * ret
    gb = _dot(u, wr_ref[:, COL_GB:IN_W])
    merged = (merged + jax.nn.sigmoid(gb) * pool).astype(BF16)
    mix = _dot(merged, wout_ref[...])
    z = ALPHA * x + g1_ref[0] * mix
    o_ref[0] = _ln(z) * lng_ref[...] + lnb_ref[...]


def _mixer(x, u, kv, sb, sf0, g1, w_rest, dm, qdf, qdb, kdf, cdf,
           pool_w, pool_scale, w_br, w_bp, w_out, ln_g, ln_b):
    B, L, _ = x.shape
    tb = TB_MIX
    nb = L // tb
    nch = tb // CHUNK
    hb = tb // U_HALO
    nh = L // U_HALO
    mod_spec = pl.BlockSpec((1, 1, D_MODEL), lambda b, j: (b, 0, 0))
    return pl.pallas_call(
        functools.partial(_mixer_kernel, nb=nb, seq_len=L),
        grid=(B, nb),
        in_specs=[pl.BlockSpec((1, tb, D_MODEL), lambda b, j: (b, j, 0)),
                  pl.BlockSpec((1, tb, D_MODEL), lambda b, j: (b, j, 0)),
                  pl.BlockSpec((1, U_HALO, D_MODEL), lambda b, j: (b, jnp.maximum(j * hb - 1, 0), 0)),
                  pl.BlockSpec((1, U_HALO, D_MODEL),
                               lambda b, j: (b, jnp.minimum((j + 1) * hb, nh - 1), 0)),
                  pl.BlockSpec((1, tb, KV_W), lambda b, j: (b, j, 0)),
                  pl.BlockSpec((1, nch, QK_W, DV), lambda b, j: (b, j, 0, 0)),
                  pl.BlockSpec((1, QK_W, DV), lambda b, j: (b, 0, 0)),
                  mod_spec,
                  _const_spec((D_MODEL, IN_W)),
                  _const_spec((PAIRS, CHUNK, 256)),
                  _const_spec((CHUNK, QK_W)), _const_spec((CHUNK, QK_W)), _const_spec((CHUNK, QK_W)),
                  _const_spec((QK_W, DV)),
                  _const_spec((len(POOL_WINDOWS), POOL_GD, POOL_GD)),
                  _const_spec((1, POOL_W)),
                  _const_spec((V_W, D_MODEL)),
                  _const_spec((POOL_W, D_MODEL)),
                  _const_spec((D_MODEL, D_MODEL)),
                  _const_spec((1, D_MODEL)), _const_spec((1, D_MODEL))],
        out_specs=pl.BlockSpec((1, tb, D_MODEL), lambda b, j: (b, j, 0)),
        out_shape=jax.ShapeDtypeStruct((B, L, D_MODEL), F32),
        scratch_shapes=[pltpu.VMEM((QK_W, DV), F32), pltpu.VMEM((tb, V_W), BF16)],
        compiler_params=pltpu.CompilerParams(
            dimension_semantics=("arbitrary", "arbitrary"), vmem_limit_bytes=VMEM_LIMIT),
        name="mixer",
    )(x, u, u, u, kv, sb, sf0, g1, w_rest, dm, qdf, qdb, kdf, cdf,
      pool_w, pool_scale, w_br, w_bp, w_out, ln_g, ln_b)


def _ffn_kernel(x_ref, xn_ref, sh_ref, sc_ref, g2_ref, wup_ref, cw_ref, wdn_ref,
                lng_ref, lnb_ref, o_ref, u_ref, acc_ref, ha_ref, hb_ref, top_ref, *, nb):
    j = pl.program_id(1)
    tb = x_ref.shape[1]
    m = tb + GRID_W
    n = tb + 2 * GRID_W
    scale = 1.0 + sc_ref[0]
    shift = sh_ref[0]
    x = x_ref[0]

    def mod(v):
        return _ln(v) * scale + shift

    @pl.when(j == 0)
    def _():
        top_ref[...] = jnp.zeros_like(top_ref)

    u_ref[0:tb] = mod(x).astype(BF16)
    u_ref[tb:m] = (mod(xn_ref[0]) * (j < nb - 1).astype(F32)).astype(BF16)

    col = lax.broadcasted_iota(jnp.int32, (tb, FF_CW), 0) & (GRID_W - 1)
    has_left = col > 0
    has_right = col < GRID_W - 1

    pad = jnp.zeros((FF_PAD, FF_CW), F32)
    for h_ref in (ha_ref, hb_ref):
        for ab in range(2):
            h_ref[ab, 0:FF_PAD] = pad
            h_ref[ab, FF_PAD + n:FF_PAD + n + FF_PAD] = pad

    def lanes(c, half):
        return pl.ds(pl.multiple_of(half * D_FF + c * FF_CW, FF_CW), FF_CW)

    def up(c, h_ref):
        u = u_ref[...]
        for half in range(2):
            h = _dot(u, wup_ref[:, lanes(c, half)])
            h_ref[half, FF_PAD:FF_PAD + GRID_W] = top_ref[half, c]
            h_ref[half, FF_PAD + GRID_W:FF_PAD + n] = h
            top_ref[half, c] = h[tb - GRID_W:tb]

    def conv(h_ref, ab, cw):
        cols = []
        for dc in range(3):
            hs = h_ref[ab, FF_PAD + dc - 1:FF_PAD + dc - 1 + n].astype(BF16)
            g = None
            for dr in range(3):
                term = cw[3 * dr + dc:3 * dr + dc + 1] * hs[dr * GRID_W:dr * GRID_W + tb]
                g = term if g is None else g + term
            cols.append(g)
        zero = jnp.zeros_like(cols[1])
        return (cols[1] + cw[9:10]) + (jnp.where(has_left, cols[0], zero) + jnp.where(has_right, cols[2], zero))

    def down(c, h_ref):
        a = conv(h_ref, 0, cw_ref[:, lanes(c, 0)])
        b = conv(h_ref, 1, cw_ref[:, lanes(c, 1)])
        acc_ref[...] += _dot(_gelu_tanh(a.astype(F32)).astype(BF16) * b, wdn_ref[c])

    acc_ref[...] = jnp.zeros_like(acc_ref)
    up(0, ha_ref)

    def body(i, carry):
        c = 2 * i
        up(c + 1, hb_ref)
        down(c, ha_ref)
        up(c + 2, ha_ref)
        down(c + 1, hb_ref)
        return carry

    lax.fori_loop(0, FF_NC // 2, body, 0)
    down(FF_NC - 1, ha_ref)
    z = ALPHA * x + g2_ref[0] * acc_ref[...]
    o_ref[0] = _ln(z) * lng_ref[...] + lnb_ref[...]


def _ffn(x, sh, sc, g2, w_up, conv_wb, w_down, ln_g, ln_b):
    B, L, _ = x.shape
    tb = TB_FFN
    nb = L // tb
    hb = tb // GRID_W
    nh = L // GRID_W
    mod_spec = pl.BlockSpec((1, 1, D_MODEL), lambda b, j: (b, 0, 0))
    return pl.pallas_call(
        functools.partial(_ffn_kernel, nb=nb),
        grid=(B, nb),
        in_specs=[pl.BlockSpec((1, tb, D_MODEL), lambda b, j: (b, j, 0)),
                  pl.BlockSpec((1, GRID_W, D_MODEL),
                               lambda b, j: (b, jnp.minimum((j + 1) * hb, nh - 1), 0)),
                  mod_spec, mod_spec, mod_spec,
                  _const_spec((D_MODEL, 2 * D_FF)),
                  _const_spec((16, 2 * D_FF)),
                  _const_spec((FF_NC, FF_CW, D_MODEL)),
                  _const_spec((1, D_MODEL)), _const_spec((1, D_MODEL))],
        out_specs=pl.BlockSpec((1, tb, D_MODEL), lambda b, j: (b, j, 0)),
        out_shape=jax.ShapeDtypeStruct((B, L, D_MODEL), F32),
        scratch_shapes=[pltpu.VMEM((tb + GRID_W, D_MODEL), BF16),
                        pltpu.VMEM((tb, D_MODEL), F32),
                        pltpu.VMEM((2, tb + 2 * GRID_W + 2 * FF_PAD, FF_CW), F32),
                        pltpu.VMEM((2, tb + 2 * GRID_W + 2 * FF_PAD, FF_CW), F32),
                        pltpu.VMEM((2, FF_NC, GRID_W, FF_CW), F32)],
        compiler_params=pltpu.CompilerParams(
            dimension_semantics=("arbitrary", "arbitrary"), vmem_limit_bytes=VMEM_LIMIT),
        name="conv_ffn",
    )(x, x, sh, sc, g2, w_up, conv_wb, w_down, ln_g, ln_b)


def _decay_tables(ret_decay_logit):
    lg = jax.nn.log_sigmoid(ret_decay_logit.astype(F32))
    pos = jnp.arange(CHUNK, dtype=F32)
    diff = pos[:, None] - pos[None, :]
    d_f = jnp.where(diff[None] >= 0, jnp.exp(jnp.maximum(diff, 0.0)[None] * lg[0][:, None, None]), 0.0)
    d_b = jnp.where(diff[None] <= 0, jnp.exp(jnp.maximum(-diff, 0.0)[None] * lg[1][:, None, None]), 0.0)
    dm = (d_f + d_b).reshape(PAIRS, 2, CHUNK, CHUNK).transpose(0, 2, 1, 3).reshape(PAIRS, CHUNK, 2 * CHUNK)

    def lanes(t):
        return jnp.repeat(t, DK, axis=1)

    qdf = lanes(jnp.exp((pos + 1.0)[:, None] * lg[0][None, :]))
    qdb = lanes(jnp.exp((CHUNK - pos)[:, None] * lg[1][None, :]))
    kdf = lanes(jnp.exp((CHUNK - 1.0 - pos)[:, None] * lg[0][None, :]))
    kdb = lanes(jnp.exp(pos[:, None] * lg[1][None, :]))

    def rows(t):
        return jnp.broadcast_to(jnp.repeat(t, DK)[:, None], (QK_W, DV))

    cdf = rows(jnp.exp(CHUNK * lg[0]))
    cdb = rows(jnp.exp(CHUNK * lg[1]))
    return dm, qdf, qdb, kdf, kdb, cdf, cdb


def kernel(x, c, ctx, c_ctx, w_ada, b_ada, w_in, ret_decay_logit, pool_w, pool_scale, w_branch_ret,
           w_branch_pool, w_out, ln1_g, ln1_b, w_up, conv_w, conv_b, w_down, ln2_g, ln2_b):
    B = x.shape[0]
    D = D_MODEL
    assert w_ada.shape[0] == 1, "single-layer stack"

    cc = jnp.zeros((MOD_ROWS, D), F32).at[:B].set(c).at[B].set(c_ctx)
    mod = _adaln(cc, w_ada[0], b_ada[0][None, :])
    lat = mod[:B].reshape(B, N_MOD, 1, D)
    sh1, sc1, g1, sh2, sc2, g2 = (lat[:, i] for i in range(N_MOD))
    sh1c = mod[B:B + 1, 0:D]
    sc1c = mod[B:B + 1, D:2 * D]

    dm, qdf, qdb, kdf, kdb, cdf, cdb = _decay_tables(ret_decay_logit[0])

    w_in_b = w_in[0].astype(BF16)

    s_f, s_b = _ctx_states(ctx, sh1c, sc1c, w_in_b, kdf, kdb, cdf, cdb)
    u, kv, sb = _kv_states(x, sh1, sc1, w_in_b, kdb, cdb, s_b)
    x1 = _mixer(x, u, kv, sb, s_f, g1, w_in_b, dm, qdf, qdb, kdf, cdf,
                pool_w[0].astype(BF16), pool_scale[0][None, :],
                w_branch_ret[0].astype(BF16), w_branch_pool[0].astype(BF16), w_out[0].astype(BF16),
                ln1_g[0][None, :], ln1_b[0][None, :])

    conv_wb = jnp.concatenate([conv_w[0].reshape(9, 2 * D_FF), conv_b[0][None, :],
                               jnp.zeros((6, 2 * D_FF), F32)], axis=0).astype(BF16)
    w_down_c = w_down[0].astype(BF16).reshape(FF_NC, FF_CW, D)
    return _ffn(x1, sh2, sc2, g2, w_up[0].astype(BF16), conv_wb, w_down_c,
                ln2_g[0][None, :], ln2_b[0][None, :])
```

```python
import functools

import jax
import jax.numpy as jnp
import numpy as np
from jax import lax
from jax.experimental import pallas as pl
from jax.experimental.pallas import tpu as pltpu

F32 = jnp.float32
BF16 = jnp.bfloat16

D_MODEL = 1024
GRID_W = 64
HEADS = 8
DK = 64
DV = 128
QK_W = HEADS * DK
V_W = HEADS * DV
KV_W = QK_W + V_W
CHUNK = 128
PAIRS = HEADS // 2
POOL_WINDOWS = (2, 4, 8, 16)
POOL_GD = 128
POOL_W = 512
COL_Q = KV_W
COL_G = COL_Q + QK_W
COL_P = COL_G + V_W
COL_GA = COL_P + POOL_W
COL_GB = COL_GA + D_MODEL
IN_W = COL_GB + D_MODEL
D_FF = 2816
FF_CW = 256
FF_NC = D_FF // FF_CW
FF_PAD = 8
N_MOD = 6
LN_EPS = 1e-6
ALPHA = 2.0 ** 0.25
POOL_HALO = 8
U_HALO = 16
MOD_ROWS = 8

VMEM_LIMIT = 60 * 1024 * 1024

TB_KV = 1024
TB_MIX = 1024
TB_FFN = 512


def _dot(a, b):
    return jnp.dot(a, b, preferred_element_type=F32)


def _dot_nt(a, b):
    return lax.dot_general(a, b, (((1,), (1,)), ((), ())), preferred_element_type=F32)


def _dot_tn(a, b):
    return lax.dot_general(a, b, (((0,), (0,)), ((), ())), preferred_element_type=F32)


def _ln(x):
    mu = jnp.mean(x, axis=-1, keepdims=True)
    xc = x - mu
    var = jnp.mean(xc * xc, axis=-1, keepdims=True)
    return xc * lax.rsqrt(var + LN_EPS)


def _gelu_tanh(x):
    c = float(np.sqrt(2.0 / np.pi))
    half = 0.5 * x
    return half + half * jnp.tanh(x * (c + (c * 0.044715) * (x * x)))


def _const_spec(shape):
    nd = len(shape)
    return pl.BlockSpec(shape, lambda *_: (0,) * nd, pipeline_mode=pl.Buffered(1))


def _pair_diag(r):
    row = lax.broadcasted_iota(jnp.int32, (CHUNK, DV), 0)
    return jnp.where(row < DK, r[:, :DV], r[:, DV:])


def _chunk_kv(kd, v):
    outs = []
    for p in range(PAIRS):
        r = _dot_tn(kd[:, p * 128:(p + 1) * 128], v[:, p * 256:(p + 1) * 256])
        outs.append(_pair_diag(r))
    return jnp.concatenate(outs, axis=0)


def _adaln_kernel(c_ref, w_ref, b_ref, o_ref):
    c = c_ref[...]
    s = c * jax.nn.sigmoid(c)
    o_ref[...] = _dot(s, w_ref[...]) + b_ref[...]


def _adaln(cc, w_ada, b_ada):
    n = w_ada.shape[1]
    bn = 1536
    return pl.pallas_call(
        _adaln_kernel,
        grid=(n // bn,),
        in_specs=[pl.BlockSpec((MOD_ROWS, D_MODEL), lambda i: (0, 0)),
                  pl.BlockSpec((D_MODEL, bn), lambda i: (0, i)),
                  pl.BlockSpec((1, bn), lambda i: (0, i))],
        out_specs=pl.BlockSpec((MOD_ROWS, bn), lambda i: (0, i)),
        out_shape=jax.ShapeDtypeStruct((MOD_ROWS, n), F32),
        compiler_params=pltpu.CompilerParams(vmem_limit_bytes=VMEM_LIMIT),
        name="adaln",
    )(cc, w_ada, b_ada)


def _ctx_kernel(ctx_ref, sh_ref, sc_ref, wkv_ref, kdf_ref, kdb_ref, cdf_ref, cdb_ref, sf_ref, sb_ref):
    x = ctx_ref[0]
    u = (_ln(x) * (1.0 + sc_ref[...]) + sh_ref[...]).astype(BF16)
    kv = _dot(u, wkv_ref[...])
    k = kv[:, :QK_W] * (DK ** -0.5)
    v = kv[:, QK_W:].astype(BF16)
    n = x.shape[0] // CHUNK
    sf = jnp.zeros((QK_W, DV), F32)
    for c in range(n):
        kc = k[c * CHUNK:(c + 1) * CHUNK]
        vc = v[c * CHUNK:(c + 1) * CHUNK]
        sf = cdf_ref[...] * sf + _chunk_kv((kc * kdf_ref[...]).astype(BF16), vc)
    sb = jnp.zeros((QK_W, DV), F32)
    for c in reversed(range(n)):
        kc = k[c * CHUNK:(c + 1) * CHUNK]
        vc = v[c * CHUNK:(c + 1) * CHUNK]
        sb = cdb_ref[...] * sb + _chunk_kv((kc * kdb_ref[...]).astype(BF16), vc)
    sf_ref[0] = sf
    sb_ref[0] = sb


def _ctx_states(ctx, sh, sc, w_kv, kdf, kdb, cdf, cdb):
    B, Lc, _ = ctx.shape
    st = jax.ShapeDtypeStruct((B, QK_W, DV), F32)
    return pl.pallas_call(
        _ctx_kernel,
        grid=(B,),
        in_specs=[pl.BlockSpec((1, Lc, D_MODEL), lambda b: (b, 0, 0)),
                  _const_spec((1, D_MODEL)), _const_spec((1, D_MODEL)),
                  _const_spec((D_MODEL, KV_W)),
                  _const_spec((CHUNK, QK_W)), _const_spec((CHUNK, QK_W)),
                  _const_spec((QK_W, DV)), _const_spec((QK_W, DV))],
        out_specs=[pl.BlockSpec((1, QK_W, DV), lambda b: (b, 0, 0)),
                   pl.BlockSpec((1, QK_W, DV), lambda b: (b, 0, 0))],
        out_shape=[st, st],
        compiler_params=pltpu.CompilerParams(vmem_limit_bytes=VMEM_LIMIT),
        name="ctx_states",
    )(ctx, sh, sc, w_kv, kdf, kdb, cdf, cdb)


def _kv_kernel(x_ref, sh_ref, sc_ref, wkv_ref, kdb_ref, cdb_ref, sb0_ref, u_ref, kv_ref, sb_ref, s_ref):
    @pl.when(pl.program_id(1) == 0)
    def _():
        s_ref[...] = sb0_ref[0]

    x = x_ref[0]
    u = (_ln(x) * (1.0 + sc_ref[0]) + sh_ref[0]).astype(BF16)
    u_ref[0] = u
    kv = _dot(u, wkv_ref[...])
    k = kv[:, :QK_W] * (DK ** -0.5)
    v = kv[:, QK_W:].astype(BF16)
    kv_ref[0, :, :QK_W] = k.astype(BF16)
    kv_ref[0, :, QK_W:] = v
    n = x.shape[0] // CHUNK
    for c in reversed(range(n)):
        s = s_ref[...]
        sb_ref[0, c] = s.astype(BF16)
        kc = k[c * CHUNK:(c + 1) * CHUNK]
        vc = v[c * CHUNK:(c + 1) * CHUNK]
        s_ref[...] = cdb_ref[...] * s + _chunk_kv((kc * kdb_ref[...]).astype(BF16), vc)


def _kv_states(x, sh, sc, w_kv, kdb, cdb, sb0):
    B, L, _ = x.shape
    tb = TB_KV
    nb = L // tb
    nch = tb // CHUNK
    return pl.pallas_call(
        _kv_kernel,
        grid=(B, nb),
        in_specs=[pl.BlockSpec((1, tb, D_MODEL), lambda b, j: (b, nb - 1 - j, 0)),
                  pl.BlockSpec((1, 1, D_MODEL), lambda b, j: (b, 0, 0)),
                  pl.BlockSpec((1, 1, D_MODEL), lambda b, j: (b, 0, 0)),
                  _const_spec((D_MODEL, KV_W)),
                  _const_spec((CHUNK, QK_W)),
                  _const_spec((QK_W, DV)),
                  pl.BlockSpec((1, QK_W, DV), lambda b, j: (b, 0, 0))],
        out_specs=[pl.BlockSpec((1, tb, D_MODEL), lambda b, j: (b, nb - 1 - j, 0)),
                   pl.BlockSpec((1, tb, KV_W), lambda b, j: (b, nb - 1 - j, 0)),
                   pl.BlockSpec((1, nch, QK_W, DV), lambda b, j: (b, nb - 1 - j, 0, 0))],
        out_shape=[jax.ShapeDtypeStruct((B, L, D_MODEL), BF16),
                   jax.ShapeDtypeStruct((B, L, KV_W), BF16),
                   jax.ShapeDtypeStruct((B, L // CHUNK, QK_W, DV), BF16)],
        scratch_shapes=[pltpu.VMEM((QK_W, DV), F32)],
        compiler_params=pltpu.CompilerParams(
            dimension_semantics=("arbitrary", "arbitrary"), vmem_limit_bytes=VMEM_LIMIT),
        name="kv_states",
    )(x, sh, sc, w_kv, kdb, cdb, sb0)


def _pool_features(pe, j, nb, tb, seq_len, poolw_ref, pscale_ref):
    n = pe.shape[0]
    t = j * tb + lax.broadcasted_iota(jnp.int32, (tb, POOL_GD), 0)
    outs = []
    for gi, w in enumerate(POOL_WINDOWS):
        half = w // 2
        a = pe[:, gi * POOL_GD:(gi + 1) * POOL_GD]
        centre = a[POOL_HALO:POOL_HALO + tb]
        s = a
        step = 1
        while step < w:
            s = s + pltpu.roll(s, n - step, axis=0)
            step *= 2
        s = pltpu.roll(s, half, axis=0)[POOL_HALO:POOL_HALO + tb]
        cnt = (jnp.minimum(t + half, seq_len) - jnp.maximum(t - half, 0)).astype(F32)
        diff = (s / cnt - centre).astype(BF16)
        outs.append(_dot(diff, poolw_ref[gi]))
    return jnp.concatenate(outs, axis=-1) * pscale_ref[...]


def _mixer_kernel(x_ref, u_ref, up_ref, un_ref, kv_ref, sb_ref, sf0_ref, g1_ref,
                  wr_ref, dm_ref, qdf_ref, qdb_ref, kdf_ref, cdf_ref,
                  poolw_ref, pscale_ref, wbr_ref, wbp_ref, wout_ref, lng_ref, lnb_ref,
                  o_ref, s_ref, r_ref, *, nb, seq_len):
    j = pl.program_id(1)
    tb = x_ref.shape[1]

    @pl.when(j == 0)
    def _():
        s_ref[...] = sf0_ref[0]

    x = x_ref[0]
    u = u_ref[0]

    q = _dot(u, wr_ref[:, COL_Q:COL_G])
    g = _dot(u, wr_ref[:, COL_G:COL_P])

    lane = lax.broadcasted_iota(jnp.int32, (CHUNK, 128), 1)
    zv = jnp.zeros((CHUNK, DV), BF16)
    zs = jnp.zeros((DK, DV), BF16)

    def block_diag_state(s):
        left = jnp.concatenate([s[:DK], zs], axis=0)
        right = jnp.concatenate([zs, s[DK:]], axis=0)
        return jnp.concatenate([left, right], axis=1)

    for c in range(tb // CHUNK):
        rows = slice(c * CHUNK, (c + 1) * CHUNK)
        qc = q[rows]
        qb16 = qc.astype(BF16)
        qf = (qc * qdf_ref[...]).astype(BF16)
        qb = (qc * qdb_ref[...]).astype(BF16)
        kc = kv_ref[0, rows, 0:QK_W].astype(F32)
        vc = kv_ref[0, rows, QK_W:KV_W]
        kd = (kc * kdf_ref[...]).astype(BF16)
        ys = []
        for p in range(PAIRS):
            ql = slice(p * 128, (p + 1) * 128)
            kp = kc[:, ql]
            k_lo = jnp.where(lane < DK, kp, 0.0).astype(BF16)
            k_hi = jnp.where(lane >= DK, kp, 0.0).astype(BF16)
            krhs = jnp.concatenate([k_lo, k_hi], axis=0)
            sc = (_dot_nt(qb16[:, ql], krhs) * dm_ref[p]).astype(BF16)
            vp = vc[:, p * 256:(p + 1) * 256]
            vrhs = jnp.concatenate([jnp.concatenate([vp[:, :DV], zv], axis=1),
                                    jnp.concatenate([zv, vp[:, DV:]], axis=1)], axis=0)
            y = _dot(sc, vrhs)
            s_f = s_ref[ql, :]
            srhs = jnp.concatenate([block_diag_state(s_f.astype(BF16)),
                                    block_diag_state(sb_ref[0, c, ql, :])], axis=0)
            qlhs = jnp.concatenate([qf[:, ql], qb[:, ql]], axis=1)
            y = y + _dot(qlhs, srhs)
            ys.append(y)
            s_ref[ql, :] = cdf_ref[ql, :] * s_f + _pair_diag(_dot_tn(kd[:, ql], vp))
        gc = g[rows]
        sg = gc * jax.nn.sigmoid(gc)
        for p in range(PAIRS):
            for hh in range(2):
                h = 2 * p + hh
                yh = ys[p][:, hh * DV:(hh + 1) * DV]
                mu = jnp.mean(yh, axis=-1, keepdims=True)
                yc = yh - mu
                var = jnp.mean(yc * yc, axis=-1, keepdims=True)
                yn = yc * lax.rsqrt(var + LN_EPS)
                r_ref[rows, h * DV:(h + 1) * DV] = (yn * sg[:, h * DV:(h + 1) * DV]).astype(BF16)

    ret = _dot(r_ref[...], wbr_ref[...])

    ph = _dot(jnp.concatenate([up_ref[0], un_ref[0]], axis=0), wr_ref[:, COL_P:COL_GA])
    p_prev = ph[U_HALO - POOL_HALO:U_HALO] * (j > 0).astype(F32)
    p_next = ph[U_HALO:U_HALO + POOL_HALO] * (j < nb - 1).astype(F32)
    pm = _dot(u, wr_ref[:, COL_P:COL_GA])
    pe = jnp.concatenate([p_prev, pm, p_next], axis=0)
    feat = _pool_features(pe, j, nb, tb, seq_len, poolw_ref, pscale_ref).astype(BF16)
    pool = _dot(feat, wbp_ref[...])

    ga = _dot(u, wr_ref[:, COL_GA:COL_GB])
    merged = jax.nn.sigmoid(ga) * ret
    gb = _dot(u, wr_ref[:, COL_GB:IN_W])
    merged = (merged + jax.nn.sigmoid(gb) * pool).astype(BF16)
    mix = _dot(merged, wout_ref[...])
    z = ALPHA * x + g1_ref[0] * mix
    o_ref[0] = _ln(z) * lng_ref[...] + lnb_ref[...]


def _mixer(x, u, kv, sb, sf0, g1, w_rest, dm, qdf, qdb, kdf, cdf,
           pool_w, pool_scale, w_br, w_bp, w_out, ln_g, ln_b):
    B, L, _ = x.shape
    tb = TB_MIX
    nb = L // tb
    nch = tb // CHUNK
    hb = tb // U_HALO
    nh = L // U_HALO
    mod_spec = pl.BlockSpec((1, 1, D_MODEL), lambda b, j: (b, 0, 0))
    return pl.pallas_call(
        functools.partial(_mixer_kernel, nb=nb, seq_len=L),
        grid=(B, nb),
        in_specs=[pl.BlockSpec((1, tb, D_MODEL), lambda b, j: (b, j, 0)),
                  pl.BlockSpec((1, tb, D_MODEL), lambda b, j: (b, j, 0)),
                  pl.BlockSpec((1, U_HALO, D_MODEL), lambda b, j: (b, jnp.maximum(j * hb - 1, 0), 0)),
                  pl.BlockSpec((1, U_HALO, D_MODEL),
                               lambda b, j: (b, jnp.minimum((j + 1) * hb, nh - 1), 0)),
                  pl.BlockSpec((1, tb, KV_W), lambda b, j: (b, j, 0)),
                  pl.BlockSpec((1, nch, QK_W, DV), lambda b, j: (b, j, 0, 0)),
                  pl.BlockSpec((1, QK_W, DV), lambda b, j: (b, 0, 0)),
                  mod_spec,
                  _const_spec((D_MODEL, IN_W)),
                  _const_spec((PAIRS, CHUNK, 256)),
                  _const_spec((CHUNK, QK_W)), _const_spec((CHUNK, QK_W)), _const_spec((CHUNK, QK_W)),
                  _const_spec((QK_W, DV)),
                  _const_spec((len(POOL_WINDOWS), POOL_GD, POOL_GD)),
                  _const_spec((1, POOL_W)),
                  _const_spec((V_W, D_MODEL)),
                  _const_spec((POOL_W, D_MODEL)),
                  _const_spec((D_MODEL, D_MODEL)),
                  _const_spec((1, D_MODEL)), _const_spec((1, D_MODEL))],
        out_specs=pl.BlockSpec((1, tb, D_MODEL), lambda b, j: (b, j, 0)),
        out_shape=jax.ShapeDtypeStruct((B, L, D_MODEL), F32),
        scratch_shapes=[pltpu.VMEM((QK_W, DV), F32), pltpu.VMEM((tb, V_W), BF16)],
        compiler_params=pltpu.CompilerParams(
            dimension_semantics=("arbitrary", "arbitrary"), vmem_limit_bytes=VMEM_LIMIT),
        name="mixer",
    )(x, u, u, u, kv, sb, sf0, g1, w_rest, dm, qdf, qdb, kdf, cdf,
      pool_w, pool_scale, w_br, w_bp, w_out, ln_g, ln_b)


def _ffn_kernel(x_ref, xn_ref, sh_ref, sc_ref, g2_ref, wup_ref, cw_ref, wdn_ref,
                lng_ref, lnb_ref, o_ref, u_ref, acc_ref, ha_ref, hb_ref, top_ref, *, nb):
    j = pl.program_id(1)
    tb = x_ref.shape[1]
    m = tb + GRID_W
    n = tb + 2 * GRID_W
    scale = 1.0 + sc_ref[0]
    shift = sh_ref[0]
    x = x_ref[0]

    def mod(v):
        return _ln(v) * scale + shift

    @pl.when(j == 0)
    def _():
        top_ref[...] = jnp.zeros_like(top_ref)

    u_ref[0:tb] = mod(x).astype(BF16)
    u_ref[tb:m] = (mod(xn_ref[0]) * (j < nb - 1).astype(F32)).astype(BF16)

    col = lax.broadcasted_iota(jnp.int32, (tb, FF_CW), 0) & (GRID_W - 1)
    has_left = col > 0
    has_right = col < GRID_W - 1

    pad = jnp.zeros((FF_PAD, FF_CW), F32)
    for h_ref in (ha_ref, hb_ref):
        for ab in range(2):
            h_ref[ab, 0:FF_PAD] = pad
            h_ref[ab, FF_PAD + n:FF_PAD + n + FF_PAD] = pad

    def lanes(c, half):
        return pl.ds(pl.multiple_of(half * D_FF + c * FF_CW, FF_CW), FF_CW)

    def up(c, h_ref):
        u = u_ref[...]
        for half in range(2):
            h = _dot(u, wup_ref[:, lanes(c, half)])
            h_ref[half, FF_PAD:FF_PAD + GRID_W] = top_ref[half, c]
            h_ref[half, FF_PAD + GRID_W:FF_PAD + n] = h
            top_ref[half, c] = h[tb - GRID_W:tb]

    def conv(h_ref, ab, cw):
        cols = []
        for dc in range(3):
            hs = h_ref[ab, FF_PAD + dc - 1:FF_PAD + dc - 1 + n].astype(BF16)
            g = None
            for dr in range(3):
                term = cw[3 * dr + dc:3 * dr + dc + 1] * hs[dr * GRID_W:dr * GRID_W + tb]
                g = term if g is None else g + term
            cols.append(g)
        zero = jnp.zeros_like(cols[1])
        return (cols[1] + cw[9:10]) + (jnp.where(has_left, cols[0], zero) + jnp.where(has_right, cols[2], zero))

    def down(c, h_ref):
        a = conv(h_ref, 0, cw_ref[:, lanes(c, 0)])
        b = conv(h_ref, 1, cw_ref[:, lanes(c, 1)])
        acc_ref[...] += _dot(_gelu_tanh(a.astype(F32)).astype(BF16) * b, wdn_ref[c])

    acc_ref[...] = jnp.zeros_like(acc_ref)
    up(0, ha_ref)

    def body(i, carry):
        c = 2 * i
        up(c + 1, hb_ref)
        down(c, ha_ref)
        up(c + 2, ha_ref)
        down(c + 1, hb_ref)
        return carry

    lax.fori_loop(0, FF_NC // 2, body, 0)
    down(FF_NC - 1, ha_ref)
    z = ALPHA * x + g2_ref[0] * acc_ref[...]
    o_ref[0] = _ln(z) * lng_ref[...] + lnb_ref[...]


def _ffn(x, sh, sc, g2, w_up, conv_wb, w_down, ln_g, ln_b):
    B, L, _ = x.shape
    tb = TB_FFN
    nb = L // tb
    hb = tb // GRID_W
    nh = L // GRID_W
    mod_spec = pl.BlockSpec((1, 1, D_MODEL), lambda b, j: (b, 0, 0))
    return pl.pallas_call(
        functools.partial(_ffn_kernel, nb=nb),
        grid=(B, nb),
        in_specs=[pl.BlockSpec((1, tb, D_MODEL), lambda b, j: (b, j, 0)),
                  pl.BlockSpec((1, GRID_W, D_MODEL),
                               lambda b, j: (b, jnp.minimum((j + 1) * hb, nh - 1), 0)),
                  mod_spec, mod_spec, mod_spec,
                  _const_spec((D_MODEL, 2 * D_FF)),
                  _const_spec((16, 2 * D_FF)),
                  _const_spec((FF_NC, FF_CW, D_MODEL)),
                  _const_spec((1, D_MODEL)), _const_spec((1, D_MODEL))],
        out_specs=pl.BlockSpec((1, tb, D_MODEL), lambda b, j: (b, j, 0)),
        out_shape=jax.ShapeDtypeStruct((B, L, D_MODEL), F32),
        scratch_shapes=[pltpu.VMEM((tb + GRID_W, D_MODEL), BF16),
                        pltpu.VMEM((tb, D_MODEL), F32),
                        pltpu.VMEM((2, tb + 2 * GRID_W + 2 * FF_PAD, FF_CW), F32),
                        pltpu.VMEM((2, tb + 2 * GRID_W + 2 * FF_PAD, FF_CW), F32),
                        pltpu.VMEM((2, FF_NC, GRID_W, FF_CW), F32)],
        compiler_params=pltpu.CompilerParams(
            dimension_semantics=("arbitrary", "arbitrary"), vmem_limit_bytes=VMEM_LIMIT),
        name="conv_ffn",
    )(x, x, sh, sc, g2, w_up, conv_wb, w_down, ln_g, ln_b)


def _decay_tables(ret_decay_logit):
    lg = jax.nn.log_sigmoid(ret_decay_logit.astype(F32))
    pos = jnp.arange(CHUNK, dtype=F32)
    diff = pos[:, None] - pos[None, :]
    d_f = jnp.where(diff[None] >= 0, jnp.exp(jnp.maximum(diff, 0.0)[None] * lg[0][:, None, None]), 0.0)
    d_b = jnp.where(diff[None] <= 0, jnp.exp(jnp.maximum(-diff, 0.0)[None] * lg[1][:, None, None]), 0.0)
    dm = (d_f + d_b).reshape(PAIRS, 2, CHUNK, CHUNK).transpose(0, 2, 1, 3).reshape(PAIRS, CHUNK, 2 * CHUNK)

    def lanes(t):
        return jnp.repeat(t, DK, axis=1)

    qdf = lanes(jnp.exp((pos + 1.0)[:, None] * lg[0][None, :]))
    qdb = lanes(jnp.exp((CHUNK - pos)[:, None] * lg[1][None, :]))
    kdf = lanes(jnp.exp((CHUNK - 1.0 - pos)[:, None] * lg[0][None, :]))
    kdb = lanes(jnp.exp(pos[:, None] * lg[1][None, :]))

    def rows(t):
        return jnp.broadcast_to(jnp.repeat(t, DK)[:, None], (QK_W, DV))

    cdf = rows(jnp.exp(CHUNK * lg[0]))
    cdb = rows(jnp.exp(CHUNK * lg[1]))
    return dm, qdf, qdb, kdf, kdb, cdf, cdb


def kernel(x, c, ctx, c_ctx, w_ada, b_ada, w_in, ret_decay_logit, pool_w, pool_scale, w_branch_ret,
           w_branch_pool, w_out, ln1_g, ln1_b, w_up, conv_w, conv_b, w_down, ln2_g, ln2_b):
    B = x.shape[0]
    D = D_MODEL
    assert w_ada.shape[0] == 1, "single-layer stack"

    cc = jnp.zeros((MOD_ROWS, D), F32).at[:B].set(c).at[B].set(c_ctx)
    mod = _adaln(cc, w_ada[0], b_ada[0][None, :])
    lat = mod[:B].reshape(B, N_MOD, 1, D)
    sh1, sc1, g1, sh2, sc2, g2 = (lat[:, i] for i in range(N_MOD))
    sh1c = mod[B:B + 1, 0:D]
    sc1c = mod[B:B + 1, D:2 * D]

    dm, qdf, qdb, kdf, kdb, cdf, cdb = _decay_tables(ret_decay_logit[0])

    w_in_b = w_in[0].astype(BF16)

    s_f, s_b = _ctx_states(ctx, sh1c, sc1c, w_in_b, kdf, kdb, cdf, cdb)
    u, kv, sb = _kv_states(x, sh1, sc1, w_in_b, kdb, cdb, s_b)
    x1 = _mixer(x, u, kv, sb, s_f, g1, w_in_b, dm, qdf, qdb, kdf, cdf,
                pool_w[0].astype(BF16), pool_scale[0][None, :],
                w_branch_ret[0].astype(BF16), w_branch_pool[0].astype(BF16), w_out[0].astype(BF16),
                ln1_g[0][None, :], ln1_b[0][None, :])

    conv_wb = jnp.concatenate([conv_w[0].reshape(9, 2 * D_FF), conv_b[0][None, :],
                               jnp.zeros((6, 2 * D_FF), F32)], axis=0).astype(BF16)
    w_down_c = w_down[0].astype(BF16).reshape(FF_NC, FF_CW, D)
    return _ffn(x1, sh2, sc2, g2, w_up[0].astype(BF16), conv_wb, w_down_c,
                ln2_g[0][None, :], ln2_b[0][None, :])
```

```python
import functools

import jax
import jax.numpy as jnp
import numpy as np
from jax import lax
from jax.experimental import pallas as pl
from jax.experimental.pallas import tpu as pltpu

F32 = jnp.float32
BF16 = jnp.bfloat16

D_MODEL = 1024
GRID_W = 64
HEADS = 8
DK = 64
DV = 128
QK_W = HEADS * DK
V_W = HEADS * DV
KV_W = QK_W + V_W
CHUNK = 128
PAIRS = HEADS // 2
POOL_WINDOWS = (2, 4, 8, 16)
POOL_GD = 128
POOL_W = 512
COL_Q = KV_W
COL_G = COL_Q + QK_W
COL_P = COL_G + V_W
COL_GA = COL_P + POOL_W
COL_GB = COL_GA + D_MODEL
IN_W = COL_GB + D_MODEL
D_FF = 2816
FF_CW = 256
FF_NC = D_FF // FF_CW
FF_PAD = 8
N_MOD = 6
LN_EPS = 1e-6
ALPHA = 2.0 ** 0.25
POOL_HALO = 8
U_HALO = 16
MOD_ROWS = 8

VMEM_LIMIT = 60 * 1024 * 1024

TB_KV = 1024
TB_MIX = 1024
TB_FFN = 1024


def _dot(a, b):
    return jnp.dot(a, b, preferred_element_type=F32)


def _dot_nt(a, b):
    return lax.dot_general(a, b, (((1,), (1,)), ((), ())), preferred_element_type=F32)


def _dot_tn(a, b):
    return lax.dot_general(a, b, (((0,), (0,)), ((), ())), preferred_element_type=F32)


def _ln(x):
    mu = jnp.mean(x, axis=-1, keepdims=True)
    xc = x - mu
    var = jnp.mean(xc * xc, axis=-1, keepdims=True)
    return xc * lax.rsqrt(var + LN_EPS)


def _gelu_tanh(x):
    c = float(np.sqrt(2.0 / np.pi))
    half = 0.5 * x
    return half + half * jnp.tanh(x * (c + (c * 0.044715) * (x * x)))


def _const_spec(shape):
    nd = len(shape)
    return pl.BlockSpec(shape, lambda *_: (0,) * nd, pipeline_mode=pl.Buffered(1))


def _pair_diag(r):
    row = lax.broadcasted_iota(jnp.int32, (CHUNK, DV), 0)
    return jnp.where(row < DK, r[:, :DV], r[:, DV:])


def _chunk_kv(kd, v):
    outs = []
    for p in range(PAIRS):
        r = _dot_tn(kd[:, p * 128:(p + 1) * 128], v[:, p * 256:(p + 1) * 256])
        outs.append(_pair_diag(r))
    return jnp.concatenate(outs, axis=0)


def _adaln_kernel(c_ref, w_ref, b_ref, o_ref):
    c = c_ref[...]
    s = c * jax.nn.sigmoid(c)
    o_ref[...] = _dot(s, w_ref[...]) + b_ref[...]


def _adaln(cc, w_ada, b_ada):
    n = w_ada.shape[1]
    bn = 1536
    return pl.pallas_call(
        _adaln_kernel,
        grid=(n // bn,),
        in_specs=[pl.BlockSpec((MOD_ROWS, D_MODEL), lambda i: (0, 0)),
                  pl.BlockSpec((D_MODEL, bn), lambda i: (0, i)),
                  pl.BlockSpec((1, bn), lambda i: (0, i))],
        out_specs=pl.BlockSpec((MOD_ROWS, bn), lambda i: (0, i)),
        out_shape=jax.ShapeDtypeStruct((MOD_ROWS, n), F32),
        compiler_params=pltpu.CompilerParams(vmem_limit_bytes=VMEM_LIMIT),
        name="adaln",
    )(cc, w_ada, b_ada)


def _ctx_kernel(ctx_ref, sh_ref, sc_ref, wkv_ref, kdf_ref, kdb_ref, cdf_ref, cdb_ref, sf_ref, sb_ref):
    x = ctx_ref[0]
    u = (_ln(x) * (1.0 + sc_ref[...]) + sh_ref[...]).astype(BF16)
    kv = _dot(u, wkv_ref[...])
    k = kv[:, :QK_W] * (DK ** -0.5)
    v = kv[:, QK_W:].astype(BF16)
    n = x.shape[0] // CHUNK
    sf = jnp.zeros((QK_W, DV), F32)
    for c in range(n):
        kc = k[c * CHUNK:(c + 1) * CHUNK]
        vc = v[c * CHUNK:(c + 1) * CHUNK]
        sf = cdf_ref[...] * sf + _chunk_kv((kc * kdf_ref[...]).astype(BF16), vc)
    sb = jnp.zeros((QK_W, DV), F32)
    for c in reversed(range(n)):
        kc = k[c * CHUNK:(c + 1) * CHUNK]
        vc = v[c * CHUNK:(c + 1) * CHUNK]
        sb = cdb_ref[...] * sb + _chunk_kv((kc * kdb_ref[...]).astype(BF16), vc)
    sf_ref[0] = sf
    sb_ref[0] = sb


def _ctx_states(ctx, sh, sc, w_kv, kdf, kdb, cdf, cdb):
    B, Lc, _ = ctx.shape
    st = jax.ShapeDtypeStruct((B, QK_W, DV), F32)
    return pl.pallas_call(
        _ctx_kernel,
        grid=(B,),
        in_specs=[pl.BlockSpec((1, Lc, D_MODEL), lambda b: (b, 0, 0)),
                  _const_spec((1, D_MODEL)), _const_spec((1, D_MODEL)),
                  _const_spec((D_MODEL, KV_W)),
                  _const_spec((CHUNK, QK_W)), _const_spec((CHUNK, QK_W)),
                  _const_spec((QK_W, DV)), _const_spec((QK_W, DV))],
        out_specs=[pl.BlockSpec((1, QK_W, DV), lambda b: (b, 0, 0)),
                   pl.BlockSpec((1, QK_W, DV), lambda b: (b, 0, 0))],
        out_shape=[st, st],
        compiler_params=pltpu.CompilerParams(vmem_limit_bytes=VMEM_LIMIT),
        name="ctx_states",
    )(ctx, sh, sc, w_kv, kdf, kdb, cdf, cdb)


def _kv_kernel(x_ref, sh_ref, sc_ref, wkv_ref, kdb_ref, cdb_ref, sb0_ref, u_ref, kv_ref, sb_ref, s_ref):
    @pl.when(pl.program_id(1) == 0)
    def _():
        s_ref[...] = sb0_ref[0]

    x = x_ref[0]
    u = (_ln(x) * (1.0 + sc_ref[0]) + sh_ref[0]).astype(BF16)
    u_ref[0] = u
    kv = _dot(u, wkv_ref[...])
    k = kv[:, :QK_W] * (DK ** -0.5)
    v = kv[:, QK_W:].astype(BF16)
    kv_ref[0, :, :QK_W] = k.astype(BF16)
    kv_ref[0, :, QK_W:] = v
    n = x.shape[0] // CHUNK
    for c in reversed(range(n)):
        s = s_ref[...]
        sb_ref[0, c] = s.astype(BF16)
        kc = k[c * CHUNK:(c + 1) * CHUNK]
        vc = v[c * CHUNK:(c + 1) * CHUNK]
        s_ref[...] = cdb_ref[...] * s + _chunk_kv((kc * kdb_ref[...]).astype(BF16), vc)


def _kv_states(x, sh, sc, w_kv, kdb, cdb, sb0):
    B, L, _ = x.shape
    tb = TB_KV
    nb = L // tb
    nch = tb // CHUNK
    return pl.pallas_call(
        _kv_kernel,
        grid=(B, nb),
        in_specs=[pl.BlockSpec((1, tb, D_MODEL), lambda b, j: (b, nb - 1 - j, 0)),
                  pl.BlockSpec((1, 1, D_MODEL), lambda b, j: (b, 0, 0)),
                  pl.BlockSpec((1, 1, D_MODEL), lambda b, j: (b, 0, 0)),
                  _const_spec((D_MODEL, KV_W)),
                  _const_spec((CHUNK, QK_W)),
                  _const_spec((QK_W, DV)),
                  pl.BlockSpec((1, QK_W, DV), lambda b, j: (b, 0, 0))],
        out_specs=[pl.BlockSpec((1, tb, D_MODEL), lambda b, j: (b, nb - 1 - j, 0)),
                   pl.BlockSpec((1, tb, KV_W), lambda b, j: (b, nb - 1 - j, 0)),
                   pl.BlockSpec((1, nch, QK_W, DV), lambda b, j: (b, nb - 1 - j, 0, 0))],
        out_shape=[jax.ShapeDtypeStruct((B, L, D_MODEL), BF16),
                   jax.ShapeDtypeStruct((B, L, KV_W), BF16),
                   jax.ShapeDtypeStruct((B, L // CHUNK, QK_W, DV), BF16)],
        scratch_shapes=[pltpu.VMEM((QK_W, DV), F32)],
        compiler_params=pltpu.CompilerParams(
            dimension_semantics=("arbitrary", "arbitrary"), vmem_limit_bytes=VMEM_LIMIT),
        name="kv_states",
    )(x, sh, sc, w_kv, kdb, cdb, sb0)


def _pool_features(pe, j, nb, tb, seq_len, poolw_ref, pscale_ref):
    n = pe.shape[0]
    t = j * tb + lax.broadcasted_iota(jnp.int32, (tb, POOL_GD), 0)
    outs = []
    for gi, w in enumerate(POOL_WINDOWS):
        half = w // 2
        a = pe[:, gi * POOL_GD:(gi + 1) * POOL_GD]
        centre = a[POOL_HALO:POOL_HALO + tb]
        s = a
        step = 1
        while step < w:
            s = s + pltpu.roll(s, n - step, axis=0)
            step *= 2
        s = pltpu.roll(s, half, axis=0)[POOL_HALO:POOL_HALO + tb]
        cnt = (jnp.minimum(t + half, seq_len) - jnp.maximum(t - half, 0)).astype(F32)
        diff = (s / cnt - centre).astype(BF16)
        outs.append(_dot(diff, poolw_ref[gi]))
    return jnp.concatenate(outs, axis=-1) * pscale_ref[...]


def _mixer_kernel(x_ref, u_ref, up_ref, un_ref, kv_ref, sb_ref, sf0_ref, g1_ref,
                  wr_ref, dm_ref, qdf_ref, qdb_ref, kdf_ref, cdf_ref,
                  poolw_ref, pscale_ref, wbr_ref, wbp_ref, wout_ref, lng_ref, lnb_ref,
                  o_ref, s_ref, r_ref, *, nb, seq_len):
    j = pl.program_id(1)
    tb = x_ref.shape[1]

    @pl.when(j == 0)
    def _():
        s_ref[...] = sf0_ref[0]

    x = x_ref[0]
    u = u_ref[0]

    q = _dot(u, wr_ref[:, COL_Q:COL_G])
    g = _dot(u, wr_ref[:, COL_G:COL_P])

    lane = lax.broadcasted_iota(jnp.int32, (CHUNK, 128), 1)
    zv = jnp.zeros((CHUNK, DV), BF16)
    zs = jnp.zeros((DK, DV), BF16)

    def block_diag_state(s):
        left = jnp.concatenate([s[:DK], zs], axis=0)
        right = jnp.concatenate([zs, s[DK:]], axis=0)
        return jnp.concatenate([left, right], axis=1)

    for c in range(tb // CHUNK):
        rows = slice(c * CHUNK, (c + 1) * CHUNK)
        qc = q[rows]
        qb16 = qc.astype(BF16)
        qf = (qc * qdf_ref[...]).astype(BF16)
        qb = (qc * qdb_ref[...]).astype(BF16)
        kc = kv_ref[0, rows, 0:QK_W].astype(F32)
        vc = kv_ref[0, rows, QK_W:KV_W]
        kd = (kc * kdf_ref[...]).astype(BF16)
        ys = []
        for p in range(PAIRS):
            ql = slice(p * 128, (p + 1) * 128)
            kp = kc[:, ql]
            k_lo = jnp.where(lane < DK, kp, 0.0).astype(BF16)
            k_hi = jnp.where(lane >= DK, kp, 0.0).astype(BF16)
            krhs = jnp.concatenate([k_lo, k_hi], axis=0)
            sc = (_dot_nt(qb16[:, ql], krhs) * dm_ref[p]).astype(BF16)
            vp = vc[:, p * 256:(p + 1) * 256]
            vrhs = jnp.concatenate([jnp.concatenate([vp[:, :DV], zv], axis=1),
                                    jnp.concatenate([zv, vp[:, DV:]], axis=1)], axis=0)
            y = _dot(sc, vrhs)
            s_f = s_ref[ql, :]
            srhs = jnp.concatenate([block_diag_state(s_f.astype(BF16)),
                                    block_diag_state(sb_ref[0, c, ql, :])], axis=0)
            qlhs = jnp.concatenate([qf[:, ql], qb[:, ql]], axis=1)
            y = y + _dot(qlhs, srhs)
            ys.append(y)
            s_ref[ql, :] = cdf_ref[ql, :] * s_f + _pair_diag(_dot_tn(kd[:, ql], vp))
        gc = g[rows]
        sg = gc * jax.nn.sigmoid(gc)
        for p in range(PAIRS):
            for hh in range(2):
                h = 2 * p + hh
                yh = ys[p][:, hh * DV:(hh + 1) * DV]
                mu = jnp.mean(yh, axis=-1, keepdims=True)
                yc = yh - mu
                var = jnp.mean(yc * yc, axis=-1, keepdims=True)
                yn = yc * lax.rsqrt(var + LN_EPS)
                r_ref[rows, h * DV:(h + 1) * DV] = (yn * sg[:, h * DV:(h + 1) * DV]).astype(BF16)

    ret = _dot(r_ref[...], wbr_ref[...])

    ph = _dot(jnp.concatenate([up_ref[0], un_ref[0]], axis=0), wr_ref[:, COL_P:COL_GA])
    p_prev = ph[U_HALO - POOL_HALO:U_HALO] * (j > 0).astype(F32)
    p_next = ph[U_HALO:U_HALO + POOL_HALO] * (j < nb - 1).astype(F32)
    pm = _dot(u, wr_ref[:, COL_P:COL_GA])
    pe = jnp.concatenate([p_prev, pm, p_next], axis=0)
    feat = _pool_features(pe, j, nb, tb, seq_len, poolw_ref, pscale_ref).astype(BF16)
    pool = _dot(feat, wbp_ref[...])

    ga = _dot(u, wr_ref[:, COL_GA:COL_GB])
    merged = jax.nn.sigmoid(ga) * ret
    gb = _dot(u, wr_ref[:, COL_GB:IN_W])
    merged = (merged + jax.nn.sigmoid(gb) * pool).astype(BF16)
    mix = _dot(merged, wout_ref[...])
    z = ALPHA * x + g1_ref[0] * mix
    o_ref[0] = _ln(z) * lng_ref[...] + lnb_ref[...]


def _mixer(x, u, kv, sb, sf0, g1, w_rest, dm, qdf, qdb, kdf, cdf,
           pool_w, pool_scale, w_br, w_bp, w_out, ln_g, ln_b):
    B, L, _ = x.shape
    tb = TB_MIX
    nb = L // tb
    nch = tb // CHUNK
    hb = tb // U_HALO
    nh = L // U_HALO
    mod_spec = pl.BlockSpec((1, 1, D_MODEL), lambda b, j: (b, 0, 0))
    return pl.pallas_call(
        functools.partial(_mixer_kernel, nb=nb, seq_len=L),
        grid=(B, nb),
        in_specs=[pl.BlockSpec((1, tb, D_MODEL), lambda b, j: (b, j, 0)),
                  pl.BlockSpec((1, tb, D_MODEL), lambda b, j: (b, j, 0)),
                  pl.BlockSpec((1, U_HALO, D_MODEL), lambda b, j: (b, jnp.maximum(j * hb - 1, 0), 0)),
                  pl.BlockSpec((1, U_HALO, D_MODEL),
                               lambda b, j: (b, jnp.minimum((j + 1) * hb, nh - 1), 0)),
                  pl.BlockSpec((1, tb, KV_W), lambda b, j: (b, j, 0)),
                  pl.BlockSpec((1, nch, QK_W, DV), lambda b, j: (b, j, 0, 0)),
                  pl.BlockSpec((1, QK_W, DV), lambda b, j: (b, 0, 0)),
                  mod_spec,
                  _const_spec((D_MODEL, IN_W)),
                  _const_spec((PAIRS, CHUNK, 256)),
                  _const_spec((CHUNK, QK_W)), _const_spec((CHUNK, QK_W)), _const_spec((CHUNK, QK_W)),
                  _const_spec((QK_W, DV)),
                  _const_spec((len(POOL_WINDOWS), POOL_GD, POOL_GD)),
                  _const_spec((1, POOL_W)),
                  _const_spec((V_W, D_MODEL)),
                  _const_spec((POOL_W, D_MODEL)),
                  _const_spec((D_MODEL, D_MODEL)),
                  _const_spec((1, D_MODEL)), _const_spec((1, D_MODEL))],
        out_specs=pl.BlockSpec((1, tb, D_MODEL), lambda b, j: (b, j, 0)),
        out_shape=jax.ShapeDtypeStruct((B, L, D_MODEL), F32),
        scratch_shapes=[pltpu.VMEM((QK_W, DV), F32), pltpu.VMEM((tb, V_W), BF16)],
        compiler_params=pltpu.CompilerParams(
            dimension_semantics=("arbitrary", "arbitrary"), vmem_limit_bytes=VMEM_LIMIT),
        name="mixer",
    )(x, u, u, u, kv, sb, sf0, g1, w_rest, dm, qdf, qdb, kdf, cdf,
      pool_w, pool_scale, w_br, w_bp, w_out, ln_g, ln_b)


def _ffn_kernel(x_ref, xn_ref, sh_ref, sc_ref, g2_ref, wup_ref, cw_ref, wdn_ref,
                lng_ref, lnb_ref, o_ref, u_ref, acc_ref, ha_ref, hb_ref, top_ref, *, nb):
    j = pl.program_id(1)
    tb = x_ref.shape[1]
    m = tb + GRID_W
    n = tb + 2 * GRID_W
    scale = 1.0 + sc_ref[0]
    shift = sh_ref[0]
    x = x_ref[0]

    def mod(v):
        return _ln(v) * scale + shift

    @pl.when(j == 0)
    def _():
        top_ref[...] = jnp.zeros_like(top_ref)

    u_ref[0:tb] = mod(x).astype(BF16)
    u_ref[tb:m] = (mod(xn_ref[0]) * (j < nb - 1).astype(F32)).astype(BF16)

    col = lax.broadcasted_iota(jnp.int32, (tb, FF_CW), 0) & (GRID_W - 1)
    has_left = col > 0
    has_right = col < GRID_W - 1

    pad = jnp.zeros((FF_PAD, FF_CW), F32)
    for h_ref in (ha_ref, hb_ref):
        for ab in range(2):
            h_ref[ab, 0:FF_PAD] = pad
            h_ref[ab, FF_PAD + n:FF_PAD + n + FF_PAD] = pad

    def lanes(c, half):
        return pl.ds(pl.multiple_of(half * D_FF + c * FF_CW, FF_CW), FF_CW)

    def up(c, h_ref):
        u = u_ref[...]
        for half in range(2):
            h = _dot(u, wup_ref[:, lanes(c, half)])
            h_ref[half, FF_PAD:FF_PAD + GRID_W] = top_ref[half, c]
            h_ref[half, FF_PAD + GRID_W:FF_PAD + n] = h
            top_ref[half, c] = h[tb - GRID_W:tb]

    def conv(h_ref, ab, cw):
        cols = []
        for dc in range(3):
            hs = h_ref[ab, FF_PAD + dc - 1:FF_PAD + dc - 1 + n].astype(BF16)
            g = None
            for dr in range(3):
                term = cw[3 * dr + dc:3 * dr + dc + 1] * hs[dr * GRID_W:dr * GRID_W + tb]
                g = term if g is None else g + term
            cols.append(g)
        zero = jnp.zeros_like(cols[1])
        return (cols[1] + cw[9:10]) + (jnp.where(has_left, cols[0], zero) + jnp.where(has_right, cols[2], zero))

    def down(c, h_ref):
        a = conv(h_ref, 0, cw_ref[:, lanes(c, 0)])
        b = conv(h_ref, 1, cw_ref[:, lanes(c, 1)])
        acc_ref[...] += _dot(_gelu_tanh(a.astype(F32)).astype(BF16) * b, wdn_ref[c])

    acc_ref[...] = jnp.zeros_like(acc_ref)
    up(0, ha_ref)

    def body(i, carry):
        c = 2 * i
        up(c + 1, hb_ref)
        down(c, ha_ref)
        up(c + 2, ha_ref)
        down(c + 1, hb_ref)
        return carry

    lax.fori_loop(0, FF_NC // 2, body, 0)
    down(FF_NC - 1, ha_ref)
    z = ALPHA * x + g2_ref[0] * acc_ref[...]
    o_ref[0] = _ln(z) * lng_ref[...] + lnb_ref[...]


def _ffn(x, sh, sc, g2, w_up, conv_wb, w_down, ln_g, ln_b):
    B, L, _ = x.shape
    tb = TB_FFN
    nb = L // tb
    hb = tb // GRID_W
    nh = L // GRID_W
    mod_spec = pl.BlockSpec((1, 1, D_MODEL), lambda b, j: (b, 0, 0))
    return pl.pallas_call(
        functools.partial(_ffn_kernel, nb=nb),
        grid=(B, nb),
        in_specs=[pl.BlockSpec((1, tb, D_MODEL), lambda b, j: (b, j, 0)),
                  pl.BlockSpec((1, GRID_W, D_MODEL),
                               lambda b, j: (b, jnp.minimum((j + 1) * hb, nh - 1), 0)),
                  mod_spec, mod_spec, mod_spec,
                  _const_spec((D_MODEL, 2 * D_FF)),
                  _const_spec((16, 2 * D_FF)),
                  _const_spec((FF_NC, FF_CW, D_MODEL)),
                  _const_spec((1, D_MODEL)), _const_spec((1, D_MODEL))],
        out_specs=pl.BlockSpec((1, tb, D_MODEL), lambda b, j: (b, j, 0)),
        out_shape=jax.ShapeDtypeStruct((B, L, D_MODEL), F32),
        scratch_shapes=[pltpu.VMEM((tb + GRID_W, D_MODEL), BF16),
                        pltpu.VMEM((tb, D_MODEL), F32),
                        pltpu.VMEM((2, tb + 2 * GRID_W + 2 * FF_PAD, FF_CW), F32),
                        pltpu.VMEM((2, tb + 2 * GRID_W + 2 * FF_PAD, FF_CW), F32),
                        pltpu.VMEM((2, FF_NC, GRID_W, FF_CW), F32)],
        compiler_params=pltpu.CompilerParams(
            dimension_semantics=("arbitrary", "arbitrary"), vmem_limit_bytes=VMEM_LIMIT),
        name="conv_ffn",
    )(x, x, sh, sc, g2, w_up, conv_wb, w_down, ln_g, ln_b)


def _decay_tables(ret_decay_logit):
    lg = jax.nn.log_sigmoid(ret_decay_logit.astype(F32))
    pos = jnp.arange(CHUNK, dtype=F32)
    diff = pos[:, None] - pos[None, :]
    d_f = jnp.where(diff[None] >= 0, jnp.exp(jnp.maximum(diff, 0.0)[None] * lg[0][:, None, None]), 0.0)
    d_b = jnp.where(diff[None] <= 0, jnp.exp(jnp.maximum(-diff, 0.0)[None] * lg[1][:, None, None]), 0.0)
    dm = (d_f + d_b).reshape(PAIRS, 2, CHUNK, CHUNK).transpose(0, 2, 1, 3).reshape(PAIRS, CHUNK, 2 * CHUNK)

    def lanes(t):
        return jnp.repeat(t, DK, axis=1)

    qdf = lanes(jnp.exp((pos + 1.0)[:, None] * lg[0][None, :]))
    qdb = lanes(jnp.exp((CHUNK - pos)[:, None] * lg[1][None, :]))
    kdf = lanes(jnp.exp((CHUNK - 1.0 - pos)[:, None] * lg[0][None, :]))
    kdb = lanes(jnp.exp(pos[:, None] * lg[1][None, :]))

    def rows(t):
        return jnp.broadcast_to(jnp.repeat(t, DK)[:, None], (QK_W, DV))

    cdf = rows(jnp.exp(CHUNK * lg[0]))
    cdb = rows(jnp.exp(CHUNK * lg[1]))
    return dm, qdf, qdb, kdf, kdb, cdf, cdb


def kernel(x, c, ctx, c_ctx, w_ada, b_ada, w_in, ret_decay_logit, pool_w, pool_scale, w_branch_ret,
           w_branch_pool, w_out, ln1_g, ln1_b, w_up, conv_w, conv_b, w_down, ln2_g, ln2_b):
    B = x.shape[0]
    D = D_MODEL
    assert w_ada.shape[0] == 1, "single-layer stack"

    cc = jnp.zeros((MOD_ROWS, D), F32).at[:B].set(c).at[B].set(c_ctx)
    mod = _adaln(cc, w_ada[0], b_ada[0][None, :])
    lat = mod[:B].reshape(B, N_MOD, 1, D)
    sh1, sc1, g1, sh2, sc2, g2 = (lat[:, i] for i in range(N_MOD))
    sh1c = mod[B:B + 1, 0:D]
    sc1c = mod[B:B + 1, D:2 * D]

    dm, qdf, qdb, kdf, kdb, cdf, cdb = _decay_tables(ret_decay_logit[0])

    w_in_b = w_in[0].astype(BF16)

    s_f, s_b = _ctx_states(ctx, sh1c, sc1c, w_in_b, kdf, kdb, cdf, cdb)
    u, kv, sb = _kv_states(x, sh1, sc1, w_in_b, kdb, cdb, s_b)
    x1 = _mixer(x, u, kv, sb, s_f, g1, w_in_b, dm, qdf, qdb, kdf, cdf,
                pool_w[0].astype(BF16), pool_scale[0][None, :],
                w_branch_ret[0].astype(BF16), w_branch_pool[0].astype(BF16), w_out[0].astype(BF16),
                ln1_g[0][None, :], ln1_b[0][None, :])

    conv_wb = jnp.concatenate([conv_w[0].reshape(9, 2 * D_FF), conv_b[0][None, :],
                               jnp.zeros((6, 2 * D_FF), F32)], axis=0).astype(BF16)
    w_down_c = w_down[0].astype(BF16).reshape(FF_NC, FF_CW, D)
    return _ffn(x1, sh2, sc2, g2, w_up[0].astype(BF16), conv_wb, w_down_c,
                ln2_g[0][None, :], ln2_b[0][None, :])
```

```python
import functools

import jax
import jax.numpy as jnp
import numpy as np
from jax import lax
from jax.experimental import pallas as pl
from jax.experimental.pallas import tpu as pltpu

F32 = jnp.float32
BF16 = jnp.bfloat16

D_MODEL = 1024
GRID_W = 64
HEADS = 8
DK = 64
DV = 128
QK_W = HEADS * DK
V_W = HEADS * DV
KV_W = QK_W + V_W
CHUNK = 128
PAIRS = HEADS // 2
POOL_WINDOWS = (2, 4, 8, 16)
POOL_GD = 128
POOL_W = 512
COL_Q = KV_W
COL_G = COL_Q + QK_W
COL_P = COL_G + V_W
COL_GA = COL_P + POOL_W
COL_GB = COL_GA + D_MODEL
IN_W = COL_GB + D_MODEL
D_FF = 2816
FF_CW = 256
FF_NC = D_FF // FF_CW
FF_PAD = 8
N_MOD = 6
LN_EPS = 1e-6
ALPHA = 2.0 ** 0.25
POOL_HALO = 8
U_HALO = 16
MOD_ROWS = 8

VMEM_LIMIT = 60 * 1024 * 1024

MIX_RB = 256

TB_KV = 1024
TB_MIX = 1024
TB_FFN = 1024


def _dot(a, b):
    return jnp.dot(a, b, preferred_element_type=F32)


def _dot_nt(a, b):
    return lax.dot_general(a, b, (((1,), (1,)), ((), ())), preferred_element_type=F32)


def _dot_tn(a, b):
    return lax.dot_general(a, b, (((0,), (0,)), ((), ())), preferred_element_type=F32)


def _ln(x):
    mu = jnp.mean(x, axis=-1, keepdims=True)
    xc = x - mu
    var = jnp.mean(xc * xc, axis=-1, keepdims=True)
    return xc * lax.rsqrt(var + LN_EPS)


def _gelu_tanh(x):
    c = float(np.sqrt(2.0 / np.pi))
    half = 0.5 * x
    return half + half * jnp.tanh(x * (c + (c * 0.044715) * (x * x)))


def _mod_spec(k, row=None):
    if row is None:
        return pl.BlockSpec((1, 1, 1, D_MODEL), lambda b, *_: (b, k, 0, 0))
    return pl.BlockSpec((1, 1, 1, D_MODEL), lambda *_: (row, k, 0, 0))


def _const_spec(shape):
    nd = len(shape)
    return pl.BlockSpec(shape, lambda *_: (0,) * nd, pipeline_mode=pl.Buffered(1))


def _pair_diag(r):
    row = lax.broadcasted_iota(jnp.int32, (CHUNK, DV), 0)
    return jnp.where(row < DK, r[:, :DV], r[:, DV:])


def _chunk_kv(kd, v):
    outs = []
    for p in range(PAIRS):
        r = _dot_tn(kd[:, p * 128:(p + 1) * 128], v[:, p * 256:(p + 1) * 256])
        outs.append(_pair_diag(r))
    return jnp.concatenate(outs, axis=0)


def _adaln_kernel(c_ref, w_ref, b_ref, o_ref):
    c = c_ref[...]
    s = c * jax.nn.sigmoid(c)
    o_ref[...] = _dot(s, w_ref[...]) + b_ref[...]


def _adaln(cc, w_ada, b_ada):
    n = w_ada.shape[1]
    bn = 1536
    return pl.pallas_call(
        _adaln_kernel,
        grid=(n // bn,),
        in_specs=[pl.BlockSpec((MOD_ROWS, D_MODEL), lambda i: (0, 0)),
                  pl.BlockSpec((D_MODEL, bn), lambda i: (0, i)),
                  pl.BlockSpec((1, bn), lambda i: (0, i))],
        out_specs=pl.BlockSpec((MOD_ROWS, bn), lambda i: (0, i)),
        out_shape=jax.ShapeDtypeStruct((MOD_ROWS, n), F32),
        compiler_params=pltpu.CompilerParams(vmem_limit_bytes=VMEM_LIMIT),
        name="adaln",
    )(cc, w_ada, b_ada)


def _ctx_kernel(ctx_ref, sh_ref, sc_ref, wkv_ref, kdf_ref, kdb_ref, cdf_ref, cdb_ref, sf_ref, sb_ref):
    x = ctx_ref[0]
    u = (_ln(x) * (1.0 + sc_ref[0, 0]) + sh_ref[0, 0]).astype(BF16)
    kv = _dot(u, wkv_ref[...])
    k = kv[:, :QK_W] * (DK ** -0.5)
    v = kv[:, QK_W:].astype(BF16)
    n = x.shape[0] // CHUNK
    sf = jnp.zeros((QK_W, DV), F32)
    for c in range(n):
        kc = k[c * CHUNK:(c + 1) * CHUNK]
        vc = v[c * CHUNK:(c + 1) * CHUNK]
        sf = cdf_ref[...] * sf + _chunk_kv((kc * kdf_ref[...]).astype(BF16), vc)
    sb = jnp.zeros((QK_W, DV), F32)
    for c in reversed(range(n)):
        kc = k[c * CHUNK:(c + 1) * CHUNK]
        vc = v[c * CHUNK:(c + 1) * CHUNK]
        sb = cdb_ref[...] * sb + _chunk_kv((kc * kdb_ref[...]).astype(BF16), vc)
    sf_ref[0] = sf
    sb_ref[0] = sb


def _ctx_states(ctx, mod, w_kv, kdf, kdb, cdf, cdb):
    B, Lc, _ = ctx.shape
    st = jax.ShapeDtypeStruct((B, QK_W, DV), F32)
    return pl.pallas_call(
        _ctx_kernel,
        grid=(B,),
        in_specs=[pl.BlockSpec((1, Lc, D_MODEL), lambda b: (b, 0, 0)),
                  _mod_spec(0, row=B), _mod_spec(1, row=B),
                  _const_spec((D_MODEL, KV_W)),
                  _const_spec((CHUNK, QK_W)), _const_spec((CHUNK, QK_W)),
                  _const_spec((QK_W, DV)), _const_spec((QK_W, DV))],
        out_specs=[pl.BlockSpec((1, QK_W, DV), lambda b: (b, 0, 0)),
                   pl.BlockSpec((1, QK_W, DV), lambda b: (b, 0, 0))],
        out_shape=[st, st],
        compiler_params=pltpu.CompilerParams(vmem_limit_bytes=VMEM_LIMIT),
        name="ctx_states",
    )(ctx, mod, mod, w_kv, kdf, kdb, cdf, cdb)


def _kv_kernel(x_ref, sh_ref, sc_ref, wkv_ref, kdb_ref, cdb_ref, sb0_ref, u_ref, kv_ref, sb_ref, s_ref):
    @pl.when(pl.program_id(1) == 0)
    def _():
        s_ref[...] = sb0_ref[0]

    x = x_ref[0]
    u = (_ln(x) * (1.0 + sc_ref[0, 0]) + sh_ref[0, 0]).astype(BF16)
    u_ref[0] = u
    kv = _dot(u, wkv_ref[...])
    k = kv[:, :QK_W] * (DK ** -0.5)
    v = kv[:, QK_W:].astype(BF16)
    kv_ref[0, :, :QK_W] = k.astype(BF16)
    kv_ref[0, :, QK_W:] = v
    n = x.shape[0] // CHUNK
    for c in reversed(range(n)):
        s = s_ref[...]
        sb_ref[0, c] = s.astype(BF16)
        kc = k[c * CHUNK:(c + 1) * CHUNK]
        vc = v[c * CHUNK:(c + 1) * CHUNK]
        s_ref[...] = cdb_ref[...] * s + _chunk_kv((kc * kdb_ref[...]).astype(BF16), vc)


def _kv_states(x, mod, w_kv, kdb, cdb, sb0):
    B, L, _ = x.shape
    tb = TB_KV
    nb = L // tb
    nch = tb // CHUNK
    return pl.pallas_call(
        _kv_kernel,
        grid=(B, nb),
        in_specs=[pl.BlockSpec((1, tb, D_MODEL), lambda b, j: (b, nb - 1 - j, 0)),
                  _mod_spec(0), _mod_spec(1),
                  _const_spec((D_MODEL, KV_W)),
                  _const_spec((CHUNK, QK_W)),
                  _const_spec((QK_W, DV)),
                  pl.BlockSpec((1, QK_W, DV), lambda b, j: (b, 0, 0))],
        out_specs=[pl.BlockSpec((1, tb, D_MODEL), lambda b, j: (b, nb - 1 - j, 0)),
                   pl.BlockSpec((1, tb, KV_W), lambda b, j: (b, nb - 1 - j, 0)),
                   pl.BlockSpec((1, nch, QK_W, DV), lambda b, j: (b, nb - 1 - j, 0, 0))],
        out_shape=[jax.ShapeDtypeStruct((B, L, D_MODEL), BF16),
                   jax.ShapeDtypeStruct((B, L, KV_W), BF16),
                   jax.ShapeDtypeStruct((B, L // CHUNK, QK_W, DV), BF16)],
        scratch_shapes=[pltpu.VMEM((QK_W, DV), F32)],
        compiler_params=pltpu.CompilerParams(
            dimension_semantics=("arbitrary", "arbitrary"), vmem_limit_bytes=VMEM_LIMIT),
        name="kv_states",
    )(x, mod, mod, w_kv, kdb, cdb, sb0)


def _pool_features(pe, j, nb, tb, seq_len, poolw_ref, pscale_ref):
    n = pe.shape[0]
    t = j * tb + lax.broadcasted_iota(jnp.int32, (tb, POOL_GD), 0)
    outs = []
    for gi, w in enumerate(POOL_WINDOWS):
        half = w // 2
        a = pe[:, gi * POOL_GD:(gi + 1) * POOL_GD]
        centre = a[POOL_HALO:POOL_HALO + tb]
        s = a
        step = 1
        while step < w:
            s = s + pltpu.roll(s, n - step, axis=0)
            step *= 2
        s = pltpu.roll(s, half, axis=0)[POOL_HALO:POOL_HALO + tb]
        cnt = (jnp.minimum(t + half, seq_len) - jnp.maximum(t - half, 0)).astype(F32)
        diff = (s / cnt - centre).astype(BF16)
        outs.append(_dot(diff, poolw_ref[gi]))
    return jnp.concatenate(outs, axis=-1) * pscale_ref[...]


def _mixer_kernel(x_ref, u_ref, up_ref, un_ref, kv_ref, sb_ref, sf0_ref, g1_ref,
                  wr_ref, dm_ref, qdf_ref, qdb_ref, kdf_ref, cdf_ref,
                  poolw_ref, pscale_ref, wbr_ref, wbp_ref, wout_ref, lng_ref, lnb_ref,
                  o_ref, s_ref, r_ref, *, nb, seq_len):
    j = pl.program_id(1)
    tb = x_ref.shape[1]

    @pl.when(j == 0)
    def _():
        s_ref[...] = sf0_ref[0]

    x = x_ref[0]
    u = u_ref[0]

    q = _dot(u, wr_ref[:, COL_Q:COL_G])
    g = _dot(u, wr_ref[:, COL_G:COL_P])

    lane = lax.broadcasted_iota(jnp.int32, (CHUNK, 128), 1)
    zv = jnp.zeros((CHUNK, DV), BF16)
    zs = jnp.zeros((DK, DV), BF16)

    def block_diag_state(s):
        left = jnp.concatenate([s[:DK], zs], axis=0)
        right = jnp.concatenate([zs, s[DK:]], axis=0)
        return jnp.concatenate([left, right], axis=1)

    for c in range(tb // CHUNK):
        rows = slice(c * CHUNK, (c + 1) * CHUNK)
        qc = q[rows]
        qb16 = qc.astype(BF16)
        qf = (qc * qdf_ref[...]).astype(BF16)
        qb = (qc * qdb_ref[...]).astype(BF16)
        kc = kv_ref[0, rows, 0:QK_W].astype(F32)
        vc = kv_ref[0, rows, QK_W:KV_W]
        kd = (kc * kdf_ref[...]).astype(BF16)
        ys = []
        for p in range(PAIRS):
            ql = slice(p * 128, (p + 1) * 128)
            kp = kc[:, ql]
            k_lo = jnp.where(lane < DK, kp, 0.0).astype(BF16)
            k_hi = jnp.where(lane >= DK, kp, 0.0).astype(BF16)
            krhs = jnp.concatenate([k_lo, k_hi], axis=0)
            sc = (_dot_nt(qb16[:, ql], krhs) * dm_ref[p]).astype(BF16)
            vp = vc[:, p * 256:(p + 1) * 256]
            vrhs = jnp.concatenate([jnp.concatenate([vp[:, :DV], zv], axis=1),
                                    jnp.concatenate([zv, vp[:, DV:]], axis=1)], axis=0)
            y = _dot(sc, vrhs)
            s_f = s_ref[ql, :]
            srhs = jnp.concatenate([block_diag_state(s_f.astype(BF16)),
                                    block_diag_state(sb_ref[0, c, ql, :])], axis=0)
            qlhs = jnp.concatenate([qf[:, ql], qb[:, ql]], axis=1)
            y = y + _dot(qlhs, srhs)
            ys.append(y)
            s_ref[ql, :] = cdf_ref[ql, :] * s_f + _pair_diag(_dot_tn(kd[:, ql], vp))
        gc = g[rows]
        sg = gc * jax.nn.sigmoid(gc)
        for p in range(PAIRS):
            for hh in range(2):
                h = 2 * p + hh
                yh = ys[p][:, hh * DV:(hh + 1) * DV]
                mu = jnp.mean(yh, axis=-1, keepdims=True)
                yc = yh - mu
                var = jnp.mean(yc * yc, axis=-1, keepdims=True)
                yn = yc * lax.rsqrt(var + LN_EPS)
                r_ref[rows, h * DV:(h + 1) * DV] = (yn * sg[:, h * DV:(h + 1) * DV]).astype(BF16)

    ret = _dot(r_ref[...], wbr_ref[...])

    ph = _dot(jnp.concatenate([up_ref[0], un_ref[0]], axis=0), wr_ref[:, COL_P:COL_GA])
    p_prev = ph[U_HALO - POOL_HALO:U_HALO] * (j > 0).astype(F32)
    p_next = ph[U_HALO:U_HALO + POOL_HALO] * (j < nb - 1).astype(F32)
    pm = _dot(u, wr_ref[:, COL_P:COL_GA])
    pe = jnp.concatenate([p_prev, pm, p_next], axis=0)
    feat = _pool_features(pe, j, nb, tb, seq_len, poolw_ref, pscale_ref).astype(BF16)
    pool = _dot(feat, wbp_ref[...])

    ga = _dot(u, wr_ref[:, COL_GA:COL_GB])
    merged = jax.nn.sigmoid(ga) * ret
    gb = _dot(u, wr_ref[:, COL_GB:IN_W])
    merged = (merged + jax.nn.sigmoid(gb) * pool).astype(BF16)
    for r0 in range(0, tb, MIX_RB):
        rs = slice(r0, r0 + MIX_RB)
        z = ALPHA * x[rs] + g1_ref[0, 0] * _dot(merged[rs], wout_ref[...])
        o_ref[0, rs] = _ln(z) * lng_ref[...] + lnb_ref[...]


def _mixer(x, u, kv, sb, sf0, mod, w_rest, dm, qdf, qdb, kdf, cdf,
           pool_w, pool_scale, w_br, w_bp, w_out, ln_g, ln_b):
    B, L, _ = x.shape
    tb = TB_MIX
    nb = L // tb
    nch = tb // CHUNK
    hb = tb // U_HALO
    nh = L // U_HALO
    return pl.pallas_call(
        functools.partial(_mixer_kernel, nb=nb, seq_len=L),
        grid=(B, nb),
        in_specs=[pl.BlockSpec((1, tb, D_MODEL), lambda b, j: (b, j, 0)),
                  pl.BlockSpec((1, tb, D_MODEL), lambda b, j: (b, j, 0)),
                  pl.BlockSpec((1, U_HALO, D_MODEL), lambda b, j: (b, jnp.maximum(j * hb - 1, 0), 0)),
                  pl.BlockSpec((1, U_HALO, D_MODEL),
                               lambda b, j: (b, jnp.minimum((j + 1) * hb, nh - 1), 0)),
                  pl.BlockSpec((1, tb, KV_W), lambda b, j: (b, j, 0)),
                  pl.BlockSpec((1, nch, QK_W, DV), lambda b, j: (b, j, 0, 0)),
                  pl.BlockSpec((1, QK_W, DV), lambda b, j: (b, 0, 0)),
                  _mod_spec(2),
                  _const_spec((D_MODEL, IN_W)),
                  _const_spec((PAIRS, CHUNK, 256)),
                  _const_spec((CHUNK, QK_W)), _const_spec((CHUNK, QK_W)), _const_spec((CHUNK, QK_W)),
                  _const_spec((QK_W, DV)),
                  _const_spec((len(POOL_WINDOWS), POOL_GD, POOL_GD)),
                  _const_spec((1, POOL_W)),
                  _const_spec((V_W, D_MODEL)),
                  _const_spec((POOL_W, D_MODEL)),
                  _const_spec((D_MODEL, D_MODEL)),
                  _const_spec((1, D_MODEL)), _const_spec((1, D_MODEL))],
        out_specs=pl.BlockSpec((1, tb, D_MODEL), lambda b, j: (b, j, 0)),
        out_shape=jax.ShapeDtypeStruct((B, L, D_MODEL), F32),
        scratch_shapes=[pltpu.VMEM((QK_W, DV), F32), pltpu.VMEM((tb, V_W), BF16)],
        compiler_params=pltpu.CompilerParams(
            dimension_semantics=("arbitrary", "arbitrary"), vmem_limit_bytes=VMEM_LIMIT),
        name="mixer",
    )(x, u, u, u, kv, sb, sf0, mod, w_rest, dm, qdf, qdb, kdf, cdf,
      pool_w, pool_scale, w_br, w_bp, w_out, ln_g, ln_b)


def _ffn_kernel(x_ref, xn_ref, sh_ref, sc_ref, g2_ref, wup_ref, cw_ref, wdn_ref,
                lng_ref, lnb_ref, o_ref, u_ref, acc_ref, ha_ref, hb_ref, top_ref, *, nb):
    j = pl.program_id(1)
    tb = x_ref.shape[1]
    m = tb + GRID_W
    n = tb + 2 * GRID_W
    scale = 1.0 + sc_ref[0, 0]
    shift = sh_ref[0, 0]
    x = x_ref[0]

    def mod(v):
        return _ln(v) * scale + shift

    @pl.when(j == 0)
    def _():
        top_ref[...] = jnp.zeros_like(top_ref)

    u_ref[0:tb] = mod(x).astype(BF16)
    u_ref[tb:m] = (mod(xn_ref[0]) * (j < nb - 1).astype(F32)).astype(BF16)

    col = lax.broadcasted_iota(jnp.int32, (tb, FF_CW), 0) & (GRID_W - 1)
    has_left = col > 0
    has_right = col < GRID_W - 1

    pad = jnp.zeros((FF_PAD, FF_CW), F32)
    for h_ref in (ha_ref, hb_ref):
        for ab in range(2):
            h_ref[ab, 0:FF_PAD] = pad
            h_ref[ab, FF_PAD + n:FF_PAD + n + FF_PAD] = pad

    def lanes(c, half):
        return pl.ds(pl.multiple_of(half * D_FF + c * FF_CW, FF_CW), FF_CW)

    def up(c, h_ref):
        u = u_ref[...]
        for half in range(2):
            h = _dot(u, wup_ref[:, lanes(c, half)])
            h_ref[half, FF_PAD:FF_PAD + GRID_W] = top_ref[half, c]
            h_ref[half, FF_PAD + GRID_W:FF_PAD + n] = h
            top_ref[half, c] = h[tb - GRID_W:tb]

    def conv(h_ref, ab, cw):
        cols = []
        for dc in range(3):
            hs = h_ref[ab, FF_PAD + dc - 1:FF_PAD + dc - 1 + n].astype(BF16)
            g = None
            for dr in range(3):
                term = cw[3 * dr + dc:3 * dr + dc + 1] * hs[dr * GRID_W:dr * GRID_W + tb]
                g = term if g is None else g + term
            cols.append(g)
        zero = jnp.zeros_like(cols[1])
        return (cols[1] + cw[9:10]) + (jnp.where(has_left, cols[0], zero) + jnp.where(has_right, cols[2], zero))

    def down(c, h_ref):
        a = conv(h_ref, 0, cw_ref[:, lanes(c, 0)])
        b = conv(h_ref, 1, cw_ref[:, lanes(c, 1)])
        acc_ref[...] += _dot(_gelu_tanh(a.astype(F32)).astype(BF16) * b, wdn_ref[c])

    acc_ref[...] = jnp.zeros_like(acc_ref)
    up(0, ha_ref)

    def body(i, carry):
        c = 2 * i
        up(c + 1, hb_ref)
        down(c, ha_ref)
        up(c + 2, ha_ref)
        down(c + 1, hb_ref)
        return carry

    lax.fori_loop(0, FF_NC // 2, body, 0)
    down(FF_NC - 1, ha_ref)
    z = ALPHA * x + g2_ref[0, 0] * acc_ref[...]
    o_ref[0] = _ln(z) * lng_ref[...] + lnb_ref[...]


def _ffn(x, mod, w_up, conv_wb, w_down, ln_g, ln_b):
    B, L, _ = x.shape
    tb = TB_FFN
    nb = L // tb
    hb = tb // GRID_W
    nh = L // GRID_W
    return pl.pallas_call(
        functools.partial(_ffn_kernel, nb=nb),
        grid=(B, nb),
        in_specs=[pl.BlockSpec((1, tb, D_MODEL), lambda b, j: (b, j, 0)),
                  pl.BlockSpec((1, GRID_W, D_MODEL),
                               lambda b, j: (b, jnp.minimum((j + 1) * hb, nh - 1), 0)),
                  _mod_spec(3), _mod_spec(4), _mod_spec(5),
                  _const_spec((D_MODEL, 2 * D_FF)),
                  _const_spec((16, 2 * D_FF)),
                  _const_spec((FF_NC, FF_CW, D_MODEL)),
                  _const_spec((1, D_MODEL)), _const_spec((1, D_MODEL))],
        out_specs=pl.BlockSpec((1, tb, D_MODEL), lambda b, j: (b, j, 0)),
        out_shape=jax.ShapeDtypeStruct((B, L, D_MODEL), F32),
        scratch_shapes=[pltpu.VMEM((tb + GRID_W, D_MODEL), BF16),
                        pltpu.VMEM((tb, D_MODEL), F32),
                        pltpu.VMEM((2, tb + 2 * GRID_W + 2 * FF_PAD, FF_CW), F32),
                        pltpu.VMEM((2, tb + 2 * GRID_W + 2 * FF_PAD, FF_CW), F32),
                        pltpu.VMEM((2, FF_NC, GRID_W, FF_CW), F32)],
        compiler_params=pltpu.CompilerParams(
            dimension_semantics=("arbitrary", "arbitrary"), vmem_limit_bytes=VMEM_LIMIT),
        name="conv_ffn",
    )(x, x, mod, mod, mod, w_up, conv_wb, w_down, ln_g, ln_b)


def _decay_tables(ret_decay_logit):
    lg = jax.nn.log_sigmoid(ret_decay_logit.astype(F32))
    pos = jnp.arange(CHUNK, dtype=F32)
    diff = pos[:, None] - pos[None, :]
    d_f = jnp.where(diff[None] >= 0, jnp.exp(jnp.maximum(diff, 0.0)[None] * lg[0][:, None, None]), 0.0)
    d_b = jnp.where(diff[None] <= 0, jnp.exp(jnp.maximum(-diff, 0.0)[None] * lg[1][:, None, None]), 0.0)
    dm = (d_f + d_b).reshape(PAIRS, 2, CHUNK, CHUNK).transpose(0, 2, 1, 3).reshape(PAIRS, CHUNK, 2 * CHUNK)

    def lanes(t):
        return jnp.repeat(t, DK, axis=1)

    qdf = lanes(jnp.exp((pos + 1.0)[:, None] * lg[0][None, :]))
    qdb = lanes(jnp.exp((CHUNK - pos)[:, None] * lg[1][None, :]))
    kdf = lanes(jnp.exp((CHUNK - 1.0 - pos)[:, None] * lg[0][None, :]))
    kdb = lanes(jnp.exp(pos[:, None] * lg[1][None, :]))

    def rows(t):
        return jnp.broadcast_to(jnp.repeat(t, DK)[:, None], (QK_W, DV))

    cdf = rows(jnp.exp(CHUNK * lg[0]))
    cdb = rows(jnp.exp(CHUNK * lg[1]))
    return dm, qdf, qdb, kdf, kdb, cdf, cdb


def kernel(x, c, ctx, c_ctx, w_ada, b_ada, w_in, ret_decay_logit, pool_w, pool_scale, w_branch_ret,
           w_branch_pool, w_out, ln1_g, ln1_b, w_up, conv_w, conv_b, w_down, ln2_g, ln2_b):
    B = x.shape[0]
    D = D_MODEL
    assert w_ada.shape[0] == 1, "single-layer stack"

    cc = jnp.concatenate([c, c_ctx[None, :], jnp.zeros((MOD_ROWS - B - 1, D), F32)], axis=0)
    mod = _adaln(cc, w_ada[0], b_ada[0][None, :]).reshape(MOD_ROWS, N_MOD, 1, D)

    dm, qdf, qdb, kdf, kdb, cdf, cdb = _decay_tables(ret_decay_logit[0])

    w_in_b = w_in[0].astype(BF16)

    s_f, s_b = _ctx_states(ctx, mod, w_in_b, kdf, kdb, cdf, cdb)
    u, kv, sb = _kv_states(x, mod, w_in_b, kdb, cdb, s_b)
    x1 = _mixer(x, u, kv, sb, s_f, mod, w_in_b, dm, qdf, qdb, kdf, cdf,
                pool_w[0].astype(BF16), pool_scale[0][None, :],
                w_branch_ret[0].astype(BF16), w_branch_pool[0].astype(BF16), w_out[0].astype(BF16),
                ln1_g[0][None, :], ln1_b[0][None, :])

    conv_wb = jnp.concatenate([conv_w[0].reshape(9, 2 * D_FF), conv_b[0][None, :],
                               jnp.zeros((6, 2 * D_FF), F32)], axis=0).astype(BF16)
    w_down_c = w_down[0].astype(BF16).reshape(FF_NC, FF_CW, D)
    return _ffn(x1, mod, w_up[0].astype(BF16), conv_wb, w_down_c,
                ln2_g[0][None, :], ln2_b[0][None, :])
```

```python
import functools

import jax
import jax.numpy as jnp
import numpy as np
from jax import lax
from jax.experimental import pallas as pl
from jax.experimental.pallas import tpu as pltpu

F32 = jnp.float32
BF16 = jnp.bfloat16

D_MODEL = 1024
GRID_W = 64
HEADS = 8
DK = 64
DV = 128
QK_W = HEADS * DK
V_W = HEADS * DV
KV_W = QK_W + V_W
CHUNK = 128
PAIRS = HEADS // 2
POOL_WINDOWS = (2, 4, 8, 16)
POOL_GD = 128
POOL_W = 512
COL_Q = KV_W
COL_G = COL_Q + QK_W
COL_P = COL_G + V_W
COL_GA = COL_P + POOL_W
COL_GB = COL_GA + D_MODEL
IN_W = COL_GB + D_MODEL
D_FF = 2816
FF_CW = 256
FF_NC = D_FF // FF_CW
FF_PAD = 8
N_MOD = 6
LN_EPS = 1e-6
ALPHA = 2.0 ** 0.25
POOL_HALO = 8
U_HALO = 16
MOD_ROWS = 8

VMEM_LIMIT = 60 * 1024 * 1024

MIX_RB = 256

TB_KV = 1024
TB_MIX = 1024
TB_FFN = 1024


def _dot(a, b):
    return jnp.dot(a, b, preferred_element_type=F32)


def _dot_nt(a, b):
    return lax.dot_general(a, b, (((1,), (1,)), ((), ())), preferred_element_type=F32)


def _dot_tn(a, b):
    return lax.dot_general(a, b, (((0,), (0,)), ((), ())), preferred_element_type=F32)


def _ln(x):
    mu = jnp.mean(x, axis=-1, keepdims=True)
    xc = x - mu
    var = jnp.mean(xc * xc, axis=-1, keepdims=True)
    return xc * lax.rsqrt(var + LN_EPS)


def _gelu_tanh(x):
    c = float(np.sqrt(2.0 / np.pi))
    half = 0.5 * x
    return half + half * jnp.tanh(x * (c + (c * 0.044715) * (x * x)))


def _mod_spec(k, row=None):
    if row is None:
        return pl.BlockSpec((1, 1, 1, D_MODEL), lambda b, *_: (b, k, 0, 0))
    return pl.BlockSpec((1, 1, 1, D_MODEL), lambda *_: (row, k, 0, 0))


def _const_spec(shape):
    nd = len(shape)
    return pl.BlockSpec(shape, lambda *_: (0,) * nd, pipeline_mode=pl.Buffered(1))


def _pair_diag(r):
    row = lax.broadcasted_iota(jnp.int32, (CHUNK, DV), 0)
    return jnp.where(row < DK, r[:, :DV], r[:, DV:])


def _chunk_kv(kd, v):
    outs = []
    for p in range(PAIRS):
        r = _dot_tn(kd[:, p * 128:(p + 1) * 128], v[:, p * 256:(p + 1) * 256])
        outs.append(_pair_diag(r))
    return jnp.concatenate(outs, axis=0)


def _adaln_kernel(c_ref, w_ref, b_ref, o_ref):
    c = c_ref[...]
    s = c * jax.nn.sigmoid(c)
    o_ref[...] = _dot(s, w_ref[...]) + b_ref[...]


def _adaln(cc, w_ada, b_ada):
    n = w_ada.shape[1]
    bn = 1536
    return pl.pallas_call(
        _adaln_kernel,
        grid=(n // bn,),
        in_specs=[pl.BlockSpec((MOD_ROWS, D_MODEL), lambda i: (0, 0)),
                  pl.BlockSpec((D_MODEL, bn), lambda i: (0, i)),
                  pl.BlockSpec((1, bn), lambda i: (0, i))],
        out_specs=pl.BlockSpec((MOD_ROWS, bn), lambda i: (0, i)),
        out_shape=jax.ShapeDtypeStruct((MOD_ROWS, n), F32),
        compiler_params=pltpu.CompilerParams(vmem_limit_bytes=VMEM_LIMIT),
        name="adaln",
    )(cc, w_ada, b_ada)


def _ctx_kernel(ctx_ref, sh_ref, sc_ref, wkv_ref, kdf_ref, kdb_ref, cdf_ref, cdb_ref, sf_ref, sb_ref):
    x = ctx_ref[0]
    u = (_ln(x) * (1.0 + sc_ref[0, 0]) + sh_ref[0, 0]).astype(BF16)
    kv = _dot(u, wkv_ref[...])
    k = kv[:, :QK_W] * (DK ** -0.5)
    v = kv[:, QK_W:].astype(BF16)
    n = x.shape[0] // CHUNK
    sf = jnp.zeros((QK_W, DV), F32)
    for c in range(n):
        kc = k[c * CHUNK:(c + 1) * CHUNK]
        vc = v[c * CHUNK:(c + 1) * CHUNK]
        sf = cdf_ref[...] * sf + _chunk_kv((kc * kdf_ref[...]).astype(BF16), vc)
    sb = jnp.zeros((QK_W, DV), F32)
    for c in reversed(range(n)):
        kc = k[c * CHUNK:(c + 1) * CHUNK]
        vc = v[c * CHUNK:(c + 1) * CHUNK]
        sb = cdb_ref[...] * sb + _chunk_kv((kc * kdb_ref[...]).astype(BF16), vc)
    sf_ref[0] = sf
    sb_ref[0] = sb


def _ctx_states(ctx, mod, w_kv, kdf, kdb, cdf, cdb):
    B, Lc, _ = ctx.shape
    st = jax.ShapeDtypeStruct((B, QK_W, DV), F32)
    return pl.pallas_call(
        _ctx_kernel,
        grid=(B,),
        in_specs=[pl.BlockSpec((1, Lc, D_MODEL), lambda b: (b, 0, 0)),
                  _mod_spec(0, row=B), _mod_spec(1, row=B),
                  _const_spec((D_MODEL, KV_W)),
                  _const_spec((CHUNK, QK_W)), _const_spec((CHUNK, QK_W)),
                  _const_spec((QK_W, DV)), _const_spec((QK_W, DV))],
        out_specs=[pl.BlockSpec((1, QK_W, DV), lambda b: (b, 0, 0)),
                   pl.BlockSpec((1, QK_W, DV), lambda b: (b, 0, 0))],
        out_shape=[st, st],
        compiler_params=pltpu.CompilerParams(vmem_limit_bytes=VMEM_LIMIT),
        name="ctx_states",
    )(ctx, mod, mod, w_kv, kdf, kdb, cdf, cdb)


def _kv_kernel(x_ref, sh_ref, sc_ref, wkv_ref, kdb_ref, cdb_ref, sb0_ref, *rest, n_cast):
    cast_in, (u_ref, kv_ref, sb_ref), cast_out, (s_ref,) = (
        rest[:n_cast], rest[n_cast:n_cast + 3], rest[n_cast + 3:2 * n_cast + 3], rest[2 * n_cast + 3:])

    @pl.when(pl.program_id(1) == 0)
    def _():
        s_ref[...] = sb0_ref[0]

    for src, dst in zip(cast_in, cast_out):
        dst[...] = src[...].astype(BF16)

    x = x_ref[0]
    u = (_ln(x) * (1.0 + sc_ref[0, 0]) + sh_ref[0, 0]).astype(BF16)
    u_ref[0] = u
    kv = _dot(u, wkv_ref[...])
    k = kv[:, :QK_W] * (DK ** -0.5)
    v = kv[:, QK_W:].astype(BF16)
    kv_ref[0, :, :QK_W] = k.astype(BF16)
    kv_ref[0, :, QK_W:] = v
    n = x.shape[0] // CHUNK
    for c in reversed(range(n)):
        s = s_ref[...]
        sb_ref[0, c] = s.astype(BF16)
        kc = k[c * CHUNK:(c + 1) * CHUNK]
        vc = v[c * CHUNK:(c + 1) * CHUNK]
        s_ref[...] = cdb_ref[...] * s + _chunk_kv((kc * kdb_ref[...]).astype(BF16), vc)


def _kv_states(x, mod, w_kv, kdb, cdb, sb0, later_weights):
    B, L, _ = x.shape
    tb = TB_KV
    nb = L // tb
    nch = tb // CHUNK
    steps = B * nb
    slab_specs = [pl.BlockSpec((w.shape[0] // steps, w.shape[1]), lambda b, j: (b * nb + j, 0))
                  for w in later_weights]
    for w in later_weights:
        assert w.shape[0] % (16 * steps) == 0, "row slabs must be whole bf16 tiles"
    return pl.pallas_call(
        functools.partial(_kv_kernel, n_cast=len(later_weights)),
        grid=(B, nb),
        in_specs=[pl.BlockSpec((1, tb, D_MODEL), lambda b, j: (b, nb - 1 - j, 0)),
                  _mod_spec(0), _mod_spec(1),
                  _const_spec((D_MODEL, KV_W)),
                  _const_spec((CHUNK, QK_W)),
                  _const_spec((QK_W, DV)),
                  pl.BlockSpec((1, QK_W, DV), lambda b, j: (b, 0, 0))] + slab_specs,
        out_specs=[pl.BlockSpec((1, tb, D_MODEL), lambda b, j: (b, nb - 1 - j, 0)),
                   pl.BlockSpec((1, tb, KV_W), lambda b, j: (b, nb - 1 - j, 0)),
                   pl.BlockSpec((1, nch, QK_W, DV), lambda b, j: (b, nb - 1 - j, 0, 0))] + slab_specs,
        out_shape=[jax.ShapeDtypeStruct((B, L, D_MODEL), BF16),
                   jax.ShapeDtypeStruct((B, L, KV_W), BF16),
                   jax.ShapeDtypeStruct((B, L // CHUNK, QK_W, DV), BF16)]
        + [jax.ShapeDtypeStruct(w.shape, BF16) for w in later_weights],
        scratch_shapes=[pltpu.VMEM((QK_W, DV), F32)],
        compiler_params=pltpu.CompilerParams(
            dimension_semantics=("arbitrary", "arbitrary"), vmem_limit_bytes=VMEM_LIMIT),
        name="kv_states",
    )(x, mod, mod, w_kv, kdb, cdb, sb0, *later_weights)


def _pool_features(pe, j, nb, tb, seq_len, poolw_ref, pscale_ref):
    n = pe.shape[0]
    t = j * tb + lax.broadcasted_iota(jnp.int32, (tb, POOL_GD), 0)
    outs = []
    for gi, w in enumerate(POOL_WINDOWS):
        half = w // 2
        a = pe[:, gi * POOL_GD:(gi + 1) * POOL_GD]
        centre = a[POOL_HALO:POOL_HALO + tb]
        s = a
        step = 1
        while step < w:
            s = s + pltpu.roll(s, n - step, axis=0)
            step *= 2
        s = pltpu.roll(s, half, axis=0)[POOL_HALO:POOL_HALO + tb]
        cnt = (jnp.minimum(t + half, seq_len) - jnp.maximum(t - half, 0)).astype(F32)
        diff = (s / cnt - centre).astype(BF16)
        outs.append(_dot(diff, poolw_ref[gi]))
    return jnp.concatenate(outs, axis=-1) * pscale_ref[...]


def _mixer_kernel(x_ref, u_ref, up_ref, un_ref, kv_ref, sb_ref, sf0_ref, g1_ref,
                  wr_ref, dm_ref, qdf_ref, qdb_ref, kdf_ref, cdf_ref,
                  poolw_ref, pscale_ref, wbr_ref, wbp_ref, wout_ref, lng_ref, lnb_ref,
                  o_ref, s_ref, r_ref, *, nb, seq_len):
    j = pl.program_id(1)
    tb = x_ref.shape[1]

    @pl.when(j == 0)
    def _():
        s_ref[...] = sf0_ref[0]

    x = x_ref[0]
    u = u_ref[0]

    q = _dot(u, wr_ref[:, COL_Q:COL_G])
    g = _dot(u, wr_ref[:, COL_G:COL_P])

    lane = lax.broadcasted_iota(jnp.int32, (CHUNK, 128), 1)
    zv = jnp.zeros((CHUNK, DV), BF16)
    zs = jnp.zeros((DK, DV), BF16)

    def block_diag_state(s):
        left = jnp.concatenate([s[:DK], zs], axis=0)
        right = jnp.concatenate([zs, s[DK:]], axis=0)
        return jnp.concatenate([left, right], axis=1)

    for c in range(tb // CHUNK):
        rows = slice(c * CHUNK, (c + 1) * CHUNK)
        qc = q[rows]
        qb16 = qc.astype(BF16)
        qf = (qc * qdf_ref[...]).astype(BF16)
        qb = (qc * qdb_ref[...]).astype(BF16)
        kc = kv_ref[0, rows, 0:QK_W].astype(F32)
        vc = kv_ref[0, rows, QK_W:KV_W]
        kd = (kc * kdf_ref[...]).astype(BF16)
        ys = []
        for p in range(PAIRS):
            ql = slice(p * 128, (p + 1) * 128)
            kp = kc[:, ql]
            k_lo = jnp.where(lane < DK, kp, 0.0).astype(BF16)
            k_hi = jnp.where(lane >= DK, kp, 0.0).astype(BF16)
            krhs = jnp.concatenate([k_lo, k_hi], axis=0)
            sc = (_dot_nt(qb16[:, ql], krhs) * dm_ref[p]).astype(BF16)
            vp = vc[:, p * 256:(p + 1) * 256]
            vrhs = jnp.concatenate([jnp.concatenate([vp[:, :DV], zv], axis=1),
                                    jnp.concatenate([zv, vp[:, DV:]], axis=1)], axis=0)
            y = _dot(sc, vrhs)
            s_f = s_ref[ql, :]
            srhs = jnp.concatenate([block_diag_state(s_f.astype(BF16)),
                                    block_diag_state(sb_ref[0, c, ql, :])], axis=0)
            qlhs = jnp.concatenate([qf[:, ql], qb[:, ql]], axis=1)
            y = y + _dot(qlhs, srhs)
            ys.append(y)
            s_ref[ql, :] = cdf_ref[ql, :] * s_f + _pair_diag(_dot_tn(kd[:, ql], vp))
        gc = g[rows]
        sg = gc * jax.nn.sigmoid(gc)
        for p in range(PAIRS):
            for hh in range(2):
                h = 2 * p + hh
                yh = ys[p][:, hh * DV:(hh + 1) * DV]
                mu = jnp.mean(yh, axis=-1, keepdims=True)
                yc = yh - mu
                var = jnp.mean(yc * yc, axis=-1, keepdims=True)
                yn = yc * lax.rsqrt(var + LN_EPS)
                r_ref[rows, h * DV:(h + 1) * DV] = (yn * sg[:, h * DV:(h + 1) * DV]).astype(BF16)

    ret = _dot(r_ref[...], wbr_ref[...])

    ph = _dot(jnp.concatenate([up_ref[0], un_ref[0]], axis=0), wr_ref[:, COL_P:COL_GA])
    p_prev = ph[U_HALO - POOL_HALO:U_HALO] * (j > 0).astype(F32)
    p_next = ph[U_HALO:U_HALO + POOL_HALO] * (j < nb - 1).astype(F32)
    pm = _dot(u, wr_ref[:, COL_P:COL_GA])
    pe = jnp.concatenate([p_prev, pm, p_next], axis=0)
    feat = _pool_features(pe, j, nb, tb, seq_len, poolw_ref, pscale_ref).astype(BF16)
    pool = _dot(feat, wbp_ref[...])

    ga = _dot(u, wr_ref[:, COL_GA:COL_GB])
    merged = jax.nn.sigmoid(ga) * ret
    gb = _dot(u, wr_ref[:, COL_GB:IN_W])
    merged = (merged + jax.nn.sigmoid(gb) * pool).astype(BF16)
    for r0 in range(0, tb, MIX_RB):
        rs = slice(r0, r0 + MIX_RB)
        z = ALPHA * x[rs] + g1_ref[0, 0] * _dot(merged[rs], wout_ref[...])
        o_ref[0, rs] = _ln(z) * lng_ref[...] + lnb_ref[...]


def _mixer(x, u, kv, sb, sf0, mod, w_rest, dm, qdf, qdb, kdf, cdf,
           pool_w, pool_scale, w_br, w_bp, w_out, ln_g, ln_b):
    B, L, _ = x.shape
    tb = TB_MIX
    nb = L // tb
    nch = tb // CHUNK
    hb = tb // U_HALO
    nh = L // U_HALO
    return pl.pallas_call(
        functools.partial(_mixer_kernel, nb=nb, seq_len=L),
        grid=(B, nb),
        in_specs=[pl.BlockSpec((1, tb, D_MODEL), lambda b, j: (b, j, 0)),
                  pl.BlockSpec((1, tb, D_MODEL), lambda b, j: (b, j, 0)),
                  pl.BlockSpec((1, U_HALO, D_MODEL), lambda b, j: (b, jnp.maximum(j * hb - 1, 0), 0)),
                  pl.BlockSpec((1, U_HALO, D_MODEL),
                               lambda b, j: (b, jnp.minimum((j + 1) * hb, nh - 1), 0)),
                  pl.BlockSpec((1, tb, KV_W), lambda b, j: (b, j, 0)),
                  pl.BlockSpec((1, nch, QK_W, DV), lambda b, j: (b, j, 0, 0)),
                  pl.BlockSpec((1, QK_W, DV), lambda b, j: (b, 0, 0)),
                  _mod_spec(2),
                  _const_spec((D_MODEL, IN_W)),
                  _const_spec((PAIRS, CHUNK, 256)),
                  _const_spec((CHUNK, QK_W)), _const_spec((CHUNK, QK_W)), _const_spec((CHUNK, QK_W)),
                  _const_spec((QK_W, DV)),
                  _const_spec((len(POOL_WINDOWS), POOL_GD, POOL_GD)),
                  _const_spec((1, POOL_W)),
                  _const_spec((V_W, D_MODEL)),
                  _const_spec((POOL_W, D_MODEL)),
                  _const_spec((D_MODEL, D_MODEL)),
                  _const_spec((1, D_MODEL)), _const_spec((1, D_MODEL))],
        out_specs=pl.BlockSpec((1, tb, D_MODEL), lambda b, j: (b, j, 0)),
        out_shape=jax.ShapeDtypeStruct((B, L, D_MODEL), F32),
        scratch_shapes=[pltpu.VMEM((QK_W, DV), F32), pltpu.VMEM((tb, V_W), BF16)],
        compiler_params=pltpu.CompilerParams(
            dimension_semantics=("arbitrary", "arbitrary"), vmem_limit_bytes=VMEM_LIMIT),
        name="mixer",
    )(x, u, u, u, kv, sb, sf0, mod, w_rest, dm, qdf, qdb, kdf, cdf,
      pool_w, pool_scale, w_br, w_bp, w_out, ln_g, ln_b)


def _ffn_kernel(x_ref, xn_ref, sh_ref, sc_ref, g2_ref, wup_ref, cw_ref, wdn_ref,
                lng_ref, lnb_ref, o_ref, u_ref, acc_ref, ha_ref, hb_ref, top_ref, *, nb):
    j = pl.program_id(1)
    tb = x_ref.shape[1]
    m = tb + GRID_W
    n = tb + 2 * GRID_W
    scale = 1.0 + sc_ref[0, 0]
    shift = sh_ref[0, 0]
    x = x_ref[0]

    def mod(v):
        return _ln(v) * scale + shift

    @pl.when(j == 0)
    def _():
        top_ref[...] = jnp.zeros_like(top_ref)

    u_ref[0:tb] = mod(x).astype(BF16)
    u_ref[tb:m] = (mod(xn_ref[0]) * (j < nb - 1).astype(F32)).astype(BF16)

    col = lax.broadcasted_iota(jnp.int32, (tb, FF_CW), 0) & (GRID_W - 1)
    has_left = col > 0
    has_right = col < GRID_W - 1

    pad = jnp.zeros((FF_PAD, FF_CW), F32)
    for h_ref in (ha_ref, hb_ref):
        for ab in range(2):
            h_ref[ab, 0:FF_PAD] = pad
            h_ref[ab, FF_PAD + n:FF_PAD + n + FF_PAD] = pad

    def lanes(c, half):
        return pl.ds(pl.multiple_of(half * D_FF + c * FF_CW, FF_CW), FF_CW)

    def up(c, h_ref):
        u = u_ref[...]
        for half in range(2):
            h = _dot(u, wup_ref[:, lanes(c, half)])
            h_ref[half, FF_PAD:FF_PAD + GRID_W] = top_ref[half, c]
            h_ref[half, FF_PAD + GRID_W:FF_PAD + n] = h
            top_ref[half, c] = h[tb - GRID_W:tb]

    def conv(h_ref, ab, cw):
        cols = []
        for dc in range(3):
            hs = h_ref[ab, FF_PAD + dc - 1:FF_PAD + dc - 1 + n].astype(BF16)
            g = None
            for dr in range(3):
                term = cw[3 * dr + dc:3 * dr + dc + 1] * hs[dr * GRID_W:dr * GRID_W + tb]
                g = term if g is None else g + term
            cols.append(g)
        zero = jnp.zeros_like(cols[1])
        return (cols[1] + cw[9:10]) + (jnp.where(has_left, cols[0], zero) + jnp.where(has_right, cols[2], zero))

    def down(c, h_ref):
        a = conv(h_ref, 0, cw_ref[:, lanes(c, 0)])
        b = conv(h_ref, 1, cw_ref[:, lanes(c, 1)])
        acc_ref[...] += _dot(_gelu_tanh(a.astype(F32)).astype(BF16) * b, wdn_ref[c])

    acc_ref[...] = jnp.zeros_like(acc_ref)
    up(0, ha_ref)

    def body(i, carry):
        c = 2 * i
        up(c + 1, hb_ref)
        down(c, ha_ref)
        up(c + 2, ha_ref)
        down(c + 1, hb_ref)
        return carry

    lax.fori_loop(0, FF_NC // 2, body, 0)
    down(FF_NC - 1, ha_ref)
    z = ALPHA * x + g2_ref[0, 0] * acc_ref[...]
    o_ref[0] = _ln(z) * lng_ref[...] + lnb_ref[...]


def _ffn(x, mod, w_up, conv_wb, w_down, ln_g, ln_b):
    B, L, _ = x.shape
    tb = TB_FFN
    nb = L // tb
    hb = tb // GRID_W
    nh = L // GRID_W
    return pl.pallas_call(
        functools.partial(_ffn_kernel, nb=nb),
        grid=(B, nb),
        in_specs=[pl.BlockSpec((1, tb, D_MODEL), lambda b, j: (b, j, 0)),
                  pl.BlockSpec((1, GRID_W, D_MODEL),
                               lambda b, j: (b, jnp.minimum((j + 1) * hb, nh - 1), 0)),
                  _mod_spec(3), _mod_spec(4), _mod_spec(5),
                  _const_spec((D_MODEL, 2 * D_FF)),
                  _const_spec((16, 2 * D_FF)),
                  _const_spec((FF_NC, FF_CW, D_MODEL)),
                  _const_spec((1, D_MODEL)), _const_spec((1, D_MODEL))],
        out_specs=pl.BlockSpec((1, tb, D_MODEL), lambda b, j: (b, j, 0)),
        out_shape=jax.ShapeDtypeStruct((B, L, D_MODEL), F32),
        scratch_shapes=[pltpu.VMEM((tb + GRID_W, D_MODEL), BF16),
                        pltpu.VMEM((tb, D_MODEL), F32),
                        pltpu.VMEM((2, tb + 2 * GRID_W + 2 * FF_PAD, FF_CW), F32),
                        pltpu.VMEM((2, tb + 2 * GRID_W + 2 * FF_PAD, FF_CW), F32),
                        pltpu.VMEM((2, FF_NC, GRID_W, FF_CW), F32)],
        compiler_params=pltpu.CompilerParams(
            dimension_semantics=("arbitrary", "arbitrary"), vmem_limit_bytes=VMEM_LIMIT),
        name="conv_ffn",
    )(x, x, mod, mod, mod, w_up, conv_wb, w_down, ln_g, ln_b)


def _decay_tables(ret_decay_logit):
    lg = jax.nn.log_sigmoid(ret_decay_logit.astype(F32))
    pos = jnp.arange(CHUNK, dtype=F32)
    diff = pos[:, None] - pos[None, :]
    d_f = jnp.where(diff[None] >= 0, jnp.exp(jnp.maximum(diff, 0.0)[None] * lg[0][:, None, None]), 0.0)
    d_b = jnp.where(diff[None] <= 0, jnp.exp(jnp.maximum(-diff, 0.0)[None] * lg[1][:, None, None]), 0.0)
    dm = (d_f + d_b).reshape(PAIRS, 2, CHUNK, CHUNK).transpose(0, 2, 1, 3).reshape(PAIRS, CHUNK, 2 * CHUNK)

    def lanes(t):
        return jnp.repeat(t, DK, axis=1)

    qdf = lanes(jnp.exp((pos + 1.0)[:, None] * lg[0][None, :]))
    qdb = lanes(jnp.exp((CHUNK - pos)[:, None] * lg[1][None, :]))
    kdf = lanes(jnp.exp((CHUNK - 1.0 - pos)[:, None] * lg[0][None, :]))
    kdb = lanes(jnp.exp(pos[:, None] * lg[1][None, :]))

    def rows(t):
        return jnp.broadcast_to(jnp.repeat(t, DK)[:, None], (QK_W, DV))

    cdf = rows(jnp.exp(CHUNK * lg[0]))
    cdb = rows(jnp.exp(CHUNK * lg[1]))
    return dm, qdf, qdb, kdf, kdb, cdf, cdb


def kernel(x, c, ctx, c_ctx, w_ada, b_ada, w_in, ret_decay_logit, pool_w, pool_scale, w_branch_ret,
           w_branch_pool, w_out, ln1_g, ln1_b, w_up, conv_w, conv_b, w_down, ln2_g, ln2_b):
    B = x.shape[0]
    D = D_MODEL
    assert w_ada.shape[0] == 1, "single-layer stack"

    cc = jnp.concatenate([c, c_ctx[None, :], jnp.zeros((MOD_ROWS - B - 1, D), F32)], axis=0)
    mod = _adaln(cc, w_ada[0], b_ada[0][None, :]).reshape(MOD_ROWS, N_MOD, 1, D)

    dm, qdf, qdb, kdf, kdb, cdf, cdb = _decay_tables(ret_decay_logit[0])

    w_in_b = w_in[0].astype(BF16)

    s_f, s_b = _ctx_states(ctx, mod, w_in_b, kdf, kdb, cdf, cdb)
    u, kv, sb, w_up_b, w_down_b, w_br_b, w_bp_b, w_out_b = _kv_states(
        x, mod, w_in_b, kdb, cdb, s_b, [w_up[0], w_down[0], w_branch_ret[0], w_branch_pool[0], w_out[0]])
    x1 = _mixer(x, u, kv, sb, s_f, mod, w_in_b, dm, qdf, qdb, kdf, cdf,
                pool_w[0].astype(BF16), pool_scale[0][None, :], w_br_b, w_bp_b, w_out_b,
                ln1_g[0][None, :], ln1_b[0][None, :])

    conv_wb = jnp.concatenate([conv_w[0].reshape(9, 2 * D_FF), conv_b[0][None, :],
                               jnp.zeros((6, 2 * D_FF), F32)], axis=0).astype(BF16)
    return _ffn(x1, mod, w_up_b, conv_wb, w_down_b.reshape(FF_NC, FF_CW, D),
                ln2_g[0][None, :], ln2_b[0][None, :])
```

```python
import functools

import jax
import jax.numpy as jnp
import numpy as np
from jax import lax
from jax.experimental import pallas as pl
from jax.experimental.pallas import tpu as pltpu

F32 = jnp.float32
BF16 = jnp.bfloat16

D_MODEL = 1024
GRID_W = 64
HEADS = 8
DK = 64
DV = 128
QK_W = HEADS * DK
V_W = HEADS * DV
KV_W = QK_W + V_W
CHUNK = 128
PAIRS = HEADS // 2
POOL_WINDOWS = (2, 4, 8, 16)
POOL_GD = 128
POOL_W = 512
COL_Q = KV_W
COL_G = COL_Q + QK_W
COL_P = COL_G + V_W
COL_GA = COL_P + POOL_W
COL_GB = COL_GA + D_MODEL
IN_W = COL_GB + D_MODEL
D_FF = 2816
FF_CW = 256
FF_NC = D_FF // FF_CW
FF_PAD = 8
N_MOD = 6
LN_EPS = 1e-6
ALPHA = 2.0 ** 0.25
POOL_HALO = 8
U_HALO = 16
MOD_ROWS = 8

VMEM_LIMIT = 60 * 1024 * 1024

MIX_RB = 256

TB_KV = 1024
TB_MIX = 1024
TB_FFN = 1024


def _dot(a, b):
    return jnp.dot(a, b, preferred_element_type=F32)


def _dot_nt(a, b):
    return lax.dot_general(a, b, (((1,), (1,)), ((), ())), preferred_element_type=F32)


def _dot_tn(a, b):
    return lax.dot_general(a, b, (((0,), (0,)), ((), ())), preferred_element_type=F32)


def _ln(x):
    mu = jnp.mean(x, axis=-1, keepdims=True)
    xc = x - mu
    var = jnp.mean(xc * xc, axis=-1, keepdims=True)
    return xc * lax.rsqrt(var + LN_EPS)


def _gelu_tanh(x):
    c = float(np.sqrt(2.0 / np.pi))
    half = 0.5 * x
    return half + half * jnp.tanh(x * (c + (c * 0.044715) * (x * x)))


def _mod_spec(k, row=None):
    if row is None:
        return pl.BlockSpec((1, 1, 1, D_MODEL), lambda b, *_: (b, k, 0, 0))
    return pl.BlockSpec((1, 1, 1, D_MODEL), lambda *_: (row, k, 0, 0))


def _const_spec(shape):
    nd = len(shape)
    return pl.BlockSpec(shape, lambda *_: (0,) * nd, pipeline_mode=pl.Buffered(1))


def _pair_diag(r):
    row = lax.broadcasted_iota(jnp.int32, (CHUNK, DV), 0)
    return jnp.where(row < DK, r[:, :DV], r[:, DV:])


def _chunk_kv(kd, v):
    outs = []
    for p in range(PAIRS):
        r = _dot_tn(kd[:, p * 128:(p + 1) * 128], v[:, p * 256:(p + 1) * 256])
        outs.append(_pair_diag(r))
    return jnp.concatenate(outs, axis=0)


def _adaln_kernel(c_ref, w_ref, b_ref, o_ref):
    c = c_ref[...]
    s = c * jax.nn.sigmoid(c)
    o_ref[...] = _dot(s, w_ref[...]) + b_ref[...]


def _adaln(cc, w_ada, b_ada):
    n = w_ada.shape[1]
    bn = 1536
    return pl.pallas_call(
        _adaln_kernel,
        grid=(n // bn,),
        in_specs=[pl.BlockSpec((MOD_ROWS, D_MODEL), lambda i: (0, 0)),
                  pl.BlockSpec((D_MODEL, bn), lambda i: (0, i)),
                  pl.BlockSpec((1, bn), lambda i: (0, i))],
        out_specs=pl.BlockSpec((MOD_ROWS, bn), lambda i: (0, i)),
        out_shape=jax.ShapeDtypeStruct((MOD_ROWS, n), F32),
        compiler_params=pltpu.CompilerParams(vmem_limit_bytes=VMEM_LIMIT),
        name="adaln",
    )(cc, w_ada, b_ada)


def _ctx_kernel(ctx_ref, sh_ref, sc_ref, wkv_ref, kdf_ref, kdb_ref, cdf_ref, cdb_ref,
                sf_ref, sb_ref, wkvb_ref):
    x = ctx_ref[0]
    u = (_ln(x) * (1.0 + sc_ref[0, 0]) + sh_ref[0, 0]).astype(BF16)
    wkv = wkv_ref[...].astype(BF16)
    wkvb_ref[...] = wkv
    kv = _dot(u, wkv)
    k = kv[:, :QK_W] * (DK ** -0.5)
    v = kv[:, QK_W:].astype(BF16)
    n = x.shape[0] // CHUNK
    sf = jnp.zeros((QK_W, DV), F32)
    for c in range(n):
        kc = k[c * CHUNK:(c + 1) * CHUNK]
        vc = v[c * CHUNK:(c + 1) * CHUNK]
        sf = cdf_ref[...] * sf + _chunk_kv((kc * kdf_ref[...]).astype(BF16), vc)
    sb = jnp.zeros((QK_W, DV), F32)
    for c in reversed(range(n)):
        kc = k[c * CHUNK:(c + 1) * CHUNK]
        vc = v[c * CHUNK:(c + 1) * CHUNK]
        sb = cdb_ref[...] * sb + _chunk_kv((kc * kdb_ref[...]).astype(BF16), vc)
    sf_ref[0] = sf
    sb_ref[0] = sb


def _ctx_states(ctx, mod, w_kv, kdf, kdb, cdf, cdb):
    B, Lc, _ = ctx.shape
    st = jax.ShapeDtypeStruct((B, QK_W, DV), F32)
    return pl.pallas_call(
        _ctx_kernel,
        grid=(B,),
        in_specs=[pl.BlockSpec((1, Lc, D_MODEL), lambda b: (b, 0, 0)),
                  _mod_spec(0, row=B), _mod_spec(1, row=B),
                  _const_spec((D_MODEL, KV_W)),
                  _const_spec((CHUNK, QK_W)), _const_spec((CHUNK, QK_W)),
                  _const_spec((QK_W, DV)), _const_spec((QK_W, DV))],
        out_specs=[pl.BlockSpec((1, QK_W, DV), lambda b: (b, 0, 0)),
                   pl.BlockSpec((1, QK_W, DV), lambda b: (b, 0, 0)),
                   pl.BlockSpec((D_MODEL, KV_W), lambda b: (0, 0))],
        out_shape=[st, st, jax.ShapeDtypeStruct((D_MODEL, KV_W), BF16)],
        compiler_params=pltpu.CompilerParams(
            dimension_semantics=("arbitrary",), vmem_limit_bytes=VMEM_LIMIT),
        name="ctx_states",
    )(ctx, mod, mod, w_kv, kdf, kdb, cdf, cdb)


def _kv_kernel(x_ref, sh_ref, sc_ref, wkv_ref, kdb_ref, cdb_ref, sb0_ref, *rest, n_cast):
    cast_in, (u_ref, kv_ref, sb_ref), cast_out, (s_ref,) = (
        rest[:n_cast], rest[n_cast:n_cast + 3], rest[n_cast + 3:2 * n_cast + 3], rest[2 * n_cast + 3:])

    @pl.when(pl.program_id(1) == 0)
    def _():
        s_ref[...] = sb0_ref[0]

    for src, dst in zip(cast_in, cast_out):
        dst[...] = src[...].astype(BF16)

    x = x_ref[0]
    u = (_ln(x) * (1.0 + sc_ref[0, 0]) + sh_ref[0, 0]).astype(BF16)
    u_ref[0] = u
    kv = _dot(u, wkv_ref[...])
    k = kv[:, :QK_W] * (DK ** -0.5)
    v = kv[:, QK_W:].astype(BF16)
    kv_ref[0, :, :QK_W] = k.astype(BF16)
    kv_ref[0, :, QK_W:] = v
    n = x.shape[0] // CHUNK
    for c in reversed(range(n)):
        s = s_ref[...]
        sb_ref[0, c] = s.astype(BF16)
        kc = k[c * CHUNK:(c + 1) * CHUNK]
        vc = v[c * CHUNK:(c + 1) * CHUNK]
        s_ref[...] = cdb_ref[...] * s + _chunk_kv((kc * kdb_ref[...]).astype(BF16), vc)


def _kv_states(x, mod, w_kv, kdb, cdb, sb0, later_weights):
    B, L, _ = x.shape
    tb = TB_KV
    nb = L // tb
    nch = tb // CHUNK
    steps = B * nb
    slab_specs = [pl.BlockSpec((w.shape[0] // steps, w.shape[1]), lambda b, j: (b * nb + j, 0))
                  for w in later_weights]
    for w in later_weights:
        assert w.shape[0] % (16 * steps) == 0, "row slabs must be whole bf16 tiles"
    return pl.pallas_call(
        functools.partial(_kv_kernel, n_cast=len(later_weights)),
        grid=(B, nb),
        in_specs=[pl.BlockSpec((1, tb, D_MODEL), lambda b, j: (b, nb - 1 - j, 0)),
                  _mod_spec(0), _mod_spec(1),
                  _const_spec((D_MODEL, KV_W)),
                  _const_spec((CHUNK, QK_W)),
                  _const_spec((QK_W, DV)),
                  pl.BlockSpec((1, QK_W, DV), lambda b, j: (b, 0, 0))] + slab_specs,
        out_specs=[pl.BlockSpec((1, tb, D_MODEL), lambda b, j: (b, nb - 1 - j, 0)),
                   pl.BlockSpec((1, tb, KV_W), lambda b, j: (b, nb - 1 - j, 0)),
                   pl.BlockSpec((1, nch, QK_W, DV), lambda b, j: (b, nb - 1 - j, 0, 0))] + slab_specs,
        out_shape=[jax.ShapeDtypeStruct((B, L, D_MODEL), BF16),
                   jax.ShapeDtypeStruct((B, L, KV_W), BF16),
                   jax.ShapeDtypeStruct((B, L // CHUNK, QK_W, DV), BF16)]
        + [jax.ShapeDtypeStruct(w.shape, BF16) for w in later_weights],
        scratch_shapes=[pltpu.VMEM((QK_W, DV), F32)],
        compiler_params=pltpu.CompilerParams(
            dimension_semantics=("arbitrary", "arbitrary"), vmem_limit_bytes=VMEM_LIMIT),
        name="kv_states",
    )(x, mod, mod, w_kv, kdb, cdb, sb0, *later_weights)


def _pool_features(pe, j, nb, tb, seq_len, poolw_ref, pscale_ref):
    n = pe.shape[0]
    t = j * tb + lax.broadcasted_iota(jnp.int32, (tb, POOL_GD), 0)
    outs = []
    for gi, w in enumerate(POOL_WINDOWS):
        half = w // 2
        a = pe[:, gi * POOL_GD:(gi + 1) * POOL_GD]
        centre = a[POOL_HALO:POOL_HALO + tb]
        s = a
        step = 1
        while step < w:
            s = s + pltpu.roll(s, n - step, axis=0)
            step *= 2
        s = pltpu.roll(s, half, axis=0)[POOL_HALO:POOL_HALO + tb]
        cnt = (jnp.minimum(t + half, seq_len) - jnp.maximum(t - half, 0)).astype(F32)
        diff = (s / cnt - centre).astype(BF16)
        outs.append(_dot(diff, poolw_ref[gi]))
    return jnp.concatenate(outs, axis=-1) * pscale_ref[...]


def _mixer_kernel(x_ref, u_ref, up_ref, un_ref, kv_ref, sb_ref, sf0_ref, g1_ref,
                  wr_ref, dm_ref, qdf_ref, qdb_ref, kdf_ref, cdf_ref,
                  poolw_ref, pscale_ref, wbr_ref, wbp_ref, wout_ref, lng_ref, lnb_ref,
                  o_ref, s_ref, r_ref, *, nb, seq_len):
    j = pl.program_id(1)
    tb = x_ref.shape[1]

    @pl.when(j == 0)
    def _():
        s_ref[...] = sf0_ref[0]

    x = x_ref[0]
    u = u_ref[0]

    q = _dot(u, wr_ref[:, COL_Q:COL_G])
    g = _dot(u, wr_ref[:, COL_G:COL_P])

    lane = lax.broadcasted_iota(jnp.int32, (CHUNK, 128), 1)
    zv = jnp.zeros((CHUNK, DV), BF16)
    zs = jnp.zeros((DK, DV), BF16)

    def block_diag_state(s):
        left = jnp.concatenate([s[:DK], zs], axis=0)
        right = jnp.concatenate([zs, s[DK:]], axis=0)
        return jnp.concatenate([left, right], axis=1)

    for c in range(tb // CHUNK):
        rows = slice(c * CHUNK, (c + 1) * CHUNK)
        qc = q[rows]
        qb16 = qc.astype(BF16)
        qf = (qc * qdf_ref[...]).astype(BF16)
        qb = (qc * qdb_ref[...]).astype(BF16)
        kc = kv_ref[0, rows, 0:QK_W].astype(F32)
        vc = kv_ref[0, rows, QK_W:KV_W]
        kd = (kc * kdf_ref[...]).astype(BF16)
        ys = []
        for p in range(PAIRS):
            ql = slice(p * 128, (p + 1) * 128)
            kp = kc[:, ql]
            k_lo = jnp.where(lane < DK, kp, 0.0).astype(BF16)
            k_hi = jnp.where(lane >= DK, kp, 0.0).astype(BF16)
            krhs = jnp.concatenate([k_lo, k_hi], axis=0)
            sc = (_dot_nt(qb16[:, ql], krhs) * dm_ref[p]).astype(BF16)
            vp = vc[:, p * 256:(p + 1) * 256]
            vrhs = jnp.concatenate([jnp.concatenate([vp[:, :DV], zv], axis=1),
                                    jnp.concatenate([zv, vp[:, DV:]], axis=1)], axis=0)
            y = _dot(sc, vrhs)
            s_f = s_ref[ql, :]
            srhs = jnp.concatenate([block_diag_state(s_f.astype(BF16)),
                                    block_diag_state(sb_ref[0, c, ql, :])], axis=0)
            qlhs = jnp.concatenate([qf[:, ql], qb[:, ql]], axis=1)
            y = y + _dot(qlhs, srhs)
            ys.append(y)
            s_ref[ql, :] = cdf_ref[ql, :] * s_f + _pair_diag(_dot_tn(kd[:, ql], vp))
        gc = g[rows]
        sg = gc * jax.nn.sigmoid(gc)
        for p in range(PAIRS):
            for hh in range(2):
                h = 2 * p + hh
                yh = ys[p][:, hh * DV:(hh + 1) * DV]
                mu = jnp.mean(yh, axis=-1, keepdims=True)
                yc = yh - mu
                var = jnp.mean(yc * yc, axis=-1, keepdims=True)
                yn = yc * lax.rsqrt(var + LN_EPS)
                r_ref[rows, h * DV:(h + 1) * DV] = (yn * sg[:, h * DV:(h + 1) * DV]).astype(BF16)

    ret = _dot(r_ref[...], wbr_ref[...])

    ph = _dot(jnp.concatenate([up_ref[0], un_ref[0]], axis=0), wr_ref[:, COL_P:COL_GA])
    p_prev = ph[U_HALO - POOL_HALO:U_HALO] * (j > 0).astype(F32)
    p_next = ph[U_HALO:U_HALO + POOL_HALO] * (j < nb - 1).astype(F32)
    pm = _dot(u, wr_ref[:, COL_P:COL_GA])
    pe = jnp.concatenate([p_prev, pm, p_next], axis=0)
    feat = _pool_features(pe, j, nb, tb, seq_len, poolw_ref, pscale_ref).astype(BF16)
    pool = _dot(feat, wbp_ref[...])

    ga = _dot(u, wr_ref[:, COL_GA:COL_GB])
    merged = jax.nn.sigmoid(ga) * ret
    gb = _dot(u, wr_ref[:, COL_GB:IN_W])
    merged = (merged + jax.nn.sigmoid(gb) * pool).astype(BF16)
    for r0 in range(0, tb, MIX_RB):
        rs = slice(r0, r0 + MIX_RB)
        z = ALPHA * x[rs] + g1_ref[0, 0] * _dot(merged[rs], wout_ref[...])
        o_ref[0, rs] = _ln(z) * lng_ref[...] + lnb_ref[...]


def _mixer(x, u, kv, sb, sf0, mod, w_rest, dm, qdf, qdb, kdf, cdf,
           pool_w, pool_scale, w_br, w_bp, w_out, ln_g, ln_b):
    B, L, _ = x.shape
    tb = TB_MIX
    nb = L // tb
    nch = tb // CHUNK
    hb = tb // U_HALO
    nh = L // U_HALO
    return pl.pallas_call(
        functools.partial(_mixer_kernel, nb=nb, seq_len=L),
        grid=(B, nb),
        in_specs=[pl.BlockSpec((1, tb, D_MODEL), lambda b, j: (b, j, 0)),
                  pl.BlockSpec((1, tb, D_MODEL), lambda b, j: (b, j, 0)),
                  pl.BlockSpec((1, U_HALO, D_MODEL), lambda b, j: (b, jnp.maximum(j * hb - 1, 0), 0)),
                  pl.BlockSpec((1, U_HALO, D_MODEL),
                               lambda b, j: (b, jnp.minimum((j + 1) * hb, nh - 1), 0)),
                  pl.BlockSpec((1, tb, KV_W), lambda b, j: (b, j, 0)),
                  pl.BlockSpec((1, nch, QK_W, DV), lambda b, j: (b, j, 0, 0)),
                  pl.BlockSpec((1, QK_W, DV), lambda b, j: (b, 0, 0)),
                  _mod_spec(2),
                  _const_spec((D_MODEL, IN_W)),
                  _const_spec((PAIRS, CHUNK, 256)),
                  _const_spec((CHUNK, QK_W)), _const_spec((CHUNK, QK_W)), _const_spec((CHUNK, QK_W)),
                  _const_spec((QK_W, DV)),
                  _const_spec((len(POOL_WINDOWS), POOL_GD, POOL_GD)),
                  _const_spec((1, POOL_W)),
                  _const_spec((V_W, D_MODEL)),
                  _const_spec((POOL_W, D_MODEL)),
                  _const_spec((D_MODEL, D_MODEL)),
                  _const_spec((1, D_MODEL)), _const_spec((1, D_MODEL))],
        out_specs=pl.BlockSpec((1, tb, D_MODEL), lambda b, j: (b, j, 0)),
        out_shape=jax.ShapeDtypeStruct((B, L, D_MODEL), F32),
        scratch_shapes=[pltpu.VMEM((QK_W, DV), F32), pltpu.VMEM((tb, V_W), BF16)],
        compiler_params=pltpu.CompilerParams(
            dimension_semantics=("arbitrary", "arbitrary"), vmem_limit_bytes=VMEM_LIMIT),
        name="mixer",
    )(x, u, u, u, kv, sb, sf0, mod, w_rest, dm, qdf, qdb, kdf, cdf,
      pool_w, pool_scale, w_br, w_bp, w_out, ln_g, ln_b)


def _ffn_kernel(x_ref, xn_ref, sh_ref, sc_ref, g2_ref, wup_ref, cw_ref, wdn_ref,
                lng_ref, lnb_ref, o_ref, u_ref, acc_ref, ha_ref, hb_ref, top_ref, *, nb):
    j = pl.program_id(1)
    tb = x_ref.shape[1]
    m = tb + GRID_W
    n = tb + 2 * GRID_W
    scale = 1.0 + sc_ref[0, 0]
    shift = sh_ref[0, 0]
    x = x_ref[0]

    def mod(v):
        return _ln(v) * scale + shift

    @pl.when(j == 0)
    def _():
        top_ref[...] = jnp.zeros_like(top_ref)

    u_ref[0:tb] = mod(x).astype(BF16)
    u_ref[tb:m] = (mod(xn_ref[0]) * (j < nb - 1).astype(F32)).astype(BF16)

    col = lax.broadcasted_iota(jnp.int32, (tb, FF_CW), 0) & (GRID_W - 1)
    has_left = col > 0
    has_right = col < GRID_W - 1

    pad = jnp.zeros((FF_PAD, FF_CW), F32)
    for h_ref in (ha_ref, hb_ref):
        for ab in range(2):
            h_ref[ab, 0:FF_PAD] = pad
            h_ref[ab, FF_PAD + n:FF_PAD + n + FF_PAD] = pad

    def lanes(c, half):
        return pl.ds(pl.multiple_of(half * D_FF + c * FF_CW, FF_CW), FF_CW)

    def up(c, h_ref):
        u = u_ref[...]
        for half in range(2):
            h = _dot(u, wup_ref[:, lanes(c, half)])
            h_ref[half, FF_PAD:FF_PAD + GRID_W] = top_ref[half, c]
            h_ref[half, FF_PAD + GRID_W:FF_PAD + n] = h
            top_ref[half, c] = h[tb - GRID_W:tb]

    def conv(h_ref, ab, cw):
        cols = []
        for dc in range(3):
            hs = h_ref[ab, FF_PAD + dc - 1:FF_PAD + dc - 1 + n].astype(BF16)
            g = None
            for dr in range(3):
                term = cw[3 * dr + dc:3 * dr + dc + 1] * hs[dr * GRID_W:dr * GRID_W + tb]
                g = term if g is None else g + term
            cols.append(g)
        zero = jnp.zeros_like(cols[1])
        return (cols[1] + cw[9:10]) + (jnp.where(has_left, cols[0], zero) + jnp.where(has_right, cols[2], zero))

    def down(c, h_ref):
        a = conv(h_ref, 0, cw_ref[:, lanes(c, 0)])
        b = conv(h_ref, 1, cw_ref[:, lanes(c, 1)])
        acc_ref[...] += _dot(_gelu_tanh(a.astype(F32)).astype(BF16) * b, wdn_ref[c])

    acc_ref[...] = jnp.zeros_like(acc_ref)
    up(0, ha_ref)

    def body(i, carry):
        c = 2 * i
        up(c + 1, hb_ref)
        down(c, ha_ref)
        up(c + 2, ha_ref)
        down(c + 1, hb_ref)
        return carry

    lax.fori_loop(0, FF_NC // 2, body, 0)
    down(FF_NC - 1, ha_ref)
    z = ALPHA * x + g2_ref[0, 0] * acc_ref[...]
    o_ref[0] = _ln(z) * lng_ref[...] + lnb_ref[...]


def _ffn(x, mod, w_up, conv_wb, w_down, ln_g, ln_b):
    B, L, _ = x.shape
    tb = TB_FFN
    nb = L // tb
    hb = tb // GRID_W
    nh = L // GRID_W
    return pl.pallas_call(
        functools.partial(_ffn_kernel, nb=nb),
        grid=(B, nb),
        in_specs=[pl.BlockSpec((1, tb, D_MODEL), lambda b, j: (b, j, 0)),
                  pl.BlockSpec((1, GRID_W, D_MODEL),
                               lambda b, j: (b, jnp.minimum((j + 1) * hb, nh - 1), 0)),
                  _mod_spec(3), _mod_spec(4), _mod_spec(5),
                  _const_spec((D_MODEL, 2 * D_FF)),
                  _const_spec((16, 2 * D_FF)),
                  _const_spec((FF_NC, FF_CW, D_MODEL)),
                  _const_spec((1, D_MODEL)), _const_spec((1, D_MODEL))],
        out_specs=pl.BlockSpec((1, tb, D_MODEL), lambda b, j: (b, j, 0)),
        out_shape=jax.ShapeDtypeStruct((B, L, D_MODEL), F32),
        scratch_shapes=[pltpu.VMEM((tb + GRID_W, D_MODEL), BF16),
                        pltpu.VMEM((tb, D_MODEL), F32),
                        pltpu.VMEM((2, tb + 2 * GRID_W + 2 * FF_PAD, FF_CW), F32),
                        pltpu.VMEM((2, tb + 2 * GRID_W + 2 * FF_PAD, FF_CW), F32),
                        pltpu.VMEM((2, FF_NC, GRID_W, FF_CW), F32)],
        compiler_params=pltpu.CompilerParams(
            dimension_semantics=("arbitrary", "arbitrary"), vmem_limit_bytes=VMEM_LIMIT),
        name="conv_ffn",
    )(x, x, mod, mod, mod, w_up, conv_wb, w_down, ln_g, ln_b)


def _decay_tables(ret_decay_logit):
    lg = jax.nn.log_sigmoid(ret_decay_logit.astype(F32))
    pos = jnp.arange(CHUNK, dtype=F32)
    diff = pos[:, None] - pos[None, :]
    d_f = jnp.where(diff[None] >= 0, jnp.exp(jnp.maximum(diff, 0.0)[None] * lg[0][:, None, None]), 0.0)
    d_b = jnp.where(diff[None] <= 0, jnp.exp(jnp.maximum(-diff, 0.0)[None] * lg[1][:, None, None]), 0.0)
    dm = (d_f + d_b).reshape(PAIRS, 2, CHUNK, CHUNK).transpose(0, 2, 1, 3).reshape(PAIRS, CHUNK, 2 * CHUNK)

    def lanes(t):
        return jnp.repeat(t, DK, axis=1)

    qdf = lanes(jnp.exp((pos + 1.0)[:, None] * lg[0][None, :]))
    qdb = lanes(jnp.exp((CHUNK - pos)[:, None] * lg[1][None, :]))
    kdf = lanes(jnp.exp((CHUNK - 1.0 - pos)[:, None] * lg[0][None, :]))
    kdb = lanes(jnp.exp(pos[:, None] * lg[1][None, :]))

    def rows(t):
        return jnp.broadcast_to(jnp.repeat(t, DK)[:, None], (QK_W, DV))

    cdf = rows(jnp.exp(CHUNK * lg[0]))
    cdb = rows(jnp.exp(CHUNK * lg[1]))
    return dm, qdf, qdb, kdf, kdb, cdf, cdb


def kernel(x, c, ctx, c_ctx, w_ada, b_ada, w_in, ret_decay_logit, pool_w, pool_scale, w_branch_ret,
           w_branch_pool, w_out, ln1_g, ln1_b, w_up, conv_w, conv_b, w_down, ln2_g, ln2_b):
    B = x.shape[0]
    D = D_MODEL
    assert w_ada.shape[0] == 1, "single-layer stack"

    cc = jnp.concatenate([c, c_ctx[None, :], jnp.zeros((MOD_ROWS - B - 1, D), F32)], axis=0)
    mod = _adaln(cc, w_ada[0], b_ada[0][None, :]).reshape(MOD_ROWS, N_MOD, 1, D)

    dm, qdf, qdb, kdf, kdb, cdf, cdb = _decay_tables(ret_decay_logit[0])

    s_f, s_b, w_kv_b = _ctx_states(ctx, mod, w_in[0], kdf, kdb, cdf, cdb)
    u, kv, sb, w_in_b, w_up_b, w_down_b, w_br_b, w_bp_b, w_out_b = _kv_states(
        x, mod, w_kv_b, kdb, cdb, s_b,
        [w_in[0], w_up[0], w_down[0], w_branch_ret[0], w_branch_pool[0], w_out[0]])
    x1 = _mixer(x, u, kv, sb, s_f, mod, w_in_b, dm, qdf, qdb, kdf, cdf,
                pool_w[0].astype(BF16), pool_scale[0][None, :], w_br_b, w_bp_b, w_out_b,
                ln1_g[0][None, :], ln1_b[0][None, :])

    conv_wb = jnp.concatenate([conv_w[0].reshape(9, 2 * D_FF), conv_b[0][None, :],
                               jnp.zeros((6, 2 * D_FF), F32)], axis=0).astype(BF16)
    return _ffn(x1, mod, w_up_b, conv_wb, w_down_b.reshape(FF_NC, FF_CW, D),
                ln2_g[0][None, :], ln2_b[0][None, :])
```

```python
import functools

import jax
import jax.numpy as jnp
import numpy as np
from jax import lax
from jax.experimental import pallas as pl
from jax.experimental.pallas import tpu as pltpu

F32 = jnp.float32
BF16 = jnp.bfloat16

D_MODEL = 1024
GRID_W = 64
HEADS = 8
DK = 64
DV = 128
QK_W = HEADS * DK
V_W = HEADS * DV
KV_W = QK_W + V_W
CHUNK = 128
PAIRS = HEADS // 2
POOL_WINDOWS = (2, 4, 8, 16)
POOL_GD = 128
POOL_W = 512
COL_Q = KV_W
COL_G = COL_Q + QK_W
COL_P = COL_G + V_W
COL_GA = COL_P + POOL_W
COL_GB = COL_GA + D_MODEL
IN_W = COL_GB + D_MODEL
D_FF = 2816
FF_CW = 256
FF_NC = D_FF // FF_CW
FF_PAD = 16
N_MOD = 6
LN_EPS = 1e-6
ALPHA = 2.0 ** 0.25
POOL_HALO = 8
U_HALO = 16
MOD_ROWS = 8

VMEM_LIMIT = 60 * 1024 * 1024

MIX_RB = 256

TB_KV = 1024
TB_MIX = 1024
TB_FFN = 1024


def _dot(a, b):
    return jnp.dot(a, b, preferred_element_type=F32)


def _dot_nt(a, b):
    return lax.dot_general(a, b, (((1,), (1,)), ((), ())), preferred_element_type=F32)


def _dot_tn(a, b):
    return lax.dot_general(a, b, (((0,), (0,)), ((), ())), preferred_element_type=F32)


def _ln(x):
    mu = jnp.mean(x, axis=-1, keepdims=True)
    xc = x - mu
    var = jnp.mean(xc * xc, axis=-1, keepdims=True)
    return xc * lax.rsqrt(var + LN_EPS)


def _gelu_tanh(x):
    c = float(np.sqrt(2.0 / np.pi))
    half = 0.5 * x
    return half + half * jnp.tanh(x * (c + (c * 0.044715) * (x * x)))


def _mod_spec(k, row=None):
    if row is None:
        return pl.BlockSpec((1, 1, 1, D_MODEL), lambda b, *_: (b, k, 0, 0))
    return pl.BlockSpec((1, 1, 1, D_MODEL), lambda *_: (row, k, 0, 0))


def _const_spec(shape):
    nd = len(shape)
    return pl.BlockSpec(shape, lambda *_: (0,) * nd, pipeline_mode=pl.Buffered(1))


def _pair_diag(r):
    row = lax.broadcasted_iota(jnp.int32, (CHUNK, DV), 0)
    return jnp.where(row < DK, r[:, :DV], r[:, DV:])


def _chunk_kv(kd, v):
    outs = []
    for p in range(PAIRS):
        r = _dot_tn(kd[:, p * 128:(p + 1) * 128], v[:, p * 256:(p + 1) * 256])
        outs.append(_pair_diag(r))
    return jnp.concatenate(outs, axis=0)


def _adaln_kernel(c_ref, w_ref, b_ref, o_ref):
    c = c_ref[...]
    s = c * jax.nn.sigmoid(c)
    o_ref[...] = _dot(s, w_ref[...]) + b_ref[...]


def _adaln(cc, w_ada, b_ada):
    n = w_ada.shape[1]
    bn = 1536
    return pl.pallas_call(
        _adaln_kernel,
        grid=(n // bn,),
        in_specs=[pl.BlockSpec((MOD_ROWS, D_MODEL), lambda i: (0, 0)),
                  pl.BlockSpec((D_MODEL, bn), lambda i: (0, i)),
                  pl.BlockSpec((1, bn), lambda i: (0, i))],
        out_specs=pl.BlockSpec((MOD_ROWS, bn), lambda i: (0, i)),
        out_shape=jax.ShapeDtypeStruct((MOD_ROWS, n), F32),
        compiler_params=pltpu.CompilerParams(vmem_limit_bytes=VMEM_LIMIT),
        name="adaln",
    )(cc, w_ada, b_ada)


def _ctx_kernel(ctx_ref, sh_ref, sc_ref, wkv_ref, kdf_ref, kdb_ref, cdf_ref, cdb_ref,
                sf_ref, sb_ref, wkvb_ref):
    x = ctx_ref[0]
    u = (_ln(x) * (1.0 + sc_ref[0, 0]) + sh_ref[0, 0]).astype(BF16)
    wkv = wkv_ref[...].astype(BF16)
    wkvb_ref[...] = wkv
    kv = _dot(u, wkv)
    k = kv[:, :QK_W] * (DK ** -0.5)
    v = kv[:, QK_W:].astype(BF16)
    n = x.shape[0] // CHUNK
    sf = jnp.zeros((QK_W, DV), F32)
    for c in range(n):
        kc = k[c * CHUNK:(c + 1) * CHUNK]
        vc = v[c * CHUNK:(c + 1) * CHUNK]
        sf = cdf_ref[...] * sf + _chunk_kv((kc * kdf_ref[...]).astype(BF16), vc)
    sb = jnp.zeros((QK_W, DV), F32)
    for c in reversed(range(n)):
        kc = k[c * CHUNK:(c + 1) * CHUNK]
        vc = v[c * CHUNK:(c + 1) * CHUNK]
        sb = cdb_ref[...] * sb + _chunk_kv((kc * kdb_ref[...]).astype(BF16), vc)
    sf_ref[0] = sf
    sb_ref[0] = sb


def _ctx_states(ctx, mod, w_kv, kdf, kdb, cdf, cdb):
    B, Lc, _ = ctx.shape
    st = jax.ShapeDtypeStruct((B, QK_W, DV), F32)
    return pl.pallas_call(
        _ctx_kernel,
        grid=(B,),
        in_specs=[pl.BlockSpec((1, Lc, D_MODEL), lambda b: (b, 0, 0)),
                  _mod_spec(0, row=B), _mod_spec(1, row=B),
                  _const_spec((D_MODEL, KV_W)),
                  _const_spec((CHUNK, QK_W)), _const_spec((CHUNK, QK_W)),
                  _const_spec((QK_W, DV)), _const_spec((QK_W, DV))],
        out_specs=[pl.BlockSpec((1, QK_W, DV), lambda b: (b, 0, 0)),
                   pl.BlockSpec((1, QK_W, DV), lambda b: (b, 0, 0)),
                   pl.BlockSpec((D_MODEL, KV_W), lambda b: (0, 0))],
        out_shape=[st, st, jax.ShapeDtypeStruct((D_MODEL, KV_W), BF16)],
        compiler_params=pltpu.CompilerParams(
            dimension_semantics=("arbitrary",), vmem_limit_bytes=VMEM_LIMIT),
        name="ctx_states",
    )(ctx, mod, mod, w_kv, kdf, kdb, cdf, cdb)


def _kv_kernel(x_ref, sh_ref, sc_ref, wkv_ref, kdb_ref, cdb_ref, sb0_ref, *rest, n_cast):
    cast_in, (u_ref, kv_ref, sb_ref), cast_out, (s_ref,) = (
        rest[:n_cast], rest[n_cast:n_cast + 3], rest[n_cast + 3:2 * n_cast + 3], rest[2 * n_cast + 3:])

    @pl.when(pl.program_id(1) == 0)
    def _():
        s_ref[...] = sb0_ref[0]

    for src, dst in zip(cast_in, cast_out):
        dst[...] = src[...].astype(BF16)

    x = x_ref[0]
    u = (_ln(x) * (1.0 + sc_ref[0, 0]) + sh_ref[0, 0]).astype(BF16)
    u_ref[0] = u
    kv = _dot(u, wkv_ref[...])
    k = kv[:, :QK_W] * (DK ** -0.5)
    v = kv[:, QK_W:].astype(BF16)
    kv_ref[0, :, :QK_W] = k.astype(BF16)
    kv_ref[0, :, QK_W:] = v
    n = x.shape[0] // CHUNK
    for c in reversed(range(n)):
        s = s_ref[...]
        sb_ref[0, c] = s.astype(BF16)
        kc = k[c * CHUNK:(c + 1) * CHUNK]
        vc = v[c * CHUNK:(c + 1) * CHUNK]
        s_ref[...] = cdb_ref[...] * s + _chunk_kv((kc * kdb_ref[...]).astype(BF16), vc)


def _kv_states(x, mod, w_kv, kdb, cdb, sb0, later_weights):
    B, L, _ = x.shape
    tb = TB_KV
    nb = L // tb
    nch = tb // CHUNK
    steps = B * nb
    slab_specs = [pl.BlockSpec((w.shape[0] // steps, w.shape[1]), lambda b, j: (b * nb + j, 0))
                  for w in later_weights]
    for w in later_weights:
        assert w.shape[0] % (16 * steps) == 0, "row slabs must be whole bf16 tiles"
    return pl.pallas_call(
        functools.partial(_kv_kernel, n_cast=len(later_weights)),
        grid=(B, nb),
        in_specs=[pl.BlockSpec((1, tb, D_MODEL), lambda b, j: (b, nb - 1 - j, 0)),
                  _mod_spec(0), _mod_spec(1),
                  _const_spec((D_MODEL, KV_W)),
                  _const_spec((CHUNK, QK_W)),
                  _const_spec((QK_W, DV)),
                  pl.BlockSpec((1, QK_W, DV), lambda b, j: (b, 0, 0))] + slab_specs,
        out_specs=[pl.BlockSpec((1, tb, D_MODEL), lambda b, j: (b, nb - 1 - j, 0)),
                   pl.BlockSpec((1, tb, KV_W), lambda b, j: (b, nb - 1 - j, 0)),
                   pl.BlockSpec((1, nch, QK_W, DV), lambda b, j: (b, nb - 1 - j, 0, 0))] + slab_specs,
        out_shape=[jax.ShapeDtypeStruct((B, L, D_MODEL), BF16),
                   jax.ShapeDtypeStruct((B, L, KV_W), BF16),
                   jax.ShapeDtypeStruct((B, L // CHUNK, QK_W, DV), BF16)]
        + [jax.ShapeDtypeStruct(w.shape, BF16) for w in later_weights],
        scratch_shapes=[pltpu.VMEM((QK_W, DV), F32)],
        compiler_params=pltpu.CompilerParams(
            dimension_semantics=("arbitrary", "arbitrary"), vmem_limit_bytes=VMEM_LIMIT),
        name="kv_states",
    )(x, mod, mod, w_kv, kdb, cdb, sb0, *later_weights)


def _pool_features(pe, j, nb, tb, seq_len, poolw_ref, pscale_ref):
    n = pe.shape[0]
    t = j * tb + lax.broadcasted_iota(jnp.int32, (tb, POOL_GD), 0)
    outs = []
    for gi, w in enumerate(POOL_WINDOWS):
        half = w // 2
        a = pe[:, gi * POOL_GD:(gi + 1) * POOL_GD]
        centre = a[POOL_HALO:POOL_HALO + tb]
        s = a
        step = 1
        while step < w:
            s = s + pltpu.roll(s, n - step, axis=0)
            step *= 2
        s = pltpu.roll(s, half, axis=0)[POOL_HALO:POOL_HALO + tb]
        cnt = (jnp.minimum(t + half, seq_len) - jnp.maximum(t - half, 0)).astype(F32)
        diff = (s / cnt - centre).astype(BF16)
        outs.append(_dot(diff, poolw_ref[gi]))
    return jnp.concatenate(outs, axis=-1) * pscale_ref[...]


def _mixer_kernel(x_ref, u_ref, up_ref, un_ref, kv_ref, sb_ref, sf0_ref, g1_ref,
                  wr_ref, dm_ref, qdf_ref, qdb_ref, kdf_ref, cdf_ref,
                  poolw_ref, pscale_ref, wbr_ref, wbp_ref, wout_ref, lng_ref, lnb_ref,
                  o_ref, s_ref, r_ref, *, nb, seq_len):
    j = pl.program_id(1)
    tb = x_ref.shape[1]

    @pl.when(j == 0)
    def _():
        s_ref[...] = sf0_ref[0]

    x = x_ref[0]
    u = u_ref[0]

    q = _dot(u, wr_ref[:, COL_Q:COL_G])
    g = _dot(u, wr_ref[:, COL_G:COL_P])

    lane = lax.broadcasted_iota(jnp.int32, (CHUNK, 128), 1)
    zv = jnp.zeros((CHUNK, DV), BF16)
    zs = jnp.zeros((DK, DV), BF16)

    def block_diag_state(s):
        left = jnp.concatenate([s[:DK], zs], axis=0)
        right = jnp.concatenate([zs, s[DK:]], axis=0)
        return jnp.concatenate([left, right], axis=1)

    for c in range(tb // CHUNK):
        rows = slice(c * CHUNK, (c + 1) * CHUNK)
        qc = q[rows]
        qb16 = qc.astype(BF16)
        qf = (qc * qdf_ref[...]).astype(BF16)
        qb = (qc * qdb_ref[...]).astype(BF16)
        kc = kv_ref[0, rows, 0:QK_W].astype(F32)
        vc = kv_ref[0, rows, QK_W:KV_W]
        kd = (kc * kdf_ref[...]).astype(BF16)
        ys = []
        for p in range(PAIRS):
            ql = slice(p * 128, (p + 1) * 128)
            kp = kc[:, ql]
            k_lo = jnp.where(lane < DK, kp, 0.0).astype(BF16)
            k_hi = jnp.where(lane >= DK, kp, 0.0).astype(BF16)
            krhs = jnp.concatenate([k_lo, k_hi], axis=0)
            sc = (_dot_nt(qb16[:, ql], krhs) * dm_ref[p]).astype(BF16)
            vp = vc[:, p * 256:(p + 1) * 256]
            vrhs = jnp.concatenate([jnp.concatenate([vp[:, :DV], zv], axis=1),
                                    jnp.concatenate([zv, vp[:, DV:]], axis=1)], axis=0)
            y = _dot(sc, vrhs)
            s_f = s_ref[ql, :]
            srhs = jnp.concatenate([block_diag_state(s_f.astype(BF16)),
                                    block_diag_state(sb_ref[0, c, ql, :])], axis=0)
            qlhs = jnp.concatenate([qf[:, ql], qb[:, ql]], axis=1)
            y = y + _dot(qlhs, srhs)
            ys.append(y)
            s_ref[ql, :] = cdf_ref[ql, :] * s_f + _pair_diag(_dot_tn(kd[:, ql], vp))
        gc = g[rows]
        sg = gc * jax.nn.sigmoid(gc)
        for p in range(PAIRS):
            for hh in range(2):
                h = 2 * p + hh
                yh = ys[p][:, hh * DV:(hh + 1) * DV]
                mu = jnp.mean(yh, axis=-1, keepdims=True)
                yc = yh - mu
                var = jnp.mean(yc * yc, axis=-1, keepdims=True)
                yn = yc * lax.rsqrt(var + LN_EPS)
                r_ref[rows, h * DV:(h + 1) * DV] = (yn * sg[:, h * DV:(h + 1) * DV]).astype(BF16)

    ret = _dot(r_ref[...], wbr_ref[...])

    ph = _dot(jnp.concatenate([up_ref[0], un_ref[0]], axis=0), wr_ref[:, COL_P:COL_GA])
    p_prev = ph[U_HALO - POOL_HALO:U_HALO] * (j > 0).astype(F32)
    p_next = ph[U_HALO:U_HALO + POOL_HALO] * (j < nb - 1).astype(F32)
    pm = _dot(u, wr_ref[:, COL_P:COL_GA])
    pe = jnp.concatenate([p_prev, pm, p_next], axis=0)
    feat = _pool_features(pe, j, nb, tb, seq_len, poolw_ref, pscale_ref).astype(BF16)
    pool = _dot(feat, wbp_ref[...])

    ga = _dot(u, wr_ref[:, COL_GA:COL_GB])
    merged = jax.nn.sigmoid(ga) * ret
    gb = _dot(u, wr_ref[:, COL_GB:IN_W])
    merged = (merged + jax.nn.sigmoid(gb) * pool).astype(BF16)
    for r0 in range(0, tb, MIX_RB):
        rs = slice(r0, r0 + MIX_RB)
        z = ALPHA * x[rs] + g1_ref[0, 0] * _dot(merged[rs], wout_ref[...])
        o_ref[0, rs] = _ln(z) * lng_ref[...] + lnb_ref[...]


def _mixer(x, u, kv, sb, sf0, mod, w_rest, dm, qdf, qdb, kdf, cdf,
           pool_w, pool_scale, w_br, w_bp, w_out, ln_g, ln_b):
    B, L, _ = x.shape
    tb = TB_MIX
    nb = L // tb
    nch = tb // CHUNK
    hb = tb // U_HALO
    nh = L // U_HALO
    return pl.pallas_call(
        functools.partial(_mixer_kernel, nb=nb, seq_len=L),
        grid=(B, nb),
        in_specs=[pl.BlockSpec((1, tb, D_MODEL), lambda b, j: (b, j, 0)),
                  pl.BlockSpec((1, tb, D_MODEL), lambda b, j: (b, j, 0)),
                  pl.BlockSpec((1, U_HALO, D_MODEL), lambda b, j: (b, jnp.maximum(j * hb - 1, 0), 0)),
                  pl.BlockSpec((1, U_HALO, D_MODEL),
                               lambda b, j: (b, jnp.minimum((j + 1) * hb, nh - 1), 0)),
                  pl.BlockSpec((1, tb, KV_W), lambda b, j: (b, j, 0)),
                  pl.BlockSpec((1, nch, QK_W, DV), lambda b, j: (b, j, 0, 0)),
                  pl.BlockSpec((1, QK_W, DV), lambda b, j: (b, 0, 0)),
                  _mod_spec(2),
                  _const_spec((D_MODEL, IN_W)),
                  _const_spec((PAIRS, CHUNK, 256)),
                  _const_spec((CHUNK, QK_W)), _const_spec((CHUNK, QK_W)), _const_spec((CHUNK, QK_W)),
                  _const_spec((QK_W, DV)),
                  _const_spec((len(POOL_WINDOWS), POOL_GD, POOL_GD)),
                  _const_spec((1, POOL_W)),
                  _const_spec((V_W, D_MODEL)),
                  _const_spec((POOL_W, D_MODEL)),
                  _const_spec((D_MODEL, D_MODEL)),
                  _const_spec((1, D_MODEL)), _const_spec((1, D_MODEL))],
        out_specs=pl.BlockSpec((1, tb, D_MODEL), lambda b, j: (b, j, 0)),
        out_shape=jax.ShapeDtypeStruct((B, L, D_MODEL), F32),
        scratch_shapes=[pltpu.VMEM((QK_W, DV), F32), pltpu.VMEM((tb, V_W), BF16)],
        compiler_params=pltpu.CompilerParams(
            dimension_semantics=("arbitrary", "arbitrary"), vmem_limit_bytes=VMEM_LIMIT),
        name="mixer",
    )(x, u, u, u, kv, sb, sf0, mod, w_rest, dm, qdf, qdb, kdf, cdf,
      pool_w, pool_scale, w_br, w_bp, w_out, ln_g, ln_b)


def _ffn_kernel(x_ref, xn_ref, sh_ref, sc_ref, g2_ref, wup_ref, cw_ref, wdn_ref,
                lng_ref, lnb_ref, o_ref, u_ref, acc_ref, ha_ref, hb_ref, top_ref, *, nb):
    j = pl.program_id(1)
    tb = x_ref.shape[1]
    m = tb + GRID_W
    n = tb + 2 * GRID_W
    scale = 1.0 + sc_ref[0, 0]
    shift = sh_ref[0, 0]
    x = x_ref[0]

    def mod(v):
        return _ln(v) * scale + shift

    @pl.when(j == 0)
    def _():
        top_ref[...] = jnp.zeros_like(top_ref)

    u_ref[0:tb] = mod(x).astype(BF16)
    u_ref[tb:m] = (mod(xn_ref[0]) * (j < nb - 1).astype(F32)).astype(BF16)

    col = lax.broadcasted_iota(jnp.int32, (tb, FF_CW), 0) & (GRID_W - 1)
    has_left = col > 0
    has_right = col < GRID_W - 1

    pad = jnp.zeros((FF_PAD, FF_CW), BF16)
    for h_ref in (ha_ref, hb_ref):
        for ab in range(2):
            h_ref[ab, 0:FF_PAD] = pad
            h_ref[ab, FF_PAD + n:FF_PAD + n + FF_PAD] = pad

    def lanes(c, half):
        return pl.ds(pl.multiple_of(half * D_FF + c * FF_CW, FF_CW), FF_CW)

    def up(c, h_ref):
        u = u_ref[...]
        for half in range(2):
            h = _dot(u, wup_ref[:, lanes(c, half)]).astype(BF16)
            h_ref[half, FF_PAD:FF_PAD + GRID_W] = top_ref[half, c]
            h_ref[half, FF_PAD + GRID_W:FF_PAD + n] = h
            top_ref[half, c] = h[tb - GRID_W:tb]

    def conv(h_ref, ab, cw):
        cols = []
        for dc in range(3):
            hs = h_ref[ab, FF_PAD + dc - 1:FF_PAD + dc - 1 + n]
            g = None
            for dr in range(3):
                term = cw[3 * dr + dc:3 * dr + dc + 1] * hs[dr * GRID_W:dr * GRID_W + tb]
                g = term if g is None else g + term
            cols.append(g)
        zero = jnp.zeros_like(cols[1])
        return (cols[1] + cw[9:10]) + (jnp.where(has_left, cols[0], zero) + jnp.where(has_right, cols[2], zero))

    def down(c, h_ref):
        a = conv(h_ref, 0, cw_ref[:, lanes(c, 0)])
        b = conv(h_ref, 1, cw_ref[:, lanes(c, 1)])
        acc_ref[...] += _dot(_gelu_tanh(a.astype(F32)).astype(BF16) * b, wdn_ref[c])

    acc_ref[...] = jnp.zeros_like(acc_ref)
    up(0, ha_ref)

    def body(i, carry):
        c = 2 * i
        up(c + 1, hb_ref)
        down(c, ha_ref)
        up(c + 2, ha_ref)
        down(c + 1, hb_ref)
        return carry

    lax.fori_loop(0, FF_NC // 2, body, 0)
    down(FF_NC - 1, ha_ref)
    z = ALPHA * x + g2_ref[0, 0] * acc_ref[...]
    o_ref[0] = _ln(z) * lng_ref[...] + lnb_ref[...]


def _ffn(x, mod, w_up, conv_wb, w_down, ln_g, ln_b):
    B, L, _ = x.shape
    tb = TB_FFN
    nb = L // tb
    hb = tb // GRID_W
    nh = L // GRID_W
    return pl.pallas_call(
        functools.partial(_ffn_kernel, nb=nb),
        grid=(B, nb),
        in_specs=[pl.BlockSpec((1, tb, D_MODEL), lambda b, j: (b, j, 0)),
                  pl.BlockSpec((1, GRID_W, D_MODEL),
                               lambda b, j: (b, jnp.minimum((j + 1) * hb, nh - 1), 0)),
                  _mod_spec(3), _mod_spec(4), _mod_spec(5),
                  _const_spec((D_MODEL, 2 * D_FF)),
                  _const_spec((16, 2 * D_FF)),
                  _const_spec((FF_NC, FF_CW, D_MODEL)),
                  _const_spec((1, D_MODEL)), _const_spec((1, D_MODEL))],
        out_specs=pl.BlockSpec((1, tb, D_MODEL), lambda b, j: (b, j, 0)),
        out_shape=jax.ShapeDtypeStruct((B, L, D_MODEL), F32),
        scratch_shapes=[pltpu.VMEM((tb + GRID_W, D_MODEL), BF16),
                        pltpu.VMEM((tb, D_MODEL), F32),
                        pltpu.VMEM((2, tb + 2 * GRID_W + 2 * FF_PAD, FF_CW), BF16),
                        pltpu.VMEM((2, tb + 2 * GRID_W + 2 * FF_PAD, FF_CW), BF16),
                        pltpu.VMEM((2, FF_NC, GRID_W, FF_CW), BF16)],
        compiler_params=pltpu.CompilerParams(
            dimension_semantics=("arbitrary", "arbitrary"), vmem_limit_bytes=VMEM_LIMIT),
        name="conv_ffn",
    )(x, x, mod, mod, mod, w_up, conv_wb, w_down, ln_g, ln_b)


def _decay_tables(ret_decay_logit):
    lg = jax.nn.log_sigmoid(ret_decay_logit.astype(F32))
    pos = jnp.arange(CHUNK, dtype=F32)
    diff = pos[:, None] - pos[None, :]
    d_f = jnp.where(diff[None] >= 0, jnp.exp(jnp.maximum(diff, 0.0)[None] * lg[0][:, None, None]), 0.0)
    d_b = jnp.where(diff[None] <= 0, jnp.exp(jnp.maximum(-diff, 0.0)[None] * lg[1][:, None, None]), 0.0)
    dm = (d_f + d_b).reshape(PAIRS, 2, CHUNK, CHUNK).transpose(0, 2, 1, 3).reshape(PAIRS, CHUNK, 2 * CHUNK)

    def lanes(t):
        return jnp.repeat(t, DK, axis=1)

    qdf = lanes(jnp.exp((pos + 1.0)[:, None] * lg[0][None, :]))
    qdb = lanes(jnp.exp((CHUNK - pos)[:, None] * lg[1][None, :]))
    kdf = lanes(jnp.exp((CHUNK - 1.0 - pos)[:, None] * lg[0][None, :]))
    kdb = lanes(jnp.exp(pos[:, None] * lg[1][None, :]))

    def rows(t):
        return jnp.broadcast_to(jnp.repeat(t, DK)[:, None], (QK_W, DV))

    cdf = rows(jnp.exp(CHUNK * lg[0]))
    cdb = rows(jnp.exp(CHUNK * lg[1]))
    return dm, qdf, qdb, kdf, kdb, cdf, cdb


def kernel(x, c, ctx, c_ctx, w_ada, b_ada, w_in, ret_decay_logit, pool_w, pool_scale, w_branch_ret,
           w_branch_pool, w_out, ln1_g, ln1_b, w_up, conv_w, conv_b, w_down, ln2_g, ln2_b):
    B = x.shape[0]
    D = D_MODEL
    assert w_ada.shape[0] == 1, "single-layer stack"

    cc = jnp.concatenate([c, c_ctx[None, :], jnp.zeros((MOD_ROWS - B - 1, D), F32)], axis=0)
    mod = _adaln(cc, w_ada[0], b_ada[0][None, :]).reshape(MOD_ROWS, N_MOD, 1, D)

    dm, qdf, qdb, kdf, kdb, cdf, cdb = _decay_tables(ret_decay_logit[0])

    s_f, s_b, w_kv_b = _ctx_states(ctx, mod, w_in[0], kdf, kdb, cdf, cdb)
    u, kv, sb, w_in_b, w_up_b, w_down_b, w_br_b, w_bp_b, w_out_b = _kv_states(
        x, mod, w_kv_b, kdb, cdb, s_b,
        [w_in[0], w_up[0], w_down[0], w_branch_ret[0], w_branch_pool[0], w_out[0]])
    x1 = _mixer(x, u, kv, sb, s_f, mod, w_in_b, dm, qdf, qdb, kdf, cdf,
                pool_w[0].astype(BF16), pool_scale[0][None, :], w_br_b, w_bp_b, w_out_b,
                ln1_g[0][None, :], ln1_b[0][None, :])

    conv_wb = jnp.concatenate([conv_w[0].reshape(9, 2 * D_FF), conv_b[0][None, :],
                               jnp.zeros((6, 2 * D_FF), F32)], axis=0).astype(BF16)
    return _ffn(x1, mod, w_up_b, conv_wb, w_down_b.reshape(FF_NC, FF_CW, D),
                ln2_g[0][None, :], ln2_b[0][None, :])
```

```python
import functools

import jax
import jax.numpy as jnp
import numpy as np
from jax import lax
from jax.experimental import pallas as pl
from jax.experimental.pallas import tpu as pltpu

F32 = jnp.float32
BF16 = jnp.bfloat16

D_MODEL = 1024
GRID_W = 64
HEADS = 8
DK = 64
DV = 128
QK_W = HEADS * DK
V_W = HEADS * DV
KV_W = QK_W + V_W
CHUNK = 128
PAIRS = HEADS // 2
POOL_WINDOWS = (2, 4, 8, 16)
POOL_GD = 128
POOL_W = 512
COL_Q = KV_W
COL_G = COL_Q + QK_W
COL_P = COL_G + V_W
COL_GA = COL_P + POOL_W
COL_GB = COL_GA + D_MODEL
IN_W = COL_GB + D_MODEL
D_FF = 2816
FF_CW = 256
FF_NC = D_FF // FF_CW
FF_PAD = 8
N_MOD = 6
LN_EPS = 1e-6
ALPHA = 2.0 ** 0.25
POOL_HALO = 8
U_HALO = 16
MOD_ROWS = 8

VMEM_LIMIT = 60 * 1024 * 1024

MIX_RB = 256

TB_KV = 1024
TB_MIX = 1024
TB_FFN = 1024


def _dot(a, b):
    return jnp.dot(a, b, preferred_element_type=F32)


def _dot_nt(a, b):
    return lax.dot_general(a, b, (((1,), (1,)), ((), ())), preferred_element_type=F32)


def _dot_tn(a, b):
    return lax.dot_general(a, b, (((0,), (0,)), ((), ())), preferred_element_type=F32)


def _ln(x):
    mu = jnp.mean(x, axis=-1, keepdims=True)
    xc = x - mu
    var = jnp.mean(xc * xc, axis=-1, keepdims=True)
    return xc * lax.rsqrt(var + LN_EPS)


def _gelu_tanh(x):
    c = float(np.sqrt(2.0 / np.pi))
    half = 0.5 * x
    return half + half * jnp.tanh(x * (c + (c * 0.044715) * (x * x)))


def _mod_spec(k, row=None):
    if row is None:
        return pl.BlockSpec((1, 1, 1, D_MODEL), lambda b, *_: (b, k, 0, 0))
    return pl.BlockSpec((1, 1, 1, D_MODEL), lambda *_: (row, k, 0, 0))


def _const_spec(shape):
    nd = len(shape)
    return pl.BlockSpec(shape, lambda *_: (0,) * nd, pipeline_mode=pl.Buffered(1))


def _pair_diag(r):
    row = lax.broadcasted_iota(jnp.int32, (CHUNK, DV), 0)
    return jnp.where(row < DK, r[:, :DV], r[:, DV:])


def _chunk_kv(kd, v):
    outs = []
    for p in range(PAIRS):
        r = _dot_tn(kd[:, p * 128:(p + 1) * 128], v[:, p * 256:(p + 1) * 256])
        outs.append(_pair_diag(r))
    return jnp.concatenate(outs, axis=0)


def _adaln_kernel(c_ref, w_ref, b_ref, o_ref):
    c = c_ref[...]
    s = c * jax.nn.sigmoid(c)
    o_ref[...] = _dot(s, w_ref[...]) + b_ref[...]


def _adaln(cc, w_ada, b_ada):
    n = w_ada.shape[1]
    bn = 1536
    return pl.pallas_call(
        _adaln_kernel,
        grid=(n // bn,),
        in_specs=[pl.BlockSpec((MOD_ROWS, D_MODEL), lambda i: (0, 0)),
                  pl.BlockSpec((D_MODEL, bn), lambda i: (0, i)),
                  pl.BlockSpec((1, bn), lambda i: (0, i))],
        out_specs=pl.BlockSpec((MOD_ROWS, bn), lambda i: (0, i)),
        out_shape=jax.ShapeDtypeStruct((MOD_ROWS, n), F32),
        compiler_params=pltpu.CompilerParams(vmem_limit_bytes=VMEM_LIMIT),
        name="adaln",
    )(cc, w_ada, b_ada)


def _ctx_kernel(ctx_ref, sh_ref, sc_ref, wkv_ref, kdf_ref, kdb_ref, cdf_ref, cdb_ref,
                sf_ref, sb_ref, wkvb_ref):
    x = ctx_ref[0]
    u = (_ln(x) * (1.0 + sc_ref[0, 0]) + sh_ref[0, 0]).astype(BF16)
    wkv = wkv_ref[...].astype(BF16)
    wkvb_ref[...] = wkv
    kv = _dot(u, wkv)
    k = kv[:, :QK_W] * (DK ** -0.5)
    v = kv[:, QK_W:].astype(BF16)
    n = x.shape[0] // CHUNK
    sf = jnp.zeros((QK_W, DV), F32)
    for c in range(n):
        kc = k[c * CHUNK:(c + 1) * CHUNK]
        vc = v[c * CHUNK:(c + 1) * CHUNK]
        sf = cdf_ref[...] * sf + _chunk_kv((kc * kdf_ref[...]).astype(BF16), vc)
    sb = jnp.zeros((QK_W, DV), F32)
    for c in reversed(range(n)):
        kc = k[c * CHUNK:(c + 1) * CHUNK]
        vc = v[c * CHUNK:(c + 1) * CHUNK]
        sb = cdb_ref[...] * sb + _chunk_kv((kc * kdb_ref[...]).astype(BF16), vc)
    sf_ref[0] = sf
    sb_ref[0] = sb


def _ctx_states(ctx, mod, w_kv, kdf, kdb, cdf, cdb):
    B, Lc, _ = ctx.shape
    st = jax.ShapeDtypeStruct((B, QK_W, DV), F32)
    return pl.pallas_call(
        _ctx_kernel,
        grid=(B,),
        in_specs=[pl.BlockSpec((1, Lc, D_MODEL), lambda b: (b, 0, 0)),
                  _mod_spec(0, row=B), _mod_spec(1, row=B),
                  _const_spec((D_MODEL, KV_W)),
                  _const_spec((CHUNK, QK_W)), _const_spec((CHUNK, QK_W)),
                  _const_spec((QK_W, DV)), _const_spec((QK_W, DV))],
        out_specs=[pl.BlockSpec((1, QK_W, DV), lambda b: (b, 0, 0)),
                   pl.BlockSpec((1, QK_W, DV), lambda b: (b, 0, 0)),
                   pl.BlockSpec((D_MODEL, KV_W), lambda b: (0, 0))],
        out_shape=[st, st, jax.ShapeDtypeStruct((D_MODEL, KV_W), BF16)],
        compiler_params=pltpu.CompilerParams(
            dimension_semantics=("arbitrary",), vmem_limit_bytes=VMEM_LIMIT),
        name="ctx_states",
    )(ctx, mod, mod, w_kv, kdf, kdb, cdf, cdb)


def _kv_kernel(x_ref, sh_ref, sc_ref, wkv_ref, kdb_ref, cdb_ref, sb0_ref, *rest, n_cast):
    cast_in, (u_ref, kv_ref, sb_ref), cast_out, (s_ref,) = (
        rest[:n_cast], rest[n_cast:n_cast + 3], rest[n_cast + 3:2 * n_cast + 3], rest[2 * n_cast + 3:])

    @pl.when(pl.program_id(1) == 0)
    def _():
        s_ref[...] = sb0_ref[0]

    for src, dst in zip(cast_in, cast_out):
        dst[...] = src[...].astype(BF16)

    x = x_ref[0]
    u = (_ln(x) * (1.0 + sc_ref[0, 0]) + sh_ref[0, 0]).astype(BF16)
    u_ref[0] = u
    kv = _dot(u, wkv_ref[...])
    k = kv[:, :QK_W] * (DK ** -0.5)
    v = kv[:, QK_W:].astype(BF16)
    kv_ref[0, :, :QK_W] = k.astype(BF16)
    kv_ref[0, :, QK_W:] = v
    n = x.shape[0] // CHUNK
    for c in reversed(range(n)):
        s = s_ref[...]
        sb_ref[0, c] = s.astype(BF16)
        kc = k[c * CHUNK:(c + 1) * CHUNK]
        vc = v[c * CHUNK:(c + 1) * CHUNK]
        s_ref[...] = cdb_ref[...] * s + _chunk_kv((kc * kdb_ref[...]).astype(BF16), vc)


def _kv_states(x, mod, w_kv, kdb, cdb, sb0, later_weights):
    B, L, _ = x.shape
    tb = TB_KV
    nb = L // tb
    nch = tb // CHUNK
    steps = B * nb
    slab_specs = [pl.BlockSpec((w.shape[0] // steps, w.shape[1]), lambda b, j: (b * nb + j, 0))
                  for w in later_weights]
    for w in later_weights:
        assert w.shape[0] % (16 * steps) == 0, "row slabs must be whole bf16 tiles"
    return pl.pallas_call(
        functools.partial(_kv_kernel, n_cast=len(later_weights)),
        grid=(B, nb),
        in_specs=[pl.BlockSpec((1, tb, D_MODEL), lambda b, j: (b, nb - 1 - j, 0)),
                  _mod_spec(0), _mod_spec(1),
                  _const_spec((D_MODEL, KV_W)),
                  _const_spec((CHUNK, QK_W)),
                  _const_spec((QK_W, DV)),
                  pl.BlockSpec((1, QK_W, DV), lambda b, j: (b, 0, 0))] + slab_specs,
        out_specs=[pl.BlockSpec((1, tb, D_MODEL), lambda b, j: (b, nb - 1 - j, 0)),
                   pl.BlockSpec((1, tb, KV_W), lambda b, j: (b, nb - 1 - j, 0)),
                   pl.BlockSpec((1, nch, QK_W, DV), lambda b, j: (b, nb - 1 - j, 0, 0))] + slab_specs,
        out_shape=[jax.ShapeDtypeStruct((B, L, D_MODEL), BF16),
                   jax.ShapeDtypeStruct((B, L, KV_W), BF16),
                   jax.ShapeDtypeStruct((B, L // CHUNK, QK_W, DV), BF16)]
        + [jax.ShapeDtypeStruct(w.shape, BF16) for w in later_weights],
        scratch_shapes=[pltpu.VMEM((QK_W, DV), F32)],
        compiler_params=pltpu.CompilerParams(
            dimension_semantics=("arbitrary", "arbitrary"), vmem_limit_bytes=VMEM_LIMIT),
        name="kv_states",
    )(x, mod, mod, w_kv, kdb, cdb, sb0, *later_weights)


def _pool_features(pe, j, nb, tb, seq_len, poolw_ref, pscale_ref):
    n = pe.shape[0]
    r = lax.broadcasted_iota(jnp.int32, (POOL_HALO, POOL_GD), 0)
    outs = []
    for gi, w in enumerate(POOL_WINDOWS):
        half = w // 2
        a = pe[:, gi * POOL_GD:(gi + 1) * POOL_GD]
        centre = a[POOL_HALO:POOL_HALO + tb]
        s = a
        step = 1
        while step < w:
            s = s + pltpu.roll(s, n - step, axis=0)
            step *= 2
        s = pltpu.roll(s, half, axis=0)[POOL_HALO:POOL_HALO + tb]
        cnt_head = (jnp.minimum(r + half, seq_len) - jnp.maximum(r - half, 0)).astype(F32)
        t_tail = seq_len - POOL_HALO + r
        cnt_tail = (jnp.minimum(t_tail + half, seq_len) - jnp.maximum(t_tail - half, 0)).astype(F32)
        head = jnp.where(j == 0, 1.0 / cnt_head, 1.0 / w)
        tail = jnp.where(j == nb - 1, 1.0 / cnt_tail, 1.0 / w)
        mean = jnp.concatenate([s[:POOL_HALO] * head, s[POOL_HALO:tb - POOL_HALO] * (1.0 / w),
                                s[tb - POOL_HALO:] * tail], axis=0)
        diff = (mean - centre).astype(BF16)
        outs.append(_dot(diff, poolw_ref[gi]))
    return jnp.concatenate(outs, axis=-1) * pscale_ref[...]


def _mixer_kernel(x_ref, u_ref, up_ref, un_ref, kv_ref, sb_ref, sf0_ref, g1_ref,
                  wr_ref, dm_ref, qdf_ref, qdb_ref, kdf_ref, cdf_ref,
                  poolw_ref, pscale_ref, wbr_ref, wbp_ref, wout_ref, lng_ref, lnb_ref,
                  o_ref, s_ref, r_ref, *, nb, seq_len):
    j = pl.program_id(1)
    tb = x_ref.shape[1]

    @pl.when(j == 0)
    def _():
        s_ref[...] = sf0_ref[0]

    x = x_ref[0]
    u = u_ref[0]

    q = _dot(u, wr_ref[:, COL_Q:COL_G])
    g = _dot(u, wr_ref[:, COL_G:COL_P])

    lane = lax.broadcasted_iota(jnp.int32, (CHUNK, 128), 1)
    zv = jnp.zeros((CHUNK, DV), BF16)
    zs = jnp.zeros((DK, DV), BF16)

    def block_diag_state(s):
        left = jnp.concatenate([s[:DK], zs], axis=0)
        right = jnp.concatenate([zs, s[DK:]], axis=0)
        return jnp.concatenate([left, right], axis=1)

    for c in range(tb // CHUNK):
        rows = slice(c * CHUNK, (c + 1) * CHUNK)
        qc = q[rows]
        qb16 = qc.astype(BF16)
        qf = (qc * qdf_ref[...]).astype(BF16)
        qb = (qc * qdb_ref[...]).astype(BF16)
        kc = kv_ref[0, rows, 0:QK_W].astype(F32)
        vc = kv_ref[0, rows, QK_W:KV_W]
        kd = (kc * kdf_ref[...]).astype(BF16)
        ys = []
        for p in range(PAIRS):
            ql = slice(p * 128, (p + 1) * 128)
            kp = kc[:, ql]
            k_lo = jnp.where(lane < DK, kp, 0.0).astype(BF16)
            k_hi = jnp.where(lane >= DK, kp, 0.0).astype(BF16)
            krhs = jnp.concatenate([k_lo, k_hi], axis=0)
            sc = (_dot_nt(qb16[:, ql], krhs) * dm_ref[p]).astype(BF16)
            vp = vc[:, p * 256:(p + 1) * 256]
            vrhs = jnp.concatenate([jnp.concatenate([vp[:, :DV], zv], axis=1),
                                    jnp.concatenate([zv, vp[:, DV:]], axis=1)], axis=0)
            y = _dot(sc, vrhs)
            s_f = s_ref[ql, :]
            srhs = jnp.concatenate([block_diag_state(s_f.astype(BF16)),
                                    block_diag_state(sb_ref[0, c, ql, :])], axis=0)
            qlhs = jnp.concatenate([qf[:, ql], qb[:, ql]], axis=1)
            y = y + _dot(qlhs, srhs)
            ys.append(y)
            s_ref[ql, :] = cdf_ref[ql, :] * s_f + _pair_diag(_dot_tn(kd[:, ql], vp))
        gc = g[rows]
        sg = gc * jax.nn.sigmoid(gc)
        for p in range(PAIRS):
            for hh in range(2):
                h = 2 * p + hh
                yh = ys[p][:, hh * DV:(hh + 1) * DV]
                mu = jnp.mean(yh, axis=-1, keepdims=True)
                yc = yh - mu
                var = jnp.mean(yc * yc, axis=-1, keepdims=True)
                yn = yc * lax.rsqrt(var + LN_EPS)
                r_ref[rows, h * DV:(h + 1) * DV] = (yn * sg[:, h * DV:(h + 1) * DV]).astype(BF16)

    ret = _dot(r_ref[...], wbr_ref[...])

    ph = _dot(jnp.concatenate([up_ref[0], un_ref[0]], axis=0), wr_ref[:, COL_P:COL_GA])
    p_prev = ph[U_HALO - POOL_HALO:U_HALO] * (j > 0).astype(F32)
    p_next = ph[U_HALO:U_HALO + POOL_HALO] * (j < nb - 1).astype(F32)
    pm = _dot(u, wr_ref[:, COL_P:COL_GA])
    pe = jnp.concatenate([p_prev, pm, p_next], axis=0)
    feat = _pool_features(pe, j, nb, tb, seq_len, poolw_ref, pscale_ref).astype(BF16)
    pool = _dot(feat, wbp_ref[...])

    ga = _dot(u, wr_ref[:, COL_GA:COL_GB])
    merged = jax.nn.sigmoid(ga) * ret
    gb = _dot(u, wr_ref[:, COL_GB:IN_W])
    merged = (merged + jax.nn.sigmoid(gb) * pool).astype(BF16)
    for r0 in range(0, tb, MIX_RB):
        rs = slice(r0, r0 + MIX_RB)
        z = ALPHA * x[rs] + g1_ref[0, 0] * _dot(merged[rs], wout_ref[...])
        o_ref[0, rs] = _ln(z) * lng_ref[...] + lnb_ref[...]


def _mixer(x, u, kv, sb, sf0, mod, w_rest, dm, qdf, qdb, kdf, cdf,
           pool_w, pool_scale, w_br, w_bp, w_out, ln_g, ln_b):
    B, L, _ = x.shape
    tb = TB_MIX
    nb = L // tb
    nch = tb // CHUNK
    hb = tb // U_HALO
    nh = L // U_HALO
    return pl.pallas_call(
        functools.partial(_mixer_kernel, nb=nb, seq_len=L),
        grid=(B, nb),
        in_specs=[pl.BlockSpec((1, tb, D_MODEL), lambda b, j: (b, j, 0)),
                  pl.BlockSpec((1, tb, D_MODEL), lambda b, j: (b, j, 0)),
                  pl.BlockSpec((1, U_HALO, D_MODEL), lambda b, j: (b, jnp.maximum(j * hb - 1, 0), 0)),
                  pl.BlockSpec((1, U_HALO, D_MODEL),
                               lambda b, j: (b, jnp.minimum((j + 1) * hb, nh - 1), 0)),
                  pl.BlockSpec((1, tb, KV_W), lambda b, j: (b, j, 0)),
                  pl.BlockSpec((1, nch, QK_W, DV), lambda b, j: (b, j, 0, 0)),
                  pl.BlockSpec((1, QK_W, DV), lambda b, j: (b, 0, 0)),
                  _mod_spec(2),
                  _const_spec((D_MODEL, IN_W)),
                  _const_spec((PAIRS, CHUNK, 256)),
                  _const_spec((CHUNK, QK_W)), _const_spec((CHUNK, QK_W)), _const_spec((CHUNK, QK_W)),
                  _const_spec((QK_W, DV)),
                  _const_spec((len(POOL_WINDOWS), POOL_GD, POOL_GD)),
                  _const_spec((1, POOL_W)),
                  _const_spec((V_W, D_MODEL)),
                  _const_spec((POOL_W, D_MODEL)),
                  _const_spec((D_MODEL, D_MODEL)),
                  _const_spec((1, D_MODEL)), _const_spec((1, D_MODEL))],
        out_specs=pl.BlockSpec((1, tb, D_MODEL), lambda b, j: (b, j, 0)),
        out_shape=jax.ShapeDtypeStruct((B, L, D_MODEL), F32),
        scratch_shapes=[pltpu.VMEM((QK_W, DV), F32), pltpu.VMEM((tb, V_W), BF16)],
        compiler_params=pltpu.CompilerParams(
            dimension_semantics=("arbitrary", "arbitrary"), vmem_limit_bytes=VMEM_LIMIT),
        name="mixer",
    )(x, u, u, u, kv, sb, sf0, mod, w_rest, dm, qdf, qdb, kdf, cdf,
      pool_w, pool_scale, w_br, w_bp, w_out, ln_g, ln_b)


def _ffn_kernel(x_ref, xn_ref, sh_ref, sc_ref, g2_ref, wup_ref, cw_ref, wdn_ref,
                lng_ref, lnb_ref, o_ref, u_ref, acc_ref, ha_ref, hb_ref, top_ref, *, nb):
    j = pl.program_id(1)
    tb = x_ref.shape[1]
    m = tb + GRID_W
    n = tb + 2 * GRID_W
    scale = 1.0 + sc_ref[0, 0]
    shift = sh_ref[0, 0]
    x = x_ref[0]

    def mod(v):
        return _ln(v) * scale + shift

    @pl.when(j == 0)
    def _():
        top_ref[...] = jnp.zeros_like(top_ref)

    u_ref[0:tb] = mod(x).astype(BF16)
    u_ref[tb:m] = (mod(xn_ref[0]) * (j < nb - 1).astype(F32)).astype(BF16)

    col = lax.broadcasted_iota(jnp.int32, (tb, FF_CW), 0) & (GRID_W - 1)
    has_left = col > 0
    has_right = col < GRID_W - 1

    pad = jnp.zeros((FF_PAD, FF_CW), F32)
    for h_ref in (ha_ref, hb_ref):
        for ab in range(2):
            h_ref[ab, 0:FF_PAD] = pad
            h_ref[ab, FF_PAD + n:FF_PAD + n + FF_PAD] = pad

    def lanes(c, half):
        return pl.ds(pl.multiple_of(half * D_FF + c * FF_CW, FF_CW), FF_CW)

    def up(c, h_ref):
        u = u_ref[...]
        for half in range(2):
            h = _dot(u, wup_ref[:, lanes(c, half)])
            h_ref[half, FF_PAD:FF_PAD + GRID_W] = top_ref[half, c]
            h_ref[half, FF_PAD + GRID_W:FF_PAD + n] = h
            top_ref[half, c] = h[tb - GRID_W:tb]

    def conv(h_ref, ab, cw):
        cols = []
        for dc in range(3):
            hs = h_ref[ab, FF_PAD + dc - 1:FF_PAD + dc - 1 + n].astype(BF16)
            g = None
            for dr in range(3):
                term = cw[3 * dr + dc:3 * dr + dc + 1] * hs[dr * GRID_W:dr * GRID_W + tb]
                g = term if g is None else g + term
            cols.append(g)
        zero = jnp.zeros_like(cols[1])
        return (cols[1] + cw[9:10]) + (jnp.where(has_left, cols[0], zero) + jnp.where(has_right, cols[2], zero))

    def down(c, h_ref):
        a = conv(h_ref, 0, cw_ref[:, lanes(c, 0)])
        b = conv(h_ref, 1, cw_ref[:, lanes(c, 1)])
        acc_ref[...] += _dot(_gelu_tanh(a.astype(F32)).astype(BF16) * b, wdn_ref[c])

    acc_ref[...] = jnp.zeros_like(acc_ref)
    up(0, ha_ref)

    def body(i, carry):
        c = 2 * i
        up(c + 1, hb_ref)
        down(c, ha_ref)
        up(c + 2, ha_ref)
        down(c + 1, hb_ref)
        return carry

    lax.fori_loop(0, FF_NC // 2, body, 0)
    down(FF_NC - 1, ha_ref)
    z = ALPHA * x + g2_ref[0, 0] * acc_ref[...]
    o_ref[0] = _ln(z) * lng_ref[...] + lnb_ref[...]


def _ffn(x, mod, w_up, conv_wb, w_down, ln_g, ln_b):
    B, L, _ = x.shape
    tb = TB_FFN
    nb = L // tb
    hb = tb // GRID_W
    nh = L // GRID_W
    return pl.pallas_call(
        functools.partial(_ffn_kernel, nb=nb),
        grid=(B, nb),
        in_specs=[pl.BlockSpec((1, tb, D_MODEL), lambda b, j: (b, j, 0)),
                  pl.BlockSpec((1, GRID_W, D_MODEL),
                               lambda b, j: (b, jnp.minimum((j + 1) * hb, nh - 1), 0)),
                  _mod_spec(3), _mod_spec(4), _mod_spec(5),
                  _const_spec((D_MODEL, 2 * D_FF)),
                  _const_spec((16, 2 * D_FF)),
                  _const_spec((FF_NC, FF_CW, D_MODEL)),
                  _const_spec((1, D_MODEL)), _const_spec((1, D_MODEL))],
        out_specs=pl.BlockSpec((1, tb, D_MODEL), lambda b, j: (b, j, 0)),
        out_shape=jax.ShapeDtypeStruct((B, L, D_MODEL), F32),
        scratch_shapes=[pltpu.VMEM((tb + GRID_W, D_MODEL), BF16),
                        pltpu.VMEM((tb, D_MODEL), F32),
                        pltpu.VMEM((2, tb + 2 * GRID_W + 2 * FF_PAD, FF_CW), F32),
                        pltpu.VMEM((2, tb + 2 * GRID_W + 2 * FF_PAD, FF_CW), F32),
                        pltpu.VMEM((2, FF_NC, GRID_W, FF_CW), F32)],
        compiler_params=pltpu.CompilerParams(
            dimension_semantics=("arbitrary", "arbitrary"), vmem_limit_bytes=VMEM_LIMIT),
        name="conv_ffn",
    )(x, x, mod, mod, mod, w_up, conv_wb, w_down, ln_g, ln_b)


def _decay_tables(ret_decay_logit):
    lg = jax.nn.log_sigmoid(ret_decay_logit.astype(F32))
    pos = jnp.arange(CHUNK, dtype=F32)
    diff = pos[:, None] - pos[None, :]
    d_f = jnp.where(diff[None] >= 0, jnp.exp(jnp.maximum(diff, 0.0)[None] * lg[0][:, None, None]), 0.0)
    d_b = jnp.where(diff[None] <= 0, jnp.exp(jnp.maximum(-diff, 0.0)[None] * lg[1][:, None, None]), 0.0)
    dm = (d_f + d_b).reshape(PAIRS, 2, CHUNK, CHUNK).transpose(0, 2, 1, 3).reshape(PAIRS, CHUNK, 2 * CHUNK)

    def lanes(t):
        return jnp.repeat(t, DK, axis=1)

    qdf = lanes(jnp.exp((pos + 1.0)[:, None] * lg[0][None, :]))
    qdb = lanes(jnp.exp((CHUNK - pos)[:, None] * lg[1][None, :]))
    kdf = lanes(jnp.exp((CHUNK - 1.0 - pos)[:, None] * lg[0][None, :]))
    kdb = lanes(jnp.exp(pos[:, None] * lg[1][None, :]))

    def rows(t):
        return jnp.broadcast_to(jnp.repeat(t, DK)[:, None], (QK_W, DV))

    cdf = rows(jnp.exp(CHUNK * lg[0]))
    cdb = rows(jnp.exp(CHUNK * lg[1]))
    return dm, qdf, qdb, kdf, kdb, cdf, cdb


def kernel(x, c, ctx, c_ctx, w_ada, b_ada, w_in, ret_decay_logit, pool_w, pool_scale, w_branch_ret,
           w_branch_pool, w_out, ln1_g, ln1_b, w_up, conv_w, conv_b, w_down, ln2_g, ln2_b):
    B = x.shape[0]
    D = D_MODEL
    assert w_ada.shape[0] == 1, "single-layer stack"

    cc = jnp.concatenate([c, c_ctx[None, :], jnp.zeros((MOD_ROWS - B - 1, D), F32)], axis=0)
    mod = _adaln(cc, w_ada[0], b_ada[0][None, :]).reshape(MOD_ROWS, N_MOD, 1, D)

    dm, qdf, qdb, kdf, kdb, cdf, cdb = _decay_tables(ret_decay_logit[0])

    s_f, s_b, w_kv_b = _ctx_states(ctx, mod, w_in[0], kdf, kdb, cdf, cdb)
    u, kv, sb, w_in_b, w_up_b, w_down_b, w_br_b, w_bp_b, w_out_b = _kv_states(
        x, mod, w_kv_b, kdb, cdb, s_b,
        [w_in[0], w_up[0], w_down[0], w_branch_ret[0], w_branch_pool[0], w_out[0]])
    x1 = _mixer(x, u, kv, sb, s_f, mod, w_in_b, dm, qdf, qdb, kdf, cdf,
                pool_w[0].astype(BF16), pool_scale[0][None, :], w_br_b, w_bp_b, w_out_b,
                ln1_g[0][None, :], ln1_b[0][None, :])

    conv_wb = jnp.concatenate([conv_w[0].reshape(9, 2 * D_FF), conv_b[0][None, :],
                               jnp.zeros((6, 2 * D_FF), F32)], axis=0).astype(BF16)
    return _ffn(x1, mod, w_up_b, conv_wb, w_down_b.reshape(FF_NC, FF_CW, D),
                ln2_g[0][None, :], ln2_b[0][None, :])
```

```python
import functools

import jax
import jax.numpy as jnp
import numpy as np
from jax import lax
from jax.experimental import pallas as pl
from jax.experimental.pallas import tpu as pltpu

F32 = jnp.float32
BF16 = jnp.bfloat16

D_MODEL = 1024
GRID_W = 64
HEADS = 8
DK = 64
DV = 128
QK_W = HEADS * DK
V_W = HEADS * DV
KV_W = QK_W + V_W
CHUNK = 128
PAIRS = HEADS // 2
POOL_WINDOWS = (2, 4, 8, 16)
POOL_GD = 128
POOL_W = 512
COL_Q = KV_W
COL_G = COL_Q + QK_W
COL_P = COL_G + V_W
COL_GA = COL_P + POOL_W
COL_GB = COL_GA + D_MODEL
IN_W = COL_GB + D_MODEL
D_FF = 2816
FF_CW = 256
FF_NC = D_FF // FF_CW
FF_PAD = 8
N_MOD = 6
LN_EPS = 1e-6
ALPHA = 2.0 ** 0.25
POOL_HALO = 8
U_HALO = 16
MOD_ROWS = 8

VMEM_LIMIT = 60 * 1024 * 1024

MIX_RB = 256

TB_KV = 1024
TB_MIX = 1024
TB_FFN = 1024


def _dot(a, b):
    return jnp.dot(a, b, preferred_element_type=F32)


def _dot_nt(a, b):
    return lax.dot_general(a, b, (((1,), (1,)), ((), ())), preferred_element_type=F32)


def _dot_tn(a, b):
    return lax.dot_general(a, b, (((0,), (0,)), ((), ())), preferred_element_type=F32)


def _ln(x):
    mu = jnp.mean(x, axis=-1, keepdims=True)
    xc = x - mu
    var = jnp.mean(xc * xc, axis=-1, keepdims=True)
    return xc * lax.rsqrt(var + LN_EPS)


def _gelu_tanh(x):
    c = float(np.sqrt(2.0 / np.pi))
    half = 0.5 * x
    return half + half * jnp.tanh(x * (c + (c * 0.044715) * (x * x)))


def _mod_spec(k, row=None):
    if row is None:
        return pl.BlockSpec((1, 1, 1, D_MODEL), lambda b, *_: (b, k, 0, 0))
    return pl.BlockSpec((1, 1, 1, D_MODEL), lambda *_: (row, k, 0, 0))


def _const_spec(shape):
    nd = len(shape)
    return pl.BlockSpec(shape, lambda *_: (0,) * nd, pipeline_mode=pl.Buffered(1))


def _pair_diag(r):
    row = lax.broadcasted_iota(jnp.int32, (CHUNK, DV), 0)
    return jnp.where(row < DK, r[:, :DV], r[:, DV:])


def _chunk_kv(kd, v):
    outs = []
    for p in range(PAIRS):
        r = _dot_tn(kd[:, p * 128:(p + 1) * 128], v[:, p * 256:(p + 1) * 256])
        outs.append(_pair_diag(r))
    return jnp.concatenate(outs, axis=0)


def _adaln_kernel(c_ref, w_ref, b_ref, o_ref):
    c = c_ref[...]
    s = c * jax.nn.sigmoid(c)
    o_ref[...] = _dot(s, w_ref[...]) + b_ref[...]


def _adaln(cc, w_ada, b_ada):
    n = w_ada.shape[1]
    bn = 1536
    return pl.pallas_call(
        _adaln_kernel,
        grid=(n // bn,),
        in_specs=[pl.BlockSpec((MOD_ROWS, D_MODEL), lambda i: (0, 0)),
                  pl.BlockSpec((D_MODEL, bn), lambda i: (0, i)),
                  pl.BlockSpec((1, bn), lambda i: (0, i))],
        out_specs=pl.BlockSpec((MOD_ROWS, bn), lambda i: (0, i)),
        out_shape=jax.ShapeDtypeStruct((MOD_ROWS, n), F32),
        compiler_params=pltpu.CompilerParams(vmem_limit_bytes=VMEM_LIMIT),
        name="adaln",
    )(cc, w_ada, b_ada)


def _ctx_kernel(ctx_ref, sh_ref, sc_ref, wkv_ref, kdf_ref, kdb_ref, cdf_ref, cdb_ref,
                sf_ref, sb_ref, wkvb_ref):
    x = ctx_ref[0]
    u = (_ln(x) * (1.0 + sc_ref[0, 0]) + sh_ref[0, 0]).astype(BF16)
    wkv = wkv_ref[...].astype(BF16)
    wkvb_ref[...] = wkv
    kv = _dot(u, wkv)
    k = kv[:, :QK_W] * (DK ** -0.5)
    v = kv[:, QK_W:].astype(BF16)
    n = x.shape[0] // CHUNK
    sf = jnp.zeros((QK_W, DV), F32)
    for c in range(n):
        kc = k[c * CHUNK:(c + 1) * CHUNK]
        vc = v[c * CHUNK:(c + 1) * CHUNK]
        sf = cdf_ref[...] * sf + _chunk_kv((kc * kdf_ref[...]).astype(BF16), vc)
    sb = jnp.zeros((QK_W, DV), F32)
    for c in reversed(range(n)):
        kc = k[c * CHUNK:(c + 1) * CHUNK]
        vc = v[c * CHUNK:(c + 1) * CHUNK]
        sb = cdb_ref[...] * sb + _chunk_kv((kc * kdb_ref[...]).astype(BF16), vc)
    sf_ref[0] = sf
    sb_ref[0] = sb


def _ctx_states(ctx, mod, w_kv, kdf, kdb, cdf, cdb):
    B, Lc, _ = ctx.shape
    st = jax.ShapeDtypeStruct((B, QK_W, DV), F32)
    return pl.pallas_call(
        _ctx_kernel,
        grid=(B,),
        in_specs=[pl.BlockSpec((1, Lc, D_MODEL), lambda b: (b, 0, 0)),
                  _mod_spec(0, row=B), _mod_spec(1, row=B),
                  _const_spec((D_MODEL, KV_W)),
                  _const_spec((CHUNK, QK_W)), _const_spec((CHUNK, QK_W)),
                  _const_spec((QK_W, DV)), _const_spec((QK_W, DV))],
        out_specs=[pl.BlockSpec((1, QK_W, DV), lambda b: (b, 0, 0)),
                   pl.BlockSpec((1, QK_W, DV), lambda b: (b, 0, 0)),
                   pl.BlockSpec((D_MODEL, KV_W), lambda b: (0, 0))],
        out_shape=[st, st, jax.ShapeDtypeStruct((D_MODEL, KV_W), BF16)],
        compiler_params=pltpu.CompilerParams(
            dimension_semantics=("arbitrary",), vmem_limit_bytes=VMEM_LIMIT),
        name="ctx_states",
    )(ctx, mod, mod, w_kv, kdf, kdb, cdf, cdb)


def _kv_kernel(x_ref, sh_ref, sc_ref, wkv_ref, kdb_ref, cdb_ref, sb0_ref, *rest, n_cast):
    cast_in, (u_ref, kv_ref, sb_ref), cast_out, (s_ref,) = (
        rest[:n_cast], rest[n_cast:n_cast + 3], rest[n_cast + 3:2 * n_cast + 3], rest[2 * n_cast + 3:])

    @pl.when(pl.program_id(1) == 0)
    def _():
        s_ref[...] = sb0_ref[0]

    for src, dst in zip(cast_in, cast_out):
        dst[...] = src[...].astype(BF16)

    x = x_ref[0]
    u = (_ln(x) * (1.0 + sc_ref[0, 0]) + sh_ref[0, 0]).astype(BF16)
    u_ref[0] = u
    kv = _dot(u, wkv_ref[...])
    k = kv[:, :QK_W] * (DK ** -0.5)
    v = kv[:, QK_W:].astype(BF16)
    kv_ref[0, :, :QK_W] = k.astype(BF16)
    kv_ref[0, :, QK_W:] = v
    n = x.shape[0] // CHUNK
    for c in reversed(range(n)):
        s = s_ref[...]
        sb_ref[0, c] = s.astype(BF16)
        kc = k[c * CHUNK:(c + 1) * CHUNK]
        vc = v[c * CHUNK:(c + 1) * CHUNK]
        s_ref[...] = cdb_ref[...] * s + _chunk_kv((kc * kdb_ref[...]).astype(BF16), vc)


def _kv_states(x, mod, w_kv, kdb, cdb, sb0, later_weights):
    B, L, _ = x.shape
    tb = TB_KV
    nb = L // tb
    nch = tb // CHUNK
    steps = B * nb
    slab_specs = [pl.BlockSpec((w.shape[0] // steps, w.shape[1]), lambda b, j: (b * nb + j, 0))
                  for w in later_weights]
    for w in later_weights:
        assert w.shape[0] % (16 * steps) == 0, "row slabs must be whole bf16 tiles"
    return pl.pallas_call(
        functools.partial(_kv_kernel, n_cast=len(later_weights)),
        grid=(B, nb),
        in_specs=[pl.BlockSpec((1, tb, D_MODEL), lambda b, j: (b, nb - 1 - j, 0)),
                  _mod_spec(0), _mod_spec(1),
                  _const_spec((D_MODEL, KV_W)),
                  _const_spec((CHUNK, QK_W)),
                  _const_spec((QK_W, DV)),
                  pl.BlockSpec((1, QK_W, DV), lambda b, j: (b, 0, 0))] + slab_specs,
        out_specs=[pl.BlockSpec((1, tb, D_MODEL), lambda b, j: (b, nb - 1 - j, 0)),
                   pl.BlockSpec((1, tb, KV_W), lambda b, j: (b, nb - 1 - j, 0)),
                   pl.BlockSpec((1, nch, QK_W, DV), lambda b, j: (b, nb - 1 - j, 0, 0))] + slab_specs,
        out_shape=[jax.ShapeDtypeStruct((B, L, D_MODEL), BF16),
                   jax.ShapeDtypeStruct((B, L, KV_W), BF16),
                   jax.ShapeDtypeStruct((B, L // CHUNK, QK_W, DV), BF16)]
        + [jax.ShapeDtypeStruct(w.shape, BF16) for w in later_weights],
        scratch_shapes=[pltpu.VMEM((QK_W, DV), F32)],
        compiler_params=pltpu.CompilerParams(
            dimension_semantics=("arbitrary", "arbitrary"), vmem_limit_bytes=VMEM_LIMIT),
        name="kv_states",
    )(x, mod, mod, w_kv, kdb, cdb, sb0, *later_weights)


def _pool_features(pe, j, nb, tb, seq_len, poolw_ref, pscale_ref):
    n = pe.shape[0]
    r = lax.broadcasted_iota(jnp.int32, (POOL_HALO, POOL_GD), 0)
    outs = []
    for gi, w in enumerate(POOL_WINDOWS):
        half = w // 2
        a = pe[:, gi * POOL_GD:(gi + 1) * POOL_GD]
        centre = a[POOL_HALO:POOL_HALO + tb]
        s = a
        step = 1
        while step < half:
            s = s + pltpu.roll(s, n - step, axis=0)
            step *= 2
        s = (s + pltpu.roll(s, half, axis=0))[POOL_HALO:POOL_HALO + tb]
        cnt_head = (jnp.minimum(r + half, seq_len) - jnp.maximum(r - half, 0)).astype(F32)
        t_tail = seq_len - POOL_HALO + r
        cnt_tail = (jnp.minimum(t_tail + half, seq_len) - jnp.maximum(t_tail - half, 0)).astype(F32)
        head = jnp.where(j == 0, 1.0 / cnt_head, 1.0 / w)
        tail = jnp.where(j == nb - 1, 1.0 / cnt_tail, 1.0 / w)
        mean = jnp.concatenate([s[:POOL_HALO] * head, s[POOL_HALO:tb - POOL_HALO] * (1.0 / w),
                                s[tb - POOL_HALO:] * tail], axis=0)
        diff = (mean - centre).astype(BF16)
        outs.append(_dot(diff, poolw_ref[gi]))
    return jnp.concatenate(outs, axis=-1) * pscale_ref[...]


def _mixer_kernel(x_ref, u_ref, up_ref, un_ref, kv_ref, sb_ref, sf0_ref, g1_ref,
                  wr_ref, dm_ref, qdf_ref, qdb_ref, kdf_ref, cdf_ref,
                  poolw_ref, pscale_ref, wbr_ref, wbp_ref, wout_ref, lng_ref, lnb_ref,
                  o_ref, s_ref, r_ref, *, nb, seq_len):
    j = pl.program_id(1)
    tb = x_ref.shape[1]

    @pl.when(j == 0)
    def _():
        s_ref[...] = sf0_ref[0]

    x = x_ref[0]
    u = u_ref[0]

    q = _dot(u, wr_ref[:, COL_Q:COL_G])
    g = _dot(u, wr_ref[:, COL_G:COL_P])

    lane = lax.broadcasted_iota(jnp.int32, (CHUNK, 128), 1)
    zv = jnp.zeros((CHUNK, DV), BF16)
    zs = jnp.zeros((DK, DV), BF16)

    def block_diag_state(s):
        left = jnp.concatenate([s[:DK], zs], axis=0)
        right = jnp.concatenate([zs, s[DK:]], axis=0)
        return jnp.concatenate([left, right], axis=1)

    for c in range(tb // CHUNK):
        rows = slice(c * CHUNK, (c + 1) * CHUNK)
        qc = q[rows]
        qb16 = qc.astype(BF16)
        qf = (qc * qdf_ref[...]).astype(BF16)
        qb = (qc * qdb_ref[...]).astype(BF16)
        kc = kv_ref[0, rows, 0:QK_W].astype(F32)
        vc = kv_ref[0, rows, QK_W:KV_W]
        kd = (kc * kdf_ref[...]).astype(BF16)
        ys = []
        for p in range(PAIRS):
            ql = slice(p * 128, (p + 1) * 128)
            kp = kc[:, ql]
            k_lo = jnp.where(lane < DK, kp, 0.0).astype(BF16)
            k_hi = jnp.where(lane >= DK, kp, 0.0).astype(BF16)
            krhs = jnp.concatenate([k_lo, k_hi], axis=0)
            sc = (_dot_nt(qb16[:, ql], krhs) * dm_ref[p]).astype(BF16)
            vp = vc[:, p * 256:(p + 1) * 256]
            vrhs = jnp.concatenate([jnp.concatenate([vp[:, :DV], zv], axis=1),
                                    jnp.concatenate([zv, vp[:, DV:]], axis=1)], axis=0)
            y = _dot(sc, vrhs)
            s_f = s_ref[ql, :]
            srhs = jnp.concatenate([block_diag_state(s_f.astype(BF16)),
                                    block_diag_state(sb_ref[0, c, ql, :])], axis=0)
            qlhs = jnp.concatenate([qf[:, ql], qb[:, ql]], axis=1)
            y = y + _dot(qlhs, srhs)
            ys.append(y)
            s_ref[ql, :] = cdf_ref[ql, :] * s_f + _pair_diag(_dot_tn(kd[:, ql], vp))
        gc = g[rows]
        sg = gc * jax.nn.sigmoid(gc)
        for p in range(PAIRS):
            for hh in range(2):
                h = 2 * p + hh
                yh = ys[p][:, hh * DV:(hh + 1) * DV]
                mu = jnp.mean(yh, axis=-1, keepdims=True)
                yc = yh - mu
                var = jnp.mean(yc * yc, axis=-1, keepdims=True)
                yn = yc * lax.rsqrt(var + LN_EPS)
                r_ref[rows, h * DV:(h + 1) * DV] = (yn * sg[:, h * DV:(h + 1) * DV]).astype(BF16)

    ret = _dot(r_ref[...], wbr_ref[...])

    ph = _dot(jnp.concatenate([up_ref[0], un_ref[0]], axis=0), wr_ref[:, COL_P:COL_GA])
    p_prev = ph[U_HALO - POOL_HALO:U_HALO] * (j > 0).astype(F32)
    p_next = ph[U_HALO:U_HALO + POOL_HALO] * (j < nb - 1).astype(F32)
    pm = _dot(u, wr_ref[:, COL_P:COL_GA])
    pe = jnp.concatenate([p_prev, pm, p_next], axis=0)
    feat = _pool_features(pe, j, nb, tb, seq_len, poolw_ref, pscale_ref).astype(BF16)
    pool = _dot(feat, wbp_ref[...])

    ga = _dot(u, wr_ref[:, COL_GA:COL_GB])
    merged = jax.nn.sigmoid(ga) * ret
    gb = _dot(u, wr_ref[:, COL_GB:IN_W])
    merged = (merged + jax.nn.sigmoid(gb) * pool).astype(BF16)
    for r0 in range(0, tb, MIX_RB):
        rs = slice(r0, r0 + MIX_RB)
        z = ALPHA * x[rs] + g1_ref[0, 0] * _dot(merged[rs], wout_ref[...])
        o_ref[0, rs] = _ln(z) * lng_ref[...] + lnb_ref[...]


def _mixer(x, u, kv, sb, sf0, mod, w_rest, dm, qdf, qdb, kdf, cdf,
           pool_w, pool_scale, w_br, w_bp, w_out, ln_g, ln_b):
    B, L, _ = x.shape
    tb = TB_MIX
    nb = L // tb
    nch = tb // CHUNK
    hb = tb // U_HALO
    nh = L // U_HALO
    return pl.pallas_call(
        functools.partial(_mixer_kernel, nb=nb, seq_len=L),
        grid=(B, nb),
        in_specs=[pl.BlockSpec((1, tb, D_MODEL), lambda b, j: (b, j, 0)),
                  pl.BlockSpec((1, tb, D_MODEL), lambda b, j: (b, j, 0)),
                  pl.BlockSpec((1, U_HALO, D_MODEL), lambda b, j: (b, jnp.maximum(j * hb - 1, 0), 0)),
                  pl.BlockSpec((1, U_HALO, D_MODEL),
                               lambda b, j: (b, jnp.minimum((j + 1) * hb, nh - 1), 0)),
                  pl.BlockSpec((1, tb, KV_W), lambda b, j: (b, j, 0)),
                  pl.BlockSpec((1, nch, QK_W, DV), lambda b, j: (b, j, 0, 0)),
                  pl.BlockSpec((1, QK_W, DV), lambda b, j: (b, 0, 0)),
                  _mod_spec(2),
                  _const_spec((D_MODEL, IN_W)),
                  _const_spec((PAIRS, CHUNK, 256)),
                  _const_spec((CHUNK, QK_W)), _const_spec((CHUNK, QK_W)), _const_spec((CHUNK, QK_W)),
                  _const_spec((QK_W, DV)),
                  _const_spec((len(POOL_WINDOWS), POOL_GD, POOL_GD)),
                  _const_spec((1, POOL_W)),
                  _const_spec((V_W, D_MODEL)),
                  _const_spec((POOL_W, D_MODEL)),
                  _const_spec((D_MODEL, D_MODEL)),
                  _const_spec((1, D_MODEL)), _const_spec((1, D_MODEL))],
        out_specs=pl.BlockSpec((1, tb, D_MODEL), lambda b, j: (b, j, 0)),
        out_shape=jax.ShapeDtypeStruct((B, L, D_MODEL), F32),
        scratch_shapes=[pltpu.VMEM((QK_W, DV), F32), pltpu.VMEM((tb, V_W), BF16)],
        compiler_params=pltpu.CompilerParams(
            dimension_semantics=("arbitrary", "arbitrary"), vmem_limit_bytes=VMEM_LIMIT),
        name="mixer",
    )(x, u, u, u, kv, sb, sf0, mod, w_rest, dm, qdf, qdb, kdf, cdf,
      pool_w, pool_scale, w_br, w_bp, w_out, ln_g, ln_b)


def _ffn_kernel(x_ref, xn_ref, sh_ref, sc_ref, g2_ref, wup_ref, cw_ref, wdn_ref,
                lng_ref, lnb_ref, o_ref, u_ref, acc_ref, ha_ref, hb_ref, top_ref, *, nb):
    j = pl.program_id(1)
    tb = x_ref.shape[1]
    m = tb + GRID_W
    n = tb + 2 * GRID_W
    scale = 1.0 + sc_ref[0, 0]
    shift = sh_ref[0, 0]
    x = x_ref[0]

    def mod(v):
        return _ln(v) * scale + shift

    @pl.when(j == 0)
    def _():
        top_ref[...] = jnp.zeros_like(top_ref)

    u_ref[0:tb] = mod(x).astype(BF16)
    u_ref[tb:m] = (mod(xn_ref[0]) * (j < nb - 1).astype(F32)).astype(BF16)

    col = lax.broadcasted_iota(jnp.int32, (tb, FF_CW), 0) & (GRID_W - 1)
    has_left = col > 0
    has_right = col < GRID_W - 1

    pad = jnp.zeros((FF_PAD, FF_CW), F32)
    for h_ref in (ha_ref, hb_ref):
        for ab in range(2):
            h_ref[ab, 0:FF_PAD] = pad
            h_ref[ab, FF_PAD + n:FF_PAD + n + FF_PAD] = pad

    def lanes(c, half):
        return pl.ds(pl.multiple_of(half * D_FF + c * FF_CW, FF_CW), FF_CW)

    def up(c, h_ref):
        u = u_ref[...]
        for half in range(2):
            h = _dot(u, wup_ref[:, lanes(c, half)])
            h_ref[half, FF_PAD:FF_PAD + GRID_W] = top_ref[half, c]
            h_ref[half, FF_PAD + GRID_W:FF_PAD + n] = h
            top_ref[half, c] = h[tb - GRID_W:tb]

    def conv(h_ref, ab, cw):
        cols = []
        for dc in range(3):
            hs = h_ref[ab, FF_PAD + dc - 1:FF_PAD + dc - 1 + n].astype(BF16)
            g = None
            for dr in range(3):
                term = cw[3 * dr + dc:3 * dr + dc + 1] * hs[dr * GRID_W:dr * GRID_W + tb]
                g = term if g is None else g + term
            cols.append(g)
        zero = jnp.zeros_like(cols[1])
        return (cols[1] + cw[9:10]) + (jnp.where(has_left, cols[0], zero) + jnp.where(has_right, cols[2], zero))

    def down(c, h_ref):
        a = conv(h_ref, 0, cw_ref[:, lanes(c, 0)])
        b = conv(h_ref, 1, cw_ref[:, lanes(c, 1)])
        acc_ref[...] += _dot(_gelu_tanh(a.astype(F32)).astype(BF16) * b, wdn_ref[c])

    acc_ref[...] = jnp.zeros_like(acc_ref)
    up(0, ha_ref)

    def body(i, carry):
        c = 2 * i
        up(c + 1, hb_ref)
        down(c, ha_ref)
        up(c + 2, ha_ref)
        down(c + 1, hb_ref)
        return carry

    lax.fori_loop(0, FF_NC // 2, body, 0)
    down(FF_NC - 1, ha_ref)
    z = ALPHA * x + g2_ref[0, 0] * acc_ref[...]
    o_ref[0] = _ln(z) * lng_ref[...] + lnb_ref[...]


def _ffn(x, mod, w_up, conv_wb, w_down, ln_g, ln_b):
    B, L, _ = x.shape
    tb = TB_FFN
    nb = L // tb
    hb = tb // GRID_W
    nh = L // GRID_W
    return pl.pallas_call(
        functools.partial(_ffn_kernel, nb=nb),
        grid=(B, nb),
        in_specs=[pl.BlockSpec((1, tb, D_MODEL), lambda b, j: (b, j, 0)),
                  pl.BlockSpec((1, GRID_W, D_MODEL),
                               lambda b, j: (b, jnp.minimum((j + 1) * hb, nh - 1), 0)),
                  _mod_spec(3), _mod_spec(4), _mod_spec(5),
                  _const_spec((D_MODEL, 2 * D_FF)),
                  _const_spec((16, 2 * D_FF)),
                  _const_spec((FF_NC, FF_CW, D_MODEL)),
                  _const_spec((1, D_MODEL)), _const_spec((1, D_MODEL))],
        out_specs=pl.BlockSpec((1, tb, D_MODEL), lambda b, j: (b, j, 0)),
        out_shape=jax.ShapeDtypeStruct((B, L, D_MODEL), F32),
        scratch_shapes=[pltpu.VMEM((tb + GRID_W, D_MODEL), BF16),
                        pltpu.VMEM((tb, D_MODEL), F32),
                        pltpu.VMEM((2, tb + 2 * GRID_W + 2 * FF_PAD, FF_CW), F32),
                        pltpu.VMEM((2, tb + 2 * GRID_W + 2 * FF_PAD, FF_CW), F32),
                        pltpu.VMEM((2, FF_NC, GRID_W, FF_CW), F32)],
        compiler_params=pltpu.CompilerParams(
            dimension_semantics=("arbitrary", "arbitrary"), vmem_limit_bytes=VMEM_LIMIT),
        name="conv_ffn",
    )(x, x, mod, mod, mod, w_up, conv_wb, w_down, ln_g, ln_b)


def _decay_tables(ret_decay_logit):
    lg = jax.nn.log_sigmoid(ret_decay_logit.astype(F32))
    pos = jnp.arange(CHUNK, dtype=F32)
    diff = pos[:, None] - pos[None, :]
    d_f = jnp.where(diff[None] >= 0, jnp.exp(jnp.maximum(diff, 0.0)[None] * lg[0][:, None, None]), 0.0)
    d_b = jnp.where(diff[None] <= 0, jnp.exp(jnp.maximum(-diff, 0.0)[None] * lg[1][:, None, None]), 0.0)
    dm = (d_f + d_b).reshape(PAIRS, 2, CHUNK, CHUNK).transpose(0, 2, 1, 3).reshape(PAIRS, CHUNK, 2 * CHUNK)

    def lanes(t):
        return jnp.repeat(t, DK, axis=1)

    qdf = lanes(jnp.exp((pos + 1.0)[:, None] * lg[0][None, :]))
    qdb = lanes(jnp.exp((CHUNK - pos)[:, None] * lg[1][None, :]))
    kdf = lanes(jnp.exp((CHUNK - 1.0 - pos)[:, None] * lg[0][None, :]))
    kdb = lanes(jnp.exp(pos[:, None] * lg[1][None, :]))

    def rows(t):
        return jnp.broadcast_to(jnp.repeat(t, DK)[:, None], (QK_W, DV))

    cdf = rows(jnp.exp(CHUNK * lg[0]))
    cdb = rows(jnp.exp(CHUNK * lg[1]))
    return dm, qdf, qdb, kdf, kdb, cdf, cdb


def kernel(x, c, ctx, c_ctx, w_ada, b_ada, w_in, ret_decay_logit, pool_w, pool_scale, w_branch_ret,
           w_branch_pool, w_out, ln1_g, ln1_b, w_up, conv_w, conv_b, w_down, ln2_g, ln2_b):
    B = x.shape[0]
    D = D_MODEL
    assert w_ada.shape[0] == 1, "single-layer stack"

    cc = jnp.concatenate([c, c_ctx[None, :], jnp.zeros((MOD_ROWS - B - 1, D), F32)], axis=0)
    mod = _adaln(cc, w_ada[0], b_ada[0][None, :]).reshape(MOD_ROWS, N_MOD, 1, D)

    dm, qdf, qdb, kdf, kdb, cdf, cdb = _decay_tables(ret_decay_logit[0])

    s_f, s_b, w_kv_b = _ctx_states(ctx, mod, w_in[0], kdf, kdb, cdf, cdb)
    u, kv, sb, w_in_b, w_up_b, w_down_b, w_br_b, w_bp_b, w_out_b = _kv_states(
        x, mod, w_kv_b, kdb, cdb, s_b,
        [w_in[0], w_up[0], w_down[0], w_branch_ret[0], w_branch_pool[0], w_out[0]])
    x1 = _mixer(x, u, kv, sb, s_f, mod, w_in_b, dm, qdf, qdb, kdf, cdf,
                pool_w[0].astype(BF16), pool_scale[0][None, :], w_br_b, w_bp_b, w_out_b,
                ln1_g[0][None, :], ln1_b[0][None, :])

    conv_wb = jnp.concatenate([conv_w[0].reshape(9, 2 * D_FF), conv_b[0][None, :],
                               jnp.zeros((6, 2 * D_FF), F32)], axis=0).astype(BF16)
    return _ffn(x1, mod, w_up_b, conv_wb, w_down_b.reshape(FF_NC, FF_CW, D),
                ln2_g[0][None, :], ln2_b[0][None, :])
```

```python
import functools

import jax
import jax.numpy as jnp
import numpy as np
from jax import lax
from jax.experimental import pallas as pl
from jax.experimental.pallas import tpu as pltpu

F32 = jnp.float32
BF16 = jnp.bfloat16

D_MODEL = 1024
GRID_W = 64
HEADS = 8
DK = 64
DV = 128
QK_W = HEADS * DK
V_W = HEADS * DV
KV_W = QK_W + V_W
CHUNK = 128
PAIRS = HEADS // 2
POOL_WINDOWS = (2, 4, 8, 16)
POOL_GD = 128
POOL_W = 512
COL_Q = KV_W
COL_G = COL_Q + QK_W
COL_P = COL_G + V_W
COL_GA = COL_P + POOL_W
COL_GB = COL_GA + D_MODEL
IN_W = COL_GB + D_MODEL
D_FF = 2816
FF_CW = 256
FF_NC = D_FF // FF_CW
FF_PAD = 8
N_MOD = 6
LN_EPS = 1e-6
ALPHA = 2.0 ** 0.25
POOL_HALO = 8
U_HALO = 16
MOD_ROWS = 8

VMEM_LIMIT = 60 * 1024 * 1024

MIX_RB = 256

TB_KV = 1024
TB_MIX = 1024
TB_FFN = 1024


def _dot(a, b):
    return jnp.dot(a, b, preferred_element_type=F32)


def _dot_nt(a, b):
    return lax.dot_general(a, b, (((1,), (1,)), ((), ())), preferred_element_type=F32)


def _dot_tn(a, b):
    return lax.dot_general(a, b, (((0,), (0,)), ((), ())), preferred_element_type=F32)


def _ln(x):
    mu = jnp.mean(x, axis=-1, keepdims=True)
    xc = x - mu
    var = jnp.mean(xc * xc, axis=-1, keepdims=True)
    return xc * lax.rsqrt(var + LN_EPS)


def _gelu_tanh(x):
    c = float(np.sqrt(2.0 / np.pi))
    half = 0.5 * x
    return half + half * jnp.tanh(x * (c + (c * 0.044715) * (x * x)))


def _mod_spec(k, row=None):
    if row is None:
        return pl.BlockSpec((1, 1, 1, D_MODEL), lambda b, *_: (b, k, 0, 0))
    return pl.BlockSpec((1, 1, 1, D_MODEL), lambda *_: (row, k, 0, 0))


def _const_spec(shape):
    nd = len(shape)
    return pl.BlockSpec(shape, lambda *_: (0,) * nd, pipeline_mode=pl.Buffered(1))


def _pair_diag(r):
    row = lax.broadcasted_iota(jnp.int32, (CHUNK, DV), 0)
    return jnp.where(row < DK, r[:, :DV], r[:, DV:])


def _chunk_kv(kd, v):
    outs = []
    for p in range(PAIRS):
        r = _dot_tn(kd[:, p * 128:(p + 1) * 128], v[:, p * 256:(p + 1) * 256])
        outs.append(_pair_diag(r))
    return jnp.concatenate(outs, axis=0)


def _adaln_kernel(c_ref, w_ref, b_ref, o_ref):
    c = c_ref[...]
    s = c * jax.nn.sigmoid(c)
    o_ref[...] = _dot(s, w_ref[...]) + b_ref[...]


def _adaln(cc, w_ada, b_ada):
    n = w_ada.shape[1]
    bn = 1536
    return pl.pallas_call(
        _adaln_kernel,
        grid=(n // bn,),
        in_specs=[pl.BlockSpec((MOD_ROWS, D_MODEL), lambda i: (0, 0)),
                  pl.BlockSpec((D_MODEL, bn), lambda i: (0, i)),
                  pl.BlockSpec((1, bn), lambda i: (0, i))],
        out_specs=pl.BlockSpec((MOD_ROWS, bn), lambda i: (0, i)),
        out_shape=jax.ShapeDtypeStruct((MOD_ROWS, n), F32),
        compiler_params=pltpu.CompilerParams(vmem_limit_bytes=VMEM_LIMIT),
        name="adaln",
    )(cc, w_ada, b_ada)


def _ctx_kernel(ctx_ref, sh_ref, sc_ref, wkv_ref, kdf_ref, kdb_ref, cdf_ref, cdb_ref,
                sf_ref, sb_ref, wkvb_ref):
    x = ctx_ref[0]
    u = (_ln(x) * (1.0 + sc_ref[0, 0]) + sh_ref[0, 0]).astype(BF16)
    wkv = wkv_ref[...].astype(BF16)
    wkvb_ref[...] = wkv
    kv = _dot(u, wkv)
    k = kv[:, :QK_W] * (DK ** -0.5)
    v = kv[:, QK_W:].astype(BF16)
    n = x.shape[0] // CHUNK
    sf = jnp.zeros((QK_W, DV), F32)
    for c in range(n):
        kc = k[c * CHUNK:(c + 1) * CHUNK]
        vc = v[c * CHUNK:(c + 1) * CHUNK]
        sf = cdf_ref[...] * sf + _chunk_kv((kc * kdf_ref[...]).astype(BF16), vc)
    sb = jnp.zeros((QK_W, DV), F32)
    for c in reversed(range(n)):
        kc = k[c * CHUNK:(c + 1) * CHUNK]
        vc = v[c * CHUNK:(c + 1) * CHUNK]
        sb = cdb_ref[...] * sb + _chunk_kv((kc * kdb_ref[...]).astype(BF16), vc)
    sf_ref[0] = sf
    sb_ref[0] = sb


def _ctx_states(ctx, mod, w_kv, kdf, kdb, cdf, cdb):
    B, Lc, _ = ctx.shape
    st = jax.ShapeDtypeStruct((B, QK_W, DV), F32)
    return pl.pallas_call(
        _ctx_kernel,
        grid=(B,),
        in_specs=[pl.BlockSpec((1, Lc, D_MODEL), lambda b: (b, 0, 0)),
                  _mod_spec(0, row=B), _mod_spec(1, row=B),
                  _const_spec((D_MODEL, KV_W)),
                  _const_spec((CHUNK, QK_W)), _const_spec((CHUNK, QK_W)),
                  _const_spec((QK_W, DV)), _const_spec((QK_W, DV))],
        out_specs=[pl.BlockSpec((1, QK_W, DV), lambda b: (b, 0, 0)),
                   pl.BlockSpec((1, QK_W, DV), lambda b: (b, 0, 0)),
                   pl.BlockSpec((D_MODEL, KV_W), lambda b: (0, 0))],
        out_shape=[st, st, jax.ShapeDtypeStruct((D_MODEL, KV_W), BF16)],
        compiler_params=pltpu.CompilerParams(
            dimension_semantics=("arbitrary",), vmem_limit_bytes=VMEM_LIMIT),
        name="ctx_states",
    )(ctx, mod, mod, w_kv, kdf, kdb, cdf, cdb)


def _kv_kernel(x_ref, sh_ref, sc_ref, wkv_ref, kdb_ref, cdb_ref, sb0_ref, *rest, n_cast):
    cast_in, (u_ref, kv_ref, sb_ref), cast_out, (s_ref,) = (
        rest[:n_cast], rest[n_cast:n_cast + 3], rest[n_cast + 3:2 * n_cast + 3], rest[2 * n_cast + 3:])

    @pl.when(pl.program_id(1) == 0)
    def _():
        s_ref[...] = sb0_ref[0]

    for src, dst in zip(cast_in, cast_out):
        dst[...] = src[...].astype(BF16)

    x = x_ref[0]
    u = (_ln(x) * (1.0 + sc_ref[0, 0]) + sh_ref[0, 0]).astype(BF16)
    u_ref[0] = u
    kv = _dot(u, wkv_ref[...])
    k = kv[:, :QK_W] * (DK ** -0.5)
    v = kv[:, QK_W:].astype(BF16)
    kv_ref[0, :, :QK_W] = k.astype(BF16)
    kv_ref[0, :, QK_W:] = v
    n = x.shape[0] // CHUNK
    for c in reversed(range(n)):
        s = s_ref[...]
        sb_ref[0, c] = s.astype(BF16)
        kc = k[c * CHUNK:(c + 1) * CHUNK]
        vc = v[c * CHUNK:(c + 1) * CHUNK]
        s_ref[...] = cdb_ref[...] * s + _chunk_kv((kc * kdb_ref[...]).astype(BF16), vc)


def _kv_states(x, mod, w_kv, kdb, cdb, sb0, later_weights):
    B, L, _ = x.shape
    tb = TB_KV
    nb = L // tb
    nch = tb // CHUNK
    steps = B * nb
    slab_specs = [pl.BlockSpec((w.shape[0] // steps, w.shape[1]), lambda b, j: (b * nb + j, 0))
                  for w in later_weights]
    for w in later_weights:
        assert w.shape[0] % (16 * steps) == 0, "row slabs must be whole bf16 tiles"
    return pl.pallas_call(
        functools.partial(_kv_kernel, n_cast=len(later_weights)),
        grid=(B, nb),
        in_specs=[pl.BlockSpec((1, tb, D_MODEL), lambda b, j: (b, nb - 1 - j, 0)),
                  _mod_spec(0), _mod_spec(1),
                  _const_spec((D_MODEL, KV_W)),
                  _const_spec((CHUNK, QK_W)),
                  _const_spec((QK_W, DV)),
                  pl.BlockSpec((1, QK_W, DV), lambda b, j: (b, 0, 0))] + slab_specs,
        out_specs=[pl.BlockSpec((1, tb, D_MODEL), lambda b, j: (b, nb - 1 - j, 0)),
                   pl.BlockSpec((1, tb, KV_W), lambda b, j: (b, nb - 1 - j, 0)),
                   pl.BlockSpec((1, nch, QK_W, DV), lambda b, j: (b, nb - 1 - j, 0, 0))] + slab_specs,
        out_shape=[jax.ShapeDtypeStruct((B, L, D_MODEL), BF16),
                   jax.ShapeDtypeStruct((B, L, KV_W), BF16),
                   jax.ShapeDtypeStruct((B, L // CHUNK, QK_W, DV), BF16)]
        + [jax.ShapeDtypeStruct(w.shape, BF16) for w in later_weights],
        scratch_shapes=[pltpu.VMEM((QK_W, DV), F32)],
        compiler_params=pltpu.CompilerParams(
            dimension_semantics=("arbitrary", "arbitrary"), vmem_limit_bytes=VMEM_LIMIT),
        name="kv_states",
    )(x, mod, mod, w_kv, kdb, cdb, sb0, *later_weights)


def _pool_features(pe, j, nb, tb, seq_len, poolw_ref, pscale_ref):
    n = pe.shape[0]
    r = lax.broadcasted_iota(jnp.int32, (POOL_HALO, POOL_GD), 0)
    outs = []
    for gi, w in enumerate(POOL_WINDOWS):
        half = w // 2
        a = pe[:, gi * POOL_GD:(gi + 1) * POOL_GD]
        centre = a[POOL_HALO:POOL_HALO + tb]
        s = a
        step = 1
        while step < half:
            s = s + pltpu.roll(s, n - step, axis=0)
            step *= 2
        s = (s + pltpu.roll(s, half, axis=0))[POOL_HALO:POOL_HALO + tb]
        cnt_head = (jnp.minimum(r + half, seq_len) - jnp.maximum(r - half, 0)).astype(F32)
        t_tail = seq_len - POOL_HALO + r
        cnt_tail = (jnp.minimum(t_tail + half, seq_len) - jnp.maximum(t_tail - half, 0)).astype(F32)
        head = jnp.where(j == 0, 1.0 / cnt_head, 1.0 / w)
        tail = jnp.where(j == nb - 1, 1.0 / cnt_tail, 1.0 / w)
        mean = jnp.concatenate([s[:POOL_HALO] * head, s[POOL_HALO:tb - POOL_HALO] * (1.0 / w),
                                s[tb - POOL_HALO:] * tail], axis=0)
        diff = (mean - centre).astype(BF16)
        outs.append(_dot(diff, poolw_ref[gi]))
    return jnp.concatenate(outs, axis=-1) * pscale_ref[...]


def _mixer_kernel(x_ref, u_ref, up_ref, un_ref, kv_ref, sb_ref, sf0_ref, g1_ref,
                  wr_ref, dm_ref, qdf_ref, qdb_ref, kdf_ref, cdf_ref,
                  poolw_ref, pscale_ref, wbr_ref, wbp_ref, wout_ref, lng_ref, lnb_ref,
                  o_ref, s_ref, r_ref, *, nb, seq_len):
    j = pl.program_id(1)
    tb = x_ref.shape[1]

    @pl.when(j == 0)
    def _():
        s_ref[...] = sf0_ref[0]

    lane = lax.broadcasted_iota(jnp.int32, (CHUNK, 128), 1)
    zv = jnp.zeros((CHUNK, DV), BF16)
    zs = jnp.zeros((DK, DV), BF16)

    def block_diag_state(s):
        left = jnp.concatenate([s[:DK], zs], axis=0)
        right = jnp.concatenate([zs, s[DK:]], axis=0)
        return jnp.concatenate([left, right], axis=1)

    per_rb = MIX_RB // CHUNK
    for c in range(tb // CHUNK):
        rows = slice(c * CHUNK, (c + 1) * CHUNK)
        if c % per_rb == 0:
            u_rb = u_ref[0, c * CHUNK:c * CHUNK + MIX_RB]
            q = _dot(u_rb, wr_ref[:, COL_Q:COL_G])
            g = _dot(u_rb, wr_ref[:, COL_G:COL_P])
        rows_rb = slice((c % per_rb) * CHUNK, (c % per_rb + 1) * CHUNK)
        qc = q[rows_rb]
        qb16 = qc.astype(BF16)
        qf = (qc * qdf_ref[...]).astype(BF16)
        qb = (qc * qdb_ref[...]).astype(BF16)
        kc = kv_ref[0, rows, 0:QK_W].astype(F32)
        vc = kv_ref[0, rows, QK_W:KV_W]
        kd = (kc * kdf_ref[...]).astype(BF16)
        ys = []
        for p in range(PAIRS):
            ql = slice(p * 128, (p + 1) * 128)
            kp = kc[:, ql]
            k_lo = jnp.where(lane < DK, kp, 0.0).astype(BF16)
            k_hi = jnp.where(lane >= DK, kp, 0.0).astype(BF16)
            krhs = jnp.concatenate([k_lo, k_hi], axis=0)
            sc = (_dot_nt(qb16[:, ql], krhs) * dm_ref[p]).astype(BF16)
            vp = vc[:, p * 256:(p + 1) * 256]
            vrhs = jnp.concatenate([jnp.concatenate([vp[:, :DV], zv], axis=1),
                                    jnp.concatenate([zv, vp[:, DV:]], axis=1)], axis=0)
            y = _dot(sc, vrhs)
            s_f = s_ref[ql, :]
            srhs = jnp.concatenate([block_diag_state(s_f.astype(BF16)),
                                    block_diag_state(sb_ref[0, c, ql, :])], axis=0)
            qlhs = jnp.concatenate([qf[:, ql], qb[:, ql]], axis=1)
            y = y + _dot(qlhs, srhs)
            ys.append(y)
            s_ref[ql, :] = cdf_ref[ql, :] * s_f + _pair_diag(_dot_tn(kd[:, ql], vp))
        gc = g[rows_rb]
        sg = gc * jax.nn.sigmoid(gc)
        for p in range(PAIRS):
            for hh in range(2):
                h = 2 * p + hh
                yh = ys[p][:, hh * DV:(hh + 1) * DV]
                mu = jnp.mean(yh, axis=-1, keepdims=True)
                yc = yh - mu
                var = jnp.mean(yc * yc, axis=-1, keepdims=True)
                yn = yc * lax.rsqrt(var + LN_EPS)
                r_ref[rows, h * DV:(h + 1) * DV] = (yn * sg[:, h * DV:(h + 1) * DV]).astype(BF16)

    ret = _dot(r_ref[...], wbr_ref[...])

    ph = _dot(jnp.concatenate([up_ref[0], un_ref[0]], axis=0), wr_ref[:, COL_P:COL_GA])
    p_prev = ph[U_HALO - POOL_HALO:U_HALO] * (j > 0).astype(F32)
    p_next = ph[U_HALO:U_HALO + POOL_HALO] * (j < nb - 1).astype(F32)
    pm = _dot(u_ref[0], wr_ref[:, COL_P:COL_GA])
    pe = jnp.concatenate([p_prev, pm, p_next], axis=0)
    feat = _pool_features(pe, j, nb, tb, seq_len, poolw_ref, pscale_ref).astype(BF16)
    pool = _dot(feat, wbp_ref[...])

    ga = _dot(u_ref[0], wr_ref[:, COL_GA:COL_GB])
    merged = jax.nn.sigmoid(ga) * ret
    gb = _dot(u_ref[0], wr_ref[:, COL_GB:IN_W])
    merged = (merged + jax.nn.sigmoid(gb) * pool).astype(BF16)
    for r0 in range(0, tb, MIX_RB):
        rs = slice(r0, r0 + MIX_RB)
        z = ALPHA * x_ref[0, rs] + g1_ref[0, 0] * _dot(merged[rs], wout_ref[...])
        o_ref[0, rs] = _ln(z) * lng_ref[...] + lnb_ref[...]


def _mixer(x, u, kv, sb, sf0, mod, w_rest, dm, qdf, qdb, kdf, cdf,
           pool_w, pool_scale, w_br, w_bp, w_out, ln_g, ln_b):
    B, L, _ = x.shape
    tb = TB_MIX
    nb = L // tb
    nch = tb // CHUNK
    hb = tb // U_HALO
    nh = L // U_HALO
    return pl.pallas_call(
        functools.partial(_mixer_kernel, nb=nb, seq_len=L),
        grid=(B, nb),
        in_specs=[pl.BlockSpec((1, tb, D_MODEL), lambda b, j: (b, j, 0)),
                  pl.BlockSpec((1, tb, D_MODEL), lambda b, j: (b, j, 0)),
                  pl.BlockSpec((1, U_HALO, D_MODEL), lambda b, j: (b, jnp.maximum(j * hb - 1, 0), 0)),
                  pl.BlockSpec((1, U_HALO, D_MODEL),
                               lambda b, j: (b, jnp.minimum((j + 1) * hb, nh - 1), 0)),
                  pl.BlockSpec((1, tb, KV_W), lambda b, j: (b, j, 0)),
                  pl.BlockSpec((1, nch, QK_W, DV), lambda b, j: (b, j, 0, 0)),
                  pl.BlockSpec((1, QK_W, DV), lambda b, j: (b, 0, 0)),
                  _mod_spec(2),
                  _const_spec((D_MODEL, IN_W)),
                  _const_spec((PAIRS, CHUNK, 256)),
                  _const_spec((CHUNK, QK_W)), _const_spec((CHUNK, QK_W)), _const_spec((CHUNK, QK_W)),
                  _const_spec((QK_W, DV)),
                  _const_spec((len(POOL_WINDOWS), POOL_GD, POOL_GD)),
                  _const_spec((1, POOL_W)),
                  _const_spec((V_W, D_MODEL)),
                  _const_spec((POOL_W, D_MODEL)),
                  _const_spec((D_MODEL, D_MODEL)),
                  _const_spec((1, D_MODEL)), _const_spec((1, D_MODEL))],
        out_specs=pl.BlockSpec((1, tb, D_MODEL), lambda b, j: (b, j, 0)),
        out_shape=jax.ShapeDtypeStruct((B, L, D_MODEL), F32),
        scratch_shapes=[pltpu.VMEM((QK_W, DV), F32), pltpu.VMEM((tb, V_W), BF16)],
        compiler_params=pltpu.CompilerParams(
            dimension_semantics=("arbitrary", "arbitrary"), vmem_limit_bytes=VMEM_LIMIT),
        name="mixer",
    )(x, u, u, u, kv, sb, sf0, mod, w_rest, dm, qdf, qdb, kdf, cdf,
      pool_w, pool_scale, w_br, w_bp, w_out, ln_g, ln_b)


def _ffn_kernel(x_ref, xn_ref, sh_ref, sc_ref, g2_ref, wup_ref, cw_ref, wdn_ref,
                lng_ref, lnb_ref, o_ref, u_ref, acc_ref, ha_ref, hb_ref, top_ref, *, nb):
    j = pl.program_id(1)
    tb = x_ref.shape[1]
    m = tb + GRID_W
    n = tb + 2 * GRID_W
    scale = 1.0 + sc_ref[0, 0]
    shift = sh_ref[0, 0]

    def mod(v):
        return _ln(v) * scale + shift

    @pl.when(j == 0)
    def _():
        top_ref[...] = jnp.zeros_like(top_ref)

    u_ref[0:tb] = mod(x_ref[0]).astype(BF16)
    u_ref[tb:m] = (mod(xn_ref[0]) * (j < nb - 1).astype(F32)).astype(BF16)

    col = lax.broadcasted_iota(jnp.int32, (tb, FF_CW), 0) & (GRID_W - 1)
    has_left = col > 0
    has_right = col < GRID_W - 1

    pad = jnp.zeros((FF_PAD, FF_CW), F32)
    for h_ref in (ha_ref, hb_ref):
        for ab in range(2):
            h_ref[ab, 0:FF_PAD] = pad
            h_ref[ab, FF_PAD + n:FF_PAD + n + FF_PAD] = pad

    def lanes(c, half):
        return pl.ds(pl.multiple_of(half * D_FF + c * FF_CW, FF_CW), FF_CW)

    def up(c, h_ref):
        u = u_ref[...]
        for half in range(2):
            h = _dot(u, wup_ref[:, lanes(c, half)])
            h_ref[half, FF_PAD:FF_PAD + GRID_W] = top_ref[half, c]
            h_ref[half, FF_PAD + GRID_W:FF_PAD + n] = h
            top_ref[half, c] = h[tb - GRID_W:tb]

    def conv(h_ref, ab, cw):
        cols = []
        for dc in range(3):
            hs = h_ref[ab, FF_PAD + dc - 1:FF_PAD + dc - 1 + n].astype(BF16)
            g = None
            for dr in range(3):
                term = cw[3 * dr + dc:3 * dr + dc + 1] * hs[dr * GRID_W:dr * GRID_W + tb]
                g = term if g is None else g + term
            cols.append(g)
        zero = jnp.zeros_like(cols[1])
        return (cols[1] + cw[9:10]) + (jnp.where(has_left, cols[0], zero) + jnp.where(has_right, cols[2], zero))

    def down(c, h_ref):
        a = conv(h_ref, 0, cw_ref[:, lanes(c, 0)])
        b = conv(h_ref, 1, cw_ref[:, lanes(c, 1)])
        acc_ref[...] += _dot(_gelu_tanh(a.astype(F32)).astype(BF16) * b, wdn_ref[c])

    acc_ref[...] = jnp.zeros_like(acc_ref)
    up(0, ha_ref)

    def body(i, carry):
        c = 2 * i
        up(c + 1, hb_ref)
        down(c, ha_ref)
        up(c + 2, ha_ref)
        down(c + 1, hb_ref)
        return carry

    lax.fori_loop(0, FF_NC // 2, body, 0)
    down(FF_NC - 1, ha_ref)
    z = ALPHA * x_ref[0] + g2_ref[0, 0] * acc_ref[...]
    o_ref[0] = _ln(z) * lng_ref[...] + lnb_ref[...]


def _ffn(x, mod, w_up, conv_wb, w_down, ln_g, ln_b):
    B, L, _ = x.shape
    tb = TB_FFN
    nb = L // tb
    hb = tb // GRID_W
    nh = L // GRID_W
    return pl.pallas_call(
        functools.partial(_ffn_kernel, nb=nb),
        grid=(B, nb),
        in_specs=[pl.BlockSpec((1, tb, D_MODEL), lambda b, j: (b, j, 0)),
                  pl.BlockSpec((1, GRID_W, D_MODEL),
                               lambda b, j: (b, jnp.minimum((j + 1) * hb, nh - 1), 0)),
                  _mod_spec(3), _mod_spec(4), _mod_spec(5),
                  _const_spec((D_MODEL, 2 * D_FF)),
                  _const_spec((16, 2 * D_FF)),
                  _const_spec((FF_NC, FF_CW, D_MODEL)),
                  _const_spec((1, D_MODEL)), _const_spec((1, D_MODEL))],
        out_specs=pl.BlockSpec((1, tb, D_MODEL), lambda b, j: (b, j, 0)),
        out_shape=jax.ShapeDtypeStruct((B, L, D_MODEL), F32),
        scratch_shapes=[pltpu.VMEM((tb + GRID_W, D_MODEL), BF16),
                        pltpu.VMEM((tb, D_MODEL), F32),
                        pltpu.VMEM((2, tb + 2 * GRID_W + 2 * FF_PAD, FF_CW), F32),
                        pltpu.VMEM((2, tb + 2 * GRID_W + 2 * FF_PAD, FF_CW), F32),
                        pltpu.VMEM((2, FF_NC, GRID_W, FF_CW), F32)],
        compiler_params=pltpu.CompilerParams(
            dimension_semantics=("arbitrary", "arbitrary"), vmem_limit_bytes=VMEM_LIMIT),
        name="conv_ffn",
    )(x, x, mod, mod, mod, w_up, conv_wb, w_down, ln_g, ln_b)


def _decay_tables(ret_decay_logit):
    lg = jax.nn.log_sigmoid(ret_decay_logit.astype(F32))
    pos = jnp.arange(CHUNK, dtype=F32)
    diff = pos[:, None] - pos[None, :]
    d_f = jnp.where(diff[None] >= 0, jnp.exp(jnp.maximum(diff, 0.0)[None] * lg[0][:, None, None]), 0.0)
    d_b = jnp.where(diff[None] <= 0, jnp.exp(jnp.maximum(-diff, 0.0)[None] * lg[1][:, None, None]), 0.0)
    dm = (d_f + d_b).reshape(PAIRS, 2, CHUNK, CHUNK).transpose(0, 2, 1, 3).reshape(PAIRS, CHUNK, 2 * CHUNK)

    def lanes(t):
        return jnp.repeat(t, DK, axis=1)

    qdf = lanes(jnp.exp((pos + 1.0)[:, None] * lg[0][None, :]))
    qdb = lanes(jnp.exp((CHUNK - pos)[:, None] * lg[1][None, :]))
    kdf = lanes(jnp.exp((CHUNK - 1.0 - pos)[:, None] * lg[0][None, :]))
    kdb = lanes(jnp.exp(pos[:, None] * lg[1][None, :]))

    def rows(t):
        return jnp.broadcast_to(jnp.repeat(t, DK)[:, None], (QK_W, DV))

    cdf = rows(jnp.exp(CHUNK * lg[0]))
    cdb = rows(jnp.exp(CHUNK * lg[1]))
    return dm, qdf, qdb, kdf, kdb, cdf, cdb


def kernel(x, c, ctx, c_ctx, w_ada, b_ada, w_in, ret_decay_logit, pool_w, pool_scale, w_branch_ret,
           w_branch_pool, w_out, ln1_g, ln1_b, w_up, conv_w, conv_b, w_down, ln2_g, ln2_b):
    B = x.shape[0]
    D = D_MODEL
    assert w_ada.shape[0] == 1, "single-layer stack"

    cc = jnp.concatenate([c, c_ctx[None, :], jnp.zeros((MOD_ROWS - B - 1, D), F32)], axis=0)
    mod = _adaln(cc, w_ada[0], b_ada[0][None, :]).reshape(MOD_ROWS, N_MOD, 1, D)

    dm, qdf, qdb, kdf, kdb, cdf, cdb = _decay_tables(ret_decay_logit[0])

    s_f, s_b, w_kv_b = _ctx_states(ctx, mod, w_in[0], kdf, kdb, cdf, cdb)
    u, kv, sb, w_in_b, w_up_b, w_down_b, w_br_b, w_bp_b, w_out_b = _kv_states(
        x, mod, w_kv_b, kdb, cdb, s_b,
        [w_in[0], w_up[0], w_down[0], w_branch_ret[0], w_branch_pool[0], w_out[0]])
    x1 = _mixer(x, u, kv, sb, s_f, mod, w_in_b, dm, qdf, qdb, kdf, cdf,
                pool_w[0].astype(BF16), pool_scale[0][None, :], w_br_b, w_bp_b, w_out_b,
                ln1_g[0][None, :], ln1_b[0][None, :])

    conv_wb = jnp.concatenate([conv_w[0].reshape(9, 2 * D_FF), conv_b[0][None, :],
                               jnp.zeros((6, 2 * D_FF), F32)], axis=0).astype(BF16)
    return _ffn(x1, mod, w_up_b, conv_wb, w_down_b.reshape(FF_NC, FF_CW, D),
                ln2_g[0][None, :], ln2_b[0][None, :])
```

```python
import functools

import jax
import jax.numpy as jnp
import numpy as np
from jax import lax
from jax.experimental import pallas as pl
from jax.experimental.pallas import tpu as pltpu

F32 = jnp.float32
BF16 = jnp.bfloat16

D_MODEL = 1024
GRID_W = 64
HEADS = 8
DK = 64
DV = 128
QK_W = HEADS * DK
V_W = HEADS * DV
KV_W = QK_W + V_W
CHUNK = 128
PAIRS = HEADS // 2
POOL_WINDOWS = (2, 4, 8, 16)
POOL_GD = 128
POOL_W = 512
COL_Q = KV_W
COL_G = COL_Q + QK_W
COL_P = COL_G + V_W
COL_GA = COL_P + POOL_W
COL_GB = COL_GA + D_MODEL
IN_W = COL_GB + D_MODEL
D_FF = 2816
FF_CW = 256
FF_NC = D_FF // FF_CW
FF_PAD = 8
N_MOD = 6
LN_EPS = 1e-6
ALPHA = 2.0 ** 0.25
POOL_HALO = 8
U_HALO = 16
MOD_ROWS = 8

VMEM_LIMIT = 60 * 1024 * 1024

MIX_RB = 256

TB_KV = 1024
TB_MIX = 1024
TB_FFN = 1024


def _dot(a, b):
    return jnp.dot(a, b, preferred_element_type=F32)


def _dot_nt(a, b):
    return lax.dot_general(a, b, (((1,), (1,)), ((), ())), preferred_element_type=F32)


def _dot_tn(a, b):
    return lax.dot_general(a, b, (((0,), (0,)), ((), ())), preferred_element_type=F32)


def _ln(x):
    mu = jnp.mean(x, axis=-1, keepdims=True)
    xc = x - mu
    var = jnp.mean(xc * xc, axis=-1, keepdims=True)
    return xc * lax.rsqrt(var + LN_EPS)


def _gelu_tanh(x):
    c = float(np.sqrt(2.0 / np.pi))
    half = 0.5 * x
    return half + half * jnp.tanh(x * (c + (c * 0.044715) * (x * x)))


def _mod_spec(k, row=None):
    if row is None:
        return pl.BlockSpec((1, 1, 1, D_MODEL), lambda b, *_: (b, k, 0, 0))
    return pl.BlockSpec((1, 1, 1, D_MODEL), lambda *_: (row, k, 0, 0))


def _const_spec(shape):
    nd = len(shape)
    return pl.BlockSpec(shape, lambda *_: (0,) * nd, pipeline_mode=pl.Buffered(1))


def _pair_diag(r):
    row = lax.broadcasted_iota(jnp.int32, (CHUNK, DV), 0)
    return jnp.where(row < DK, r[:, :DV], r[:, DV:])


def _chunk_kv(kd, v):
    outs = []
    for p in range(PAIRS):
        r = _dot_tn(kd[:, p * 128:(p + 1) * 128], v[:, p * 256:(p + 1) * 256])
        outs.append(_pair_diag(r))
    return jnp.concatenate(outs, axis=0)


def _adaln_kernel(c_ref, w_ref, b_ref, o_ref):
    c = c_ref[...]
    s = c * jax.nn.sigmoid(c)
    o_ref[...] = _dot(s, w_ref[...]) + b_ref[...]


def _adaln(cc, w_ada, b_ada):
    n = w_ada.shape[1]
    bn = 1536
    return pl.pallas_call(
        _adaln_kernel,
        grid=(n // bn,),
        in_specs=[pl.BlockSpec((MOD_ROWS, D_MODEL), lambda i: (0, 0)),
                  pl.BlockSpec((D_MODEL, bn), lambda i: (0, i)),
                  pl.BlockSpec((1, bn), lambda i: (0, i))],
        out_specs=pl.BlockSpec((MOD_ROWS, bn), lambda i: (0, i)),
        out_shape=jax.ShapeDtypeStruct((MOD_ROWS, n), F32),
        compiler_params=pltpu.CompilerParams(vmem_limit_bytes=VMEM_LIMIT),
        name="adaln",
    )(cc, w_ada, b_ada)


def _ctx_kernel(ctx_ref, sh_ref, sc_ref, wkv_ref, kdf_ref, kdb_ref, cdf_ref, cdb_ref,
                sf_ref, sb_ref, wkvb_ref):
    x = ctx_ref[0]
    u = (_ln(x) * (1.0 + sc_ref[0, 0]) + sh_ref[0, 0]).astype(BF16)
    wkv = wkv_ref[...].astype(BF16)
    wkvb_ref[...] = wkv
    kv = _dot(u, wkv)
    k = kv[:, :QK_W] * (DK ** -0.5)
    v = kv[:, QK_W:].astype(BF16)
    n = x.shape[0] // CHUNK
    sf = jnp.zeros((QK_W, DV), F32)
    for c in range(n):
        kc = k[c * CHUNK:(c + 1) * CHUNK]
        vc = v[c * CHUNK:(c + 1) * CHUNK]
        sf = cdf_ref[...] * sf + _chunk_kv((kc * kdf_ref[...]).astype(BF16), vc)
    sb = jnp.zeros((QK_W, DV), F32)
    for c in reversed(range(n)):
        kc = k[c * CHUNK:(c + 1) * CHUNK]
        vc = v[c * CHUNK:(c + 1) * CHUNK]
        sb = cdb_ref[...] * sb + _chunk_kv((kc * kdb_ref[...]).astype(BF16), vc)
    sf_ref[0] = sf
    sb_ref[0] = sb


def _ctx_states(ctx, mod, w_kv, kdf, kdb, cdf, cdb):
    B, Lc, _ = ctx.shape
    st = jax.ShapeDtypeStruct((B, QK_W, DV), F32)
    return pl.pallas_call(
        _ctx_kernel,
        grid=(B,),
        in_specs=[pl.BlockSpec((1, Lc, D_MODEL), lambda b: (b, 0, 0)),
                  _mod_spec(0, row=B), _mod_spec(1, row=B),
                  _const_spec((D_MODEL, KV_W)),
                  _const_spec((CHUNK, QK_W)), _const_spec((CHUNK, QK_W)),
                  _const_spec((QK_W, DV)), _const_spec((QK_W, DV))],
        out_specs=[pl.BlockSpec((1, QK_W, DV), lambda b: (b, 0, 0)),
                   pl.BlockSpec((1, QK_W, DV), lambda b: (b, 0, 0)),
                   pl.BlockSpec((D_MODEL, KV_W), lambda b: (0, 0))],
        out_shape=[st, st, jax.ShapeDtypeStruct((D_MODEL, KV_W), BF16)],
        compiler_params=pltpu.CompilerParams(
            dimension_semantics=("arbitrary",), vmem_limit_bytes=VMEM_LIMIT),
        name="ctx_states",
    )(ctx, mod, mod, w_kv, kdf, kdb, cdf, cdb)


def _kv_kernel(x_ref, sh_ref, sc_ref, wkv_ref, kdb_ref, cdb_ref, sb0_ref, *rest, n_cast):
    cast_in, (u_ref, kv_ref, sb_ref), cast_out, (s_ref,) = (
        rest[:n_cast], rest[n_cast:n_cast + 3], rest[n_cast + 3:2 * n_cast + 3], rest[2 * n_cast + 3:])

    @pl.when(pl.program_id(1) == 0)
    def _():
        s_ref[...] = sb0_ref[0]

    for src, dst in zip(cast_in, cast_out):
        dst[...] = src[...].astype(BF16)

    x = x_ref[0]
    u = (_ln(x) * (1.0 + sc_ref[0, 0]) + sh_ref[0, 0]).astype(BF16)
    u_ref[0] = u
    kv = _dot(u, wkv_ref[...])
    k = kv[:, :QK_W] * (DK ** -0.5)
    v = kv[:, QK_W:].astype(BF16)
    kv_ref[0, :, :QK_W] = k.astype(BF16)
    kv_ref[0, :, QK_W:] = v
    n = x.shape[0] // CHUNK
    for c in reversed(range(n)):
        s = s_ref[...]
        sb_ref[0, c] = s.astype(BF16)
        kc = k[c * CHUNK:(c + 1) * CHUNK]
        vc = v[c * CHUNK:(c + 1) * CHUNK]
        s_ref[...] = cdb_ref[...] * s + _chunk_kv((kc * kdb_ref[...]).astype(BF16), vc)


def _kv_states(x, mod, w_kv, kdb, cdb, sb0, later_weights):
    B, L, _ = x.shape
    tb = TB_KV
    nb = L // tb
    nch = tb // CHUNK
    steps = B * nb
    slab_specs = [pl.BlockSpec((w.shape[0] // steps, w.shape[1]), lambda b, j: (b * nb + j, 0))
                  for w in later_weights]
    for w in later_weights:
        assert w.shape[0] % (16 * steps) == 0, "row slabs must be whole bf16 tiles"
    return pl.pallas_call(
        functools.partial(_kv_kernel, n_cast=len(later_weights)),
        grid=(B, nb),
        in_specs=[pl.BlockSpec((1, tb, D_MODEL), lambda b, j: (b, nb - 1 - j, 0)),
                  _mod_spec(0), _mod_spec(1),
                  _const_spec((D_MODEL, KV_W)),
                  _const_spec((CHUNK, QK_W)),
                  _const_spec((QK_W, DV)),
                  pl.BlockSpec((1, QK_W, DV), lambda b, j: (b, 0, 0))] + slab_specs,
        out_specs=[pl.BlockSpec((1, tb, D_MODEL), lambda b, j: (b, nb - 1 - j, 0)),
                   pl.BlockSpec((1, tb, KV_W), lambda b, j: (b, nb - 1 - j, 0)),
                   pl.BlockSpec((1, nch, QK_W, DV), lambda b, j: (b, nb - 1 - j, 0, 0))] + slab_specs,
        out_shape=[jax.ShapeDtypeStruct((B, L, D_MODEL), BF16),
                   jax.ShapeDtypeStruct((B, L, KV_W), BF16),
                   jax.ShapeDtypeStruct((B, L // CHUNK, QK_W, DV), BF16)]
        + [jax.ShapeDtypeStruct(w.shape, BF16) for w in later_weights],
        scratch_shapes=[pltpu.VMEM((QK_W, DV), F32)],
        compiler_params=pltpu.CompilerParams(
            dimension_semantics=("arbitrary", "arbitrary"), vmem_limit_bytes=VMEM_LIMIT),
        name="kv_states",
    )(x, mod, mod, w_kv, kdb, cdb, sb0, *later_weights)


def _pool_features(pe, j, nb, tb, seq_len, poolw_ref, pscale_ref):
    n = pe.shape[0]
    r = lax.broadcasted_iota(jnp.int32, (POOL_HALO, POOL_GD), 0)
    outs = []
    for gi, w in enumerate(POOL_WINDOWS):
        half = w // 2
        a = pe[:, gi * POOL_GD:(gi + 1) * POOL_GD]
        centre = a[POOL_HALO:POOL_HALO + tb]
        s = a
        step = 1
        while step < half:
            s = s + pltpu.roll(s, n - step, axis=0)
            step *= 2
        s = (s + pltpu.roll(s, half, axis=0))[POOL_HALO:POOL_HALO + tb]
        cnt_head = (jnp.minimum(r + half, seq_len) - jnp.maximum(r - half, 0)).astype(F32)
        t_tail = seq_len - POOL_HALO + r
        cnt_tail = (jnp.minimum(t_tail + half, seq_len) - jnp.maximum(t_tail - half, 0)).astype(F32)
        head = jnp.where(j == 0, 1.0 / cnt_head, 1.0 / w)
        tail = jnp.where(j == nb - 1, 1.0 / cnt_tail, 1.0 / w)
        mean = jnp.concatenate([s[:POOL_HALO] * head, s[POOL_HALO:tb - POOL_HALO] * (1.0 / w),
                                s[tb - POOL_HALO:] * tail], axis=0)
        diff = (mean - centre).astype(BF16)
        outs.append(_dot(diff, poolw_ref[gi]))
    return jnp.concatenate(outs, axis=-1) * pscale_ref[...]


def _mixer_kernel(x_ref, u_ref, up_ref, un_ref, kv_ref, sb_ref, sf0_ref, g1_ref,
                  wr_ref, dm_ref, qdf_ref, qdb_ref, kdf_ref, cdf_ref,
                  poolw_ref, pscale_ref, wbr_ref, wbp_ref, wout_ref, lng_ref, lnb_ref,
                  o_ref, s_ref, r_ref, *, nb, seq_len):
    j = pl.program_id(1)
    tb = x_ref.shape[1]

    @pl.when(j == 0)
    def _():
        s_ref[...] = sf0_ref[0]

    lane = lax.broadcasted_iota(jnp.int32, (CHUNK, 128), 1)
    zv = jnp.zeros((CHUNK, DV), BF16)
    zs = jnp.zeros((DK, DV), BF16)

    def block_diag_state(s):
        left = jnp.concatenate([s[:DK], zs], axis=0)
        right = jnp.concatenate([zs, s[DK:]], axis=0)
        return jnp.concatenate([left, right], axis=1)

    per_rb = MIX_RB // CHUNK
    for c in range(tb // CHUNK):
        rows = slice(c * CHUNK, (c + 1) * CHUNK)
        if c % per_rb == 0:
            u_rb = u_ref[0, c * CHUNK:c * CHUNK + MIX_RB]
            q = _dot(u_rb, wr_ref[:, COL_Q:COL_G])
            g = _dot(u_rb, wr_ref[:, COL_G:COL_P])
        rows_rb = slice((c % per_rb) * CHUNK, (c % per_rb + 1) * CHUNK)
        qc = q[rows_rb]
        qb16 = qc.astype(BF16)
        qf = (qc * qdf_ref[...]).astype(BF16)
        qb = (qc * qdb_ref[...]).astype(BF16)
        kc = kv_ref[0, rows, 0:QK_W].astype(F32)
        vc = kv_ref[0, rows, QK_W:KV_W]
        kd = (kc * kdf_ref[...]).astype(BF16)
        ys = []
        for p in range(PAIRS):
            ql = slice(p * 128, (p + 1) * 128)
            kp = kc[:, ql]
            k_lo = jnp.where(lane < DK, kp, 0.0).astype(BF16)
            k_hi = jnp.where(lane >= DK, kp, 0.0).astype(BF16)
            krhs = jnp.concatenate([k_lo, k_hi], axis=0)
            sc = (_dot_nt(qb16[:, ql], krhs) * dm_ref[p]).astype(BF16)
            vp = vc[:, p * 256:(p + 1) * 256]
            vrhs = jnp.concatenate([jnp.concatenate([vp[:, :DV], zv], axis=1),
                                    jnp.concatenate([zv, vp[:, DV:]], axis=1)], axis=0)
            y = _dot(sc, vrhs)
            s_f = s_ref[ql, :]
            srhs = jnp.concatenate([block_diag_state(s_f.astype(BF16)),
                                    block_diag_state(sb_ref[0, c, ql, :])], axis=0)
            qlhs = jnp.concatenate([qf[:, ql], qb[:, ql]], axis=1)
            y = y + _dot(qlhs, srhs)
            ys.append(y)
            s_ref[ql, :] = cdf_ref[ql, :] * s_f + _pair_diag(_dot_tn(kd[:, ql], vp))
        gc = g[rows_rb]
        sg = gc * jax.nn.sigmoid(gc)
        for p in range(PAIRS):
            for hh in range(2):
                h = 2 * p + hh
                yh = ys[p][:, hh * DV:(hh + 1) * DV]
                mu = jnp.mean(yh, axis=-1, keepdims=True)
                yc = yh - mu
                var = jnp.mean(yc * yc, axis=-1, keepdims=True)
                yn = yc * lax.rsqrt(var + LN_EPS)
                r_ref[rows, h * DV:(h + 1) * DV] = (yn * sg[:, h * DV:(h + 1) * DV]).astype(BF16)

    ph = _dot(jnp.concatenate([up_ref[0], un_ref[0]], axis=0), wr_ref[:, COL_P:COL_GA])
    p_prev = ph[U_HALO - POOL_HALO:U_HALO] * (j > 0).astype(F32)
    p_next = ph[U_HALO:U_HALO + POOL_HALO] * (j < nb - 1).astype(F32)
    pm = _dot(u_ref[0], wr_ref[:, COL_P:COL_GA])
    pe = jnp.concatenate([p_prev, pm, p_next], axis=0)
    feat = _pool_features(pe, j, nb, tb, seq_len, poolw_ref, pscale_ref).astype(BF16)

    for r0 in range(0, tb, MIX_RB):
        rs = slice(r0, r0 + MIX_RB)
        u_rb = u_ref[0, rs]
        ret = _dot(r_ref[rs, :], wbr_ref[...])
        pool = _dot(feat[rs], wbp_ref[...])
        merged = jax.nn.sigmoid(_dot(u_rb, wr_ref[:, COL_GA:COL_GB])) * ret
        merged = (merged + jax.nn.sigmoid(_dot(u_rb, wr_ref[:, COL_GB:IN_W])) * pool).astype(BF16)
        z = ALPHA * x_ref[0, rs] + g1_ref[0, 0] * _dot(merged, wout_ref[...])
        o_ref[0, rs] = _ln(z) * lng_ref[...] + lnb_ref[...]


def _mixer(x, u, kv, sb, sf0, mod, w_rest, dm, qdf, qdb, kdf, cdf,
           pool_w, pool_scale, w_br, w_bp, w_out, ln_g, ln_b):
    B, L, _ = x.shape
    tb = TB_MIX
    nb = L // tb
    nch = tb // CHUNK
    hb = tb // U_HALO
    nh = L // U_HALO
    return pl.pallas_call(
        functools.partial(_mixer_kernel, nb=nb, seq_len=L),
        grid=(B, nb),
        in_specs=[pl.BlockSpec((1, tb, D_MODEL), lambda b, j: (b, j, 0)),
                  pl.BlockSpec((1, tb, D_MODEL), lambda b, j: (b, j, 0)),
                  pl.BlockSpec((1, U_HALO, D_MODEL), lambda b, j: (b, jnp.maximum(j * hb - 1, 0), 0)),
                  pl.BlockSpec((1, U_HALO, D_MODEL),
                               lambda b, j: (b, jnp.minimum((j + 1) * hb, nh - 1), 0)),
                  pl.BlockSpec((1, tb, KV_W), lambda b, j: (b, j, 0)),
                  pl.BlockSpec((1, nch, QK_W, DV), lambda b, j: (b, j, 0, 0)),
                  pl.BlockSpec((1, QK_W, DV), lambda b, j: (b, 0, 0)),
                  _mod_spec(2),
                  _const_spec((D_MODEL, IN_W)),
                  _const_spec((PAIRS, CHUNK, 256)),
                  _const_spec((CHUNK, QK_W)), _const_spec((CHUNK, QK_W)), _const_spec((CHUNK, QK_W)),
                  _const_spec((QK_W, DV)),
                  _const_spec((len(POOL_WINDOWS), POOL_GD, POOL_GD)),
                  _const_spec((1, POOL_W)),
                  _const_spec((V_W, D_MODEL)),
                  _const_spec((POOL_W, D_MODEL)),
                  _const_spec((D_MODEL, D_MODEL)),
                  _const_spec((1, D_MODEL)), _const_spec((1, D_MODEL))],
        out_specs=pl.BlockSpec((1, tb, D_MODEL), lambda b, j: (b, j, 0)),
        out_shape=jax.ShapeDtypeStruct((B, L, D_MODEL), F32),
        scratch_shapes=[pltpu.VMEM((QK_W, DV), F32), pltpu.VMEM((tb, V_W), BF16)],
        compiler_params=pltpu.CompilerParams(
            dimension_semantics=("arbitrary", "arbitrary"), vmem_limit_bytes=VMEM_LIMIT),
        name="mixer",
    )(x, u, u, u, kv, sb, sf0, mod, w_rest, dm, qdf, qdb, kdf, cdf,
      pool_w, pool_scale, w_br, w_bp, w_out, ln_g, ln_b)


def _ffn_kernel(x_ref, xn_ref, sh_ref, sc_ref, g2_ref, wup_ref, cw_ref, wdn_ref,
                lng_ref, lnb_ref, o_ref, u_ref, acc_ref, ha_ref, hb_ref, top_ref, *, nb):
    j = pl.program_id(1)
    tb = x_ref.shape[1]
    m = tb + GRID_W
    n = tb + 2 * GRID_W
    scale = 1.0 + sc_ref[0, 0]
    shift = sh_ref[0, 0]

    def mod(v):
        return _ln(v) * scale + shift

    @pl.when(j == 0)
    def _():
        top_ref[...] = jnp.zeros_like(top_ref)

    u_ref[0:tb] = mod(x_ref[0]).astype(BF16)
    u_ref[tb:m] = (mod(xn_ref[0]) * (j < nb - 1).astype(F32)).astype(BF16)

    col = lax.broadcasted_iota(jnp.int32, (tb, FF_CW), 0) & (GRID_W - 1)
    has_left = col > 0
    has_right = col < GRID_W - 1

    pad = jnp.zeros((FF_PAD, FF_CW), F32)
    for h_ref in (ha_ref, hb_ref):
        for ab in range(2):
            h_ref[ab, 0:FF_PAD] = pad
            h_ref[ab, FF_PAD + n:FF_PAD + n + FF_PAD] = pad

    def lanes(c, half):
        return pl.ds(pl.multiple_of(half * D_FF + c * FF_CW, FF_CW), FF_CW)

    def up(c, h_ref):
        u = u_ref[...]
        for half in range(2):
            h = _dot(u, wup_ref[:, lanes(c, half)])
            h_ref[half, FF_PAD:FF_PAD + GRID_W] = top_ref[half, c]
            h_ref[half, FF_PAD + GRID_W:FF_PAD + n] = h
            top_ref[half, c] = h[tb - GRID_W:tb]

    def conv(h_ref, ab, cw):
        cols = []
        for dc in range(3):
            hs = h_ref[ab, FF_PAD + dc - 1:FF_PAD + dc - 1 + n].astype(BF16)
            g = None
            for dr in range(3):
                term = cw[3 * dr + dc:3 * dr + dc + 1] * hs[dr * GRID_W:dr * GRID_W + tb]
                g = term if g is None else g + term
            cols.append(g)
        zero = jnp.zeros_like(cols[1])
        return (cols[1] + cw[9:10]) + (jnp.where(has_left, cols[0], zero) + jnp.where(has_right, cols[2], zero))

    def down(c, h_ref):
        a = conv(h_ref, 0, cw_ref[:, lanes(c, 0)])
        b = conv(h_ref, 1, cw_ref[:, lanes(c, 1)])
        acc_ref[...] += _dot(_gelu_tanh(a.astype(F32)).astype(BF16) * b, wdn_ref[c])

    acc_ref[...] = jnp.zeros_like(acc_ref)
    up(0, ha_ref)

    def body(i, carry):
        c = 2 * i
        up(c + 1, hb_ref)
        down(c, ha_ref)
        up(c + 2, ha_ref)
        down(c + 1, hb_ref)
        return carry

    lax.fori_loop(0, FF_NC // 2, body, 0)
    down(FF_NC - 1, ha_ref)
    z = ALPHA * x_ref[0] + g2_ref[0, 0] * acc_ref[...]
    o_ref[0] = _ln(z) * lng_ref[...] + lnb_ref[...]


def _ffn(x, mod, w_up, conv_wb, w_down, ln_g, ln_b):
    B, L, _ = x.shape
    tb = TB_FFN
    nb = L // tb
    hb = tb // GRID_W
    nh = L // GRID_W
    return pl.pallas_call(
        functools.partial(_ffn_kernel, nb=nb),
        grid=(B, nb),
        in_specs=[pl.BlockSpec((1, tb, D_MODEL), lambda b, j: (b, j, 0)),
                  pl.BlockSpec((1, GRID_W, D_MODEL),
                               lambda b, j: (b, jnp.minimum((j + 1) * hb, nh - 1), 0)),
                  _mod_spec(3), _mod_spec(4), _mod_spec(5),
                  _const_spec((D_MODEL, 2 * D_FF)),
                  _const_spec((16, 2 * D_FF)),
                  _const_spec((FF_NC, FF_CW, D_MODEL)),
                  _const_spec((1, D_MODEL)), _const_spec((1, D_MODEL))],
        out_specs=pl.BlockSpec((1, tb, D_MODEL), lambda b, j: (b, j, 0)),
        out_shape=jax.ShapeDtypeStruct((B, L, D_MODEL), F32),
        scratch_shapes=[pltpu.VMEM((tb + GRID_W, D_MODEL), BF16),
                        pltpu.VMEM((tb, D_MODEL), F32),
                        pltpu.VMEM((2, tb + 2 * GRID_W + 2 * FF_PAD, FF_CW), F32),
                        pltpu.VMEM((2, tb + 2 * GRID_W + 2 * FF_PAD, FF_CW), F32),
                        pltpu.VMEM((2, FF_NC, GRID_W, FF_CW), F32)],
        compiler_params=pltpu.CompilerParams(
            dimension_semantics=("arbitrary", "arbitrary"), vmem_limit_bytes=VMEM_LIMIT),
        name="conv_ffn",
    )(x, x, mod, mod, mod, w_up, conv_wb, w_down, ln_g, ln_b)


def _decay_tables(ret_decay_logit):
    lg = jax.nn.log_sigmoid(ret_decay_logit.astype(F32))
    pos = jnp.arange(CHUNK, dtype=F32)
    diff = pos[:, None] - pos[None, :]
    d_f = jnp.where(diff[None] >= 0, jnp.exp(jnp.maximum(diff, 0.0)[None] * lg[0][:, None, None]), 0.0)
    d_b = jnp.where(diff[None] <= 0, jnp.exp(jnp.maximum(-diff, 0.0)[None] * lg[1][:, None, None]), 0.0)
    dm = (d_f + d_b).reshape(PAIRS, 2, CHUNK, CHUNK).transpose(0, 2, 1, 3).reshape(PAIRS, CHUNK, 2 * CHUNK)

    def lanes(t):
        return jnp.repeat(t, DK, axis=1)

    qdf = lanes(jnp.exp((pos + 1.0)[:, None] * lg[0][None, :]))
    qdb = lanes(jnp.exp((CHUNK - pos)[:, None] * lg[1][None, :]))
    kdf = lanes(jnp.exp((CHUNK - 1.0 - pos)[:, None] * lg[0][None, :]))
    kdb = lanes(jnp.exp(pos[:, None] * lg[1][None, :]))

    def rows(t):
        return jnp.broadcast_to(jnp.repeat(t, DK)[:, None], (QK_W, DV))

    cdf = rows(jnp.exp(CHUNK * lg[0]))
    cdb = rows(jnp.exp(CHUNK * lg[1]))
    return dm, qdf, qdb, kdf, kdb, cdf, cdb


def kernel(x, c, ctx, c_ctx, w_ada, b_ada, w_in, ret_decay_logit, pool_w, pool_scale, w_branch_ret,
           w_branch_pool, w_out, ln1_g, ln1_b, w_up, conv_w, conv_b, w_down, ln2_g, ln2_b):
    B = x.shape[0]
    D = D_MODEL
    assert w_ada.shape[0] == 1, "single-layer stack"

    cc = jnp.concatenate([c, c_ctx[None, :], jnp.zeros((MOD_ROWS - B - 1, D), F32)], axis=0)
    mod = _adaln(cc, w_ada[0], b_ada[0][None, :]).reshape(MOD_ROWS, N_MOD, 1, D)

    dm, qdf, qdb, kdf, kdb, cdf, cdb = _decay_tables(ret_decay_logit[0])

    s_f, s_b, w_kv_b = _ctx_states(ctx, mod, w_in[0], kdf, kdb, cdf, cdb)
    u, kv, sb, w_in_b, w_up_b, w_down_b, w_br_b, w_bp_b, w_out_b = _kv_states(
        x, mod, w_kv_b, kdb, cdb, s_b,
        [w_in[0], w_up[0], w_down[0], w_branch_ret[0], w_branch_pool[0], w_out[0]])
    x1 = _mixer(x, u, kv, sb, s_f, mod, w_in_b, dm, qdf, qdb, kdf, cdf,
                pool_w[0].astype(BF16), pool_scale[0][None, :], w_br_b, w_bp_b, w_out_b,
                ln1_g[0][None, :], ln1_b[0][None, :])

    conv_wb = jnp.concatenate([conv_w[0].reshape(9, 2 * D_FF), conv_b[0][None, :],
                               jnp.zeros((6, 2 * D_FF), F32)], axis=0).astype(BF16)
    return _ffn(x1, mod, w_up_b, conv_wb, w_down_b.reshape(FF_NC, FF_CW, D),
                ln2_g[0][None, :], ln2_b[0][None, :])
```
